```python
import math
import numpy as np
import jax
import jax.numpy as jnp
from jax import lax

D_MODEL = 4096
BATCH = 8
SEQ = 2048
DEPTH = 4

CTX_LEN = 256
GRID_W = 64
Q_BLOCK = 128
RET_CHUNK = 128
ROPE_BASE = 10000.0
EPS = 1e-6

HEAD_DIM = 128
DIFF_HEADS = D_MODEL // 512
DIFF_HALF = HEAD_DIM // 2
DIFF_WIDTH = DIFF_HEADS * HEAD_DIM
RET_HEADS = D_MODEL // 512
RET_DK = 128
RET_DV = 128
RET_WIDTH = RET_HEADS * RET_DV
RET_DECAY_BASE = 5.0
MLA_HEADS = D_MODEL // 256
MLA_Q_RANK = D_MODEL // 4
MLA_KV_RANK = D_MODEL // 8
MLA_NOPE = 128
MLA_ROPE = 64
MLA_V = 128
MLA_WIDTH = MLA_HEADS * MLA_V
MIX_WIDTH = DIFF_WIDTH + RET_WIDTH + MLA_WIDTH

IN_SPLITS = (DIFF_WIDTH, DIFF_WIDTH, DIFF_WIDTH,
             RET_HEADS * RET_DK, RET_HEADS * RET_DK, RET_WIDTH, RET_WIDTH,
             MLA_Q_RANK, MLA_KV_RANK, MLA_ROPE)
IN_WIDTH = sum(IN_SPLITS)
IN_BOUNDS = tuple(zip(np.cumsum((0,) + IN_SPLITS[:-1]).tolist(), np.cumsum(IN_SPLITS).tolist()))

N_EXPERTS = 32
TOP_K = 4
EXPERT_FF = 128
SWIGLU_LIMIT = 7.0
SWIGLU_ALPHA = 1.702

kernel_name = "hymba_style_diff_ret_mla_moe_dit"


def rms_norm(x, g, eps=EPS):
    xf = x.astype(jnp.float32)
    y = xf * lax.rsqrt(jnp.mean(xf * xf, axis=-1, keepdims=True) + eps)
    return (y * g.astype(jnp.float32)).astype(x.dtype)


def head_layer_norm(x, g, eps=1e-5):
    xf = x.astype(jnp.float32)
    mu = jnp.mean(xf, axis=-1, keepdims=True)
    xc = xf - mu
    y = xc * lax.rsqrt(jnp.mean(xc * xc, axis=-1, keepdims=True) + eps)
    return y * g.astype(jnp.float32).reshape(x.shape[-2:])


def axial_rope_tables(n, rot_dim):
    rows = n // GRID_W
    row = jnp.repeat(jnp.arange(rows, dtype=jnp.float32), GRID_W)
    col = jnp.tile(jnp.arange(GRID_W, dtype=jnp.float32), rows)
    pairs = rot_dim // 4
    inv = ROPE_BASE ** (-jnp.arange(pairs, dtype=jnp.float32) / pairs)
    ang = jnp.concatenate([row[:, None] * inv, col[:, None] * inv], axis=-1)
    return jnp.cos(ang), jnp.sin(ang)


def apply_rope(x, cos, sin):
    bshape = (cos.shape[0],) + (1,) * (x.ndim - 3) + (cos.shape[1],)
    c = cos.reshape(bshape)
    s = sin.reshape(bshape)
    x1, x2 = jnp.split(x.astype(jnp.float32), 2, axis=-1)
    return jnp.concatenate([x1 * c - x2 * s, x2 * c + x1 * s], axis=-1).astype(x.dtype)


def map_query_blocks(fn, *qs):
    b, n = qs[0].shape[:2]
    nb = n // Q_BLOCK
    blocks = tuple(jnp.moveaxis(q.reshape((b, nb, Q_BLOCK) + q.shape[2:]), 1, 0) for q in qs)
    out = lax.map(lambda qb: fn(*qb), blocks)
    out = jnp.moveaxis(out, 0, 1)
    return out.reshape((b, n) + out.shape[3:])


def diff_attend(q, k, v, lam):
    scale = DIFF_HALF ** -0.5

    def block(qb):
        s = jnp.einsum('bqhcd,bkhcd->bhcqk', qb, k, preferred_element_type=jnp.float32) * scale
        p = jax.nn.softmax(s, axis=-1)
        p = p[:, :, 0] - lam * p[:, :, 1]
        return jnp.einsum('bhqk,bkhd->bqhd', p, v)

    return map_query_blocks(block, q).astype(v.dtype)


def diff_mixer(pc, px, rope, lam, subln, lam_init, need_ctx):
    cos, sin = rope

    def heads(q, k, v):
        b, n, _ = q.shape
        return (q.reshape(b, n, DIFF_HEADS, 2, DIFF_HALF),
                k.reshape(b, n, DIFF_HEADS, 2, DIFF_HALF),
                v.reshape(b, n, DIFF_HEADS, HEAD_DIM))

    qc, kc, vc = heads(*pc)
    qx, kx, vx = heads(*px)
    qx = apply_rope(qx, cos, sin)
    kx = apply_rope(kx, cos, sin)

    def post(o):
        b, n = o.shape[:2]
        return (rms_norm(o, subln) * (1.0 - lam_init)).reshape(b, n, DIFF_WIDTH)

    out_x = post(diff_attend(qx, jnp.concatenate([kc, kx], axis=1),
                             jnp.concatenate([vc, vx], axis=1), lam))
    out_c = post(diff_attend(qc, kc, vc, lam)) if need_ctx else None
    return out_x, out_c


def retention_scan(q, k, v, log_gamma, s0):
    b, n, h, _ = q.shape
    dv = v.shape[-1]
    nc = n // RET_CHUNK

    def to_chunks(t):
        return jnp.moveaxis(t.reshape(b, nc, RET_CHUNK, h, t.shape[-1]), 1, 0)

    i = jnp.arange(RET_CHUNK, dtype=jnp.float32)
    rel = i[:, None] - i[None, :]
    intra = jnp.where(rel >= 0, jnp.exp(jnp.maximum(rel, 0.0)[None] * log_gamma[:, None, None]), 0.0)
    q_dec = jnp.exp((i + 1.0)[:, None] * log_gamma[None, :])
    k_dec = jnp.exp((RET_CHUNK - 1.0 - i)[:, None] * log_gamma[None, :])
    c_dec = jnp.exp(RET_CHUNK * log_gamma)

    def step(s, chunk):
        qc, kc, vc = chunk
        att = jnp.einsum('bihd,bjhd->bhij', qc, kc) * intra
        o = (jnp.einsum('bhij,bjhe->bihe', att, vc)
             + jnp.einsum('bihd,bhde->bihe', qc, s) * q_dec[None, :, :, None])
        s = s * c_dec[None, :, None, None] + jnp.einsum('bjhd,bjhe->bhde', kc * k_dec[None, :, :, None], vc)
        return s, o

    s, o = lax.scan(step, s0, (to_chunks(q), to_chunks(k), to_chunks(v)))
    return s, jnp.moveaxis(o, 0, 1).reshape(b, n, h, dv)


def retention_mixer(pc, px, rope, decay_scale, gn_gain, need_ctx):
    cos, sin = rope
    dtype = px[0].dtype

    def heads(q, k, v, g):
        b, n, _ = q.shape
        f = lambda t, d: t.reshape(b, n, RET_HEADS, d).astype(jnp.float32)
        return f(q, RET_DK), f(k, RET_DK) * (RET_DK ** -0.5), f(v, RET_DV), f(g, RET_DV)

    qc, kc, vc, gc = heads(*pc)
    qx, kx, vx, gx = heads(*px)
    qx = apply_rope(qx, cos, sin)
    kx = apply_rope(kx, cos, sin)

    log_gamma = jnp.log1p(-jnp.exp2(-decay_scale.astype(jnp.float32)))
    flip = lambda t: jnp.flip(t, axis=1)
    s0 = jnp.zeros((qx.shape[0], RET_HEADS, RET_DK, RET_DV), jnp.float32)

    s_cf, o_cf = retention_scan(qc, kc, vc, log_gamma[0], s0)
    s_cb, o_cb = retention_scan(flip(qc), flip(kc), flip(vc), log_gamma[1], s0)
    _, o_xf = retention_scan(qx, kx, vx, log_gamma[0], s_cf)
    _, o_xb = retention_scan(flip(qx), flip(kx), flip(vx), log_gamma[1], s_cb)

    def post(o, g):
        b, n = o.shape[:2]
        return (head_layer_norm(o, gn_gain) * jax.nn.silu(g)).reshape(b, n, RET_WIDTH).astype(dtype)

    out_x = post(o_xf + flip(o_xb), gx)
    out_c = post(o_cf + flip(o_cb), gc) if need_ctx else None
    return out_x, out_c


def mla_attend(qn, qr, kn, kr, v):
    scale = (MLA_NOPE + MLA_ROPE) ** -0.5

    def block(qnb, qrb):
        s = (jnp.einsum('bqhd,bkhd->bhqk', qnb, kn, preferred_element_type=jnp.float32)
             + jnp.einsum('bqhr,bkr->bhqk', qrb, kr, preferred_element_type=jnp.float32)) * scale
        p = jax.nn.softmax(s, axis=-1)
        return jnp.einsum('bhqk,bkhd->bqhd', p, v)

    return map_query_blocks(block, qn, qr).astype(v.dtype)


def mla_mixer(pc, px, rope, q_norm, kv_norm, wq_b, wkv_b, need_ctx):
    cos, sin = rope

    def project(q_a, kv_a, k_r):
        b, n, _ = q_a.shape
        q = (rms_norm(q_a, q_norm) @ wq_b).reshape(b, n, MLA_HEADS, MLA_NOPE + MLA_ROPE)
        kv = (rms_norm(kv_a, kv_norm) @ wkv_b).reshape(b, n, MLA_HEADS, MLA_NOPE + MLA_V)
        return q[..., :MLA_NOPE], q[..., MLA_NOPE:], kv[..., :MLA_NOPE], kv[..., MLA_NOPE:], k_r

    qn_c, qr_c, kn_c, v_c, kr_c = project(*pc)
    qn_x, qr_x, kn_x, v_x, kr_x = project(*px)
    qr_x = apply_rope(qr_x, cos, sin)
    kr_x = apply_rope(kr_x, cos, sin)

    b, n = qn_x.shape[:2]
    out_x = mla_attend(qn_x, qr_x, jnp.concatenate([kn_c, kn_x], axis=1),
                       jnp.concatenate([kr_c, kr_x], axis=1),
                       jnp.concatenate([v_c, v_x], axis=1)).reshape(b, n, MLA_WIDTH)
    out_c = None
    if need_ctx:
        out_c = mla_attend(qn_c, qr_c, kn_c, kr_c, v_c).reshape(b, qn_c.shape[1], MLA_WIDTH)
    return out_x, out_c


def moe_ffn(h, router_w, router_b, w_gu, b_gu, w_down, b_down):
    b, n, d = h.shape
    t = h.reshape(b * n, d)
    logits = (t @ router_w + router_b).astype(jnp.float32)
    top_v, top_i = lax.top_k(logits, TOP_K)
    top_w = jax.nn.softmax(top_v, axis=-1)
    gates = jnp.einsum('tk,tke->te', top_w, jax.nn.one_hot(top_i, N_EXPERTS, dtype=jnp.float32))
    gu = jnp.einsum('td,edf->tef', t, w_gu) + b_gu
    glu = jnp.minimum(gu[..., :EXPERT_FF], SWIGLU_LIMIT)
    lin = jnp.clip(gu[..., EXPERT_FF:], -SWIGLU_LIMIT, SWIGLU_LIMIT)
    act = glu * jax.nn.sigmoid(SWIGLU_ALPHA * glu) * (lin + 1.0)
    g = gates.astype(act.dtype)
    y = jnp.einsum('tef,efd->td', act * g[..., None], w_down) + g @ b_down
    return y.reshape(b, n, d).astype(h.dtype)


def trunk_layer(x, ctx, cs, ccs, ada_w, ada_b, norm1, norm2, w_in,
                diff_lq1, diff_lk1, diff_lq2, diff_lk2, diff_subln,
                ret_decay, ret_norm, mla_q_norm, mla_kv_norm, mla_wq_b, mla_wkv_b,
                w_out, router_w, router_b, exp_w_gu, exp_b_gu, exp_w_down, exp_b_down,
                lam_init, rope64, rope128, need_ctx):
    sh1x, sc1x, g1x, sh2x, sc2x, g2x = [m[:, None, :] for m in jnp.split(cs @ ada_w + ada_b, 6, axis=-1)]
    sh1c, sc1c, g1c, sh2c, sc2c, g2c = jnp.split(ccs @ ada_w + ada_b, 6, axis=-1)

    hx = rms_norm(x, norm1) * (1.0 + sc1x) + sh1x
    hc = rms_norm(ctx, norm1) * (1.0 + sc1c) + sh1c
    px = [hx @ w_in[:, s:e] for s, e in IN_BOUNDS]
    pc = [hc @ w_in[:, s:e] for s, e in IN_BOUNDS]

    f32 = jnp.float32
    lam = (jnp.exp(jnp.sum(diff_lq1.astype(f32) * diff_lk1.astype(f32)))
           - jnp.exp(jnp.sum(diff_lq2.astype(f32) * diff_lk2.astype(f32))) + lam_init)

    a_x, a_c = diff_mixer(pc[0:3], px[0:3], rope64, lam, diff_subln, lam_init, need_ctx)
    b_x, b_c = retention_mixer(pc[3:7], px[3:7], rope128, ret_decay, ret_norm, need_ctx)
    c_x, c_c = mla_mixer(pc[7:10], px[7:10], rope64, mla_q_norm, mla_kv_norm, mla_wq_b, mla_wkv_b, need_ctx)

    x = x + g1x * (jnp.concatenate([a_x, b_x, c_x], axis=-1) @ w_out)
    hx2 = rms_norm(x, norm2) * (1.0 + sc2x) + sh2x
    x = x + g2x * moe_ffn(hx2, router_w, router_b, exp_w_gu, exp_b_gu, exp_w_down, exp_b_down)

    if need_ctx:
        ctx = ctx + g1c * (jnp.concatenate([a_c, b_c, c_c], axis=-1) @ w_out)
        hc2 = rms_norm(ctx, norm2) * (1.0 + sc2c) + sh2c
        ctx = ctx + g2c * moe_ffn(hc2, router_w, router_b, exp_w_gu, exp_b_gu, exp_w_down, exp_b_down)
    return x, ctx


def setup_inputs(seed: int = 0) -> dict:
    key = jax.random.key(seed)
    ks = iter(jax.random.split(key, 40))
    L, D = DEPTH, D_MODEL

    def nrm(shape, scale):
        return jax.random.normal(next(ks), shape, jnp.float32) * scale

    return {
        "x": nrm((BATCH, SEQ, D), 1.0),
        "c": nrm((BATCH, D), 1.0),
        "ctx": nrm((BATCH, CTX_LEN, D), 1.0),
        "c_ctx": nrm((D,), 1.0),
        "ada_w": nrm((L, D, 6 * D), 0.5 * D ** -0.5),
        "ada_b": nrm((L, 6 * D), 0.02),
        "norm1": 1.0 + nrm((L, D), 0.02),
        "norm2": 1.0 + nrm((L, D), 0.02),
        "w_in": nrm((L, D, IN_WIDTH), D ** -0.5),
        "diff_lq1": nrm((L, DIFF_HALF), 0.1),
        "diff_lk1": nrm((L, DIFF_HALF), 0.1),
        "diff_lq2": nrm((L, DIFF_HALF), 0.1),
        "diff_lk2": nrm((L, DIFF_HALF), 0.1),
        "diff_subln": 1.0 + nrm((L, HEAD_DIM), 0.02),
        "ret_decay": RET_DECAY_BASE + jnp.arange(RET_HEADS, dtype=jnp.float32) + nrm((L, 2, RET_HEADS), 0.1),
        "ret_norm": 1.0 + nrm((L, RET_WIDTH), 0.02),
        "mla_q_norm": 1.0 + nrm((L, MLA_Q_RANK), 0.02),
        "mla_kv_norm": 1.0 + nrm((L, MLA_KV_RANK), 0.02),
        "mla_wq_b": nrm((L, MLA_Q_RANK, MLA_HEADS * (MLA_NOPE + MLA_ROPE)), MLA_Q_RANK ** -0.5),
        "mla_wkv_b": nrm((L, MLA_KV_RANK, MLA_HEADS * (MLA_NOPE + MLA_V)), MLA_KV_RANK ** -0.5),
        "w_out": nrm((L, MIX_WIDTH, D), MIX_WIDTH ** -0.5),
        "router_w": nrm((L, D, N_EXPERTS), D ** -0.5),
        "router_b": nrm((L, N_EXPERTS), 0.01),
        "exp_w_gu": nrm((L, N_EXPERTS, D, 2 * EXPERT_FF), D ** -0.5),
        "exp_b_gu": nrm((L, N_EXPERTS, 2 * EXPERT_FF), 0.02),
        "exp_w_down": nrm((L, N_EXPERTS, EXPERT_FF, D), EXPERT_FF ** -0.5),
        "exp_b_down": nrm((L, N_EXPERTS, D), 0.02),
        "final_norm": 1.0 + nrm((D,), 0.02),
    }


def reference(x, c, ctx, c_ctx, ada_w, ada_b, norm1, norm2, w_in,
              diff_lq1, diff_lk1, diff_lq2, diff_lk2, diff_subln,
              ret_decay, ret_norm, mla_q_norm, mla_kv_norm, mla_wq_b, mla_wkv_b,
              w_out, router_w, router_b, exp_w_gu, exp_b_gu, exp_w_down, exp_b_down,
              final_norm):
    cs = jax.nn.silu(c)
    ccs = jax.nn.silu(c_ctx)
    n = x.shape[1]
    rope64 = axial_rope_tables(n, DIFF_HALF)
    rope128 = axial_rope_tables(n, RET_DK)
    for l in range(DEPTH):
        lam_init = 0.8 - 0.6 * math.exp(-0.3 * l)
        x, ctx = trunk_layer(
            x, ctx, cs, ccs, ada_w[l], ada_b[l], norm1[l], norm2[l], w_in[l],
            diff_lq1[l], diff_lk1[l], diff_lq2[l], diff_lk2[l], diff_subln[l],
            ret_decay[l], ret_norm[l], mla_q_norm[l], mla_kv_norm[l], mla_wq_b[l], mla_wkv_b[l],
            w_out[l], router_w[l], router_b[l], exp_w_gu[l], exp_b_gu[l], exp_w_down[l], exp_b_down[l],
            lam_init, rope64, rope128, l < DEPTH - 1)
    return rms_norm(x, final_norm)
```

```python
import functools
import math

import jax
import jax.numpy as jnp
from jax import lax
from jax.experimental import pallas as pl
from jax.experimental.pallas import tpu as pltpu

F32 = jnp.float32
BF16 = jnp.bfloat16

D_MODEL = 4096
GRID_W = 64
RET_CHUNK = 128
ROPE_BASE = 10000.0
EPS = 1e-6
HEAD_DIM = 128
DIFF_HEADS = 8
DIFF_HALF = 64
DIFF_WIDTH = 1024
RET_HEADS = 8
RET_DK = 128
RET_WIDTH = 1024
MLA_HEADS = 16
MLA_Q_RANK = 1024
MLA_KV_RANK = 512
MLA_NOPE = 128
MLA_ROPE = 64
MLA_V = 128
MLA_WIDTH = 2048
MLA_QK_PAD = 256
N_EXPERTS = 32
TOP_K = 4
EXPERT_FF = 128
SWIGLU_LIMIT = 7.0
SWIGLU_ALPHA = 1.702
MAIN_WIDTH = 8192
TAIL_WIDTH = 640
SEG = 1024

LANES = 128
VMEM_LIMIT_BYTES = 56 * 1024 * 1024
NEG_BIG = -1e30
CTX_MOD_ROW = 8
MOD_ROWS = 16


def _cparams(sem):
    return pltpu.CompilerParams(dimension_semantics=sem, vmem_limit_bytes=VMEM_LIMIT_BYTES)


def _pick_tile(total, candidates, at_least=1):
    for c in candidates:
        if total % c == 0 and c >= at_least:
            return c
    raise ValueError(f"no tile for {total}")


def _ada_kernel(c_ref, w_ref, b_ref, o_ref):
    c = c_ref[...]
    cs = (c * jax.nn.sigmoid(c)).astype(BF16)
    o_ref[...] = jnp.dot(cs, w_ref[...].astype(BF16), preferred_element_type=F32) + b_ref[...]


def _stack_conditioning(c, c_ctx):
    batch, d = c.shape
    return jnp.concatenate([c, jnp.zeros((CTX_MOD_ROW - batch, d), F32), c_ctx[None, :],
                            jnp.zeros((MOD_ROWS - CTX_MOD_ROW - 1, d), F32)], axis=0)


def _ada_tables(c_all, ada_w, ada_b):
    depth, d, n6 = ada_w.shape
    tn = 512
    return pl.pallas_call(
        _ada_kernel,
        grid=(depth, n6 // tn),
        in_specs=[
            pl.BlockSpec((MOD_ROWS, d), lambda l, j: (0, 0)),
            pl.BlockSpec((None, d, tn), lambda l, j: (l, 0, j)),
            pl.BlockSpec((None, 1, tn), lambda l, j: (l, 0, j)),
        ],
        out_specs=pl.BlockSpec((None, MOD_ROWS, tn), lambda l, j: (l, 0, j)),
        out_shape=jax.ShapeDtypeStruct((depth, MOD_ROWS, n6), F32),
        compiler_params=_cparams(("arbitrary", "arbitrary")),
        name="ada_tables",
    )(c_all, ada_w, ada_b.reshape(depth, 1, n6))


def _mod_norm(x, g_ref, sh_ref, sc_ref, row):
    ms = jnp.mean(x * x, axis=-1, keepdims=True)
    y = x * lax.rsqrt(ms + EPS) * g_ref[...]
    return y * (1.0 + sc_ref[pl.ds(row, 1), :]) + sh_ref[pl.ds(row, 1), :]


def _norm_kernel(x_ref, g_ref, sh_ref, sc_ref, o_ref, *, n_ctx_tiles):
    row = jnp.where(pl.program_id(1) < n_ctx_tiles, CTX_MOD_ROW, pl.program_id(0))
    o_ref[...] = _mod_norm(x_ref[...], g_ref, sh_ref, sc_ref, row).astype(BF16)


def _norm_router_kernel(x_ref, g_ref, sh_ref, sc_ref, rw_ref, rb_ref, o_ref, gate_ref, *, n_ctx_tiles):
    row = jnp.where(pl.program_id(1) < n_ctx_tiles, CTX_MOD_ROW, pl.program_id(0))
    h = _mod_norm(x_ref[...], g_ref, sh_ref, sc_ref, row).astype(BF16)
    o_ref[...] = h
    logits = jnp.dot(h, rw_ref[...], preferred_element_type=F32) + rb_ref[...]
    lane = lax.broadcasted_iota(jnp.int32, logits.shape, 1)
    work = logits
    num = jnp.zeros_like(logits)
    den = jnp.zeros((logits.shape[0], 1), F32)
    m0 = None
    for k in range(TOP_K):
        m = jnp.max(work, axis=-1, keepdims=True)
        idx = jnp.min(jnp.where(work == m, lane, LANES), axis=-1, keepdims=True)
        pick = lane == idx
        if k == 0:
            m0 = m
        e = jnp.exp(m - m0)
        num = num + jnp.where(pick, e, 0.0)
        den = den + e
        work = jnp.where(pick, -jnp.inf, work)
    gate_ref[...] = num / den


def _norm_call(xs, gain, mods_l, shift_blk, scale_blk, n_ctx_tiles, tr, router=None):
    b, s, d = xs.shape
    grid = (b, s // tr)
    x_spec = pl.BlockSpec((None, tr, d), lambda bi, i: (bi, i, 0))
    g_spec = pl.BlockSpec((1, d), lambda bi, i: (0, 0))
    sh_spec = pl.BlockSpec((MOD_ROWS, d), lambda bi, i: (0, shift_blk))
    sc_spec = pl.BlockSpec((MOD_ROWS, d), lambda bi, i: (0, scale_blk))
    h_shape = jax.ShapeDtypeStruct((b, s, d), BF16)
    if router is None:
        return pl.pallas_call(
            functools.partial(_norm_kernel, n_ctx_tiles=n_ctx_tiles),
            grid=grid,
            in_specs=[x_spec, g_spec, sh_spec, sc_spec],
            out_specs=x_spec,
            out_shape=h_shape,
            compiler_params=_cparams(("parallel", "parallel")),
            name="mod_norm",
        )(xs, gain, mods_l, mods_l)
    rw, rb = router
    return pl.pallas_call(
        functools.partial(_norm_router_kernel, n_ctx_tiles=n_ctx_tiles),
        grid=grid,
        in_specs=[x_spec, g_spec, sh_spec, sc_spec,
                  pl.BlockSpec((d, LANES), lambda bi, i: (0, 0)),
                  pl.BlockSpec((1, LANES), lambda bi, i: (0, 0))],
        out_specs=[x_spec, pl.BlockSpec((None, tr, LANES), lambda bi, i: (bi, i, 0))],
        out_shape=[h_shape, jax.ShapeDtypeStruct((b, s, LANES), F32)],
        compiler_params=_cparams(("parallel", "parallel")),
        name="mod_norm_router",
    )(xs, gain, mods_l, mods_l, rw, rb)


def _rope64(blk, c, sa, sb):
    return blk * c + pltpu.roll(blk, 96, 1) * sa + pltpu.roll(blk, 32, 1) * sb


def _proj_kernel(h_ref, w_ref, c64_ref, sa64_ref, sb64_ref, c128_ref, s128_ref, o_ref):
    j = pl.program_id(1)
    acc = jnp.dot(h_ref[...], w_ref[...], preferred_element_type=F32)
    n_blk = acc.shape[1] // LANES

    @pl.when((j == 0) | (j == 1))
    def _():
        scale = jnp.where(j == 0, DIFF_HALF ** -0.5, 1.0).astype(F32)
        c, sa, sb = c64_ref[...] * scale, sa64_ref[...] * scale, sb64_ref[...] * scale
        for t in range(n_blk):
            sl = slice(t * LANES, (t + 1) * LANES)
            o_ref[:, sl] = _rope64(acc[:, sl], c, sa, sb).astype(BF16)

    @pl.when((j == 3) | (j == 4))
    def _():
        scale = jnp.where(j == 4, RET_DK ** -0.5, 1.0).astype(F32)
        c, s = c128_ref[...] * scale, s128_ref[...] * scale
        for t in range(n_blk):
            sl = slice(t * LANES, (t + 1) * LANES)
            blk = acc[:, sl]
            o_ref[:, sl] = (blk * c + pltpu.roll(blk, 64, 1) * s).astype(BF16)

    @pl.when((j == 2) | (j >= 5))
    def _():
        o_ref[...] = acc.astype(BF16)


def _proj_call(h2d, w_main, tabs, tm, tiles_per_batch):
    r, d = h2d.shape
    tab_spec = pl.BlockSpec((tm, LANES), lambda i, j: (i % tiles_per_batch, 0))
    return pl.pallas_call(
        _proj_kernel,
        grid=(r // tm, MAIN_WIDTH // SEG),
        in_specs=[pl.BlockSpec((tm, d), lambda i, j: (i, 0)),
                  pl.BlockSpec((d, SEG), lambda i, j: (0, j)),
                  tab_spec, tab_spec, tab_spec, tab_spec, tab_spec],
        out_specs=pl.BlockSpec((tm, SEG), lambda i, j: (i, j)),
        out_shape=jax.ShapeDtypeStruct((r, MAIN_WIDTH), BF16),
        compiler_params=_cparams(("parallel", "arbitrary")),
        name="in_proj",
    )(h2d, w_main, tabs["c64"], tabs["sa64"], tabs["sb64"], tabs["c128"], tabs["s128"])


def _rms_gain(x, g_ref):
    ms = jnp.mean(x * x, axis=-1, keepdims=True)
    return x * lax.rsqrt(ms + EPS) * g_ref[...]


def _mla_q_kernel(qa_ref, g_ref, w_ref, c_ref, sa_ref, sb_ref, o_ref):
    qn = _rms_gain(qa_ref[...].astype(F32), g_ref).astype(BF16)
    q = jnp.dot(qn, w_ref[...], preferred_element_type=F32)
    scale = (MLA_NOPE + MLA_ROPE) ** -0.5
    c, sa, sb = c_ref[...] * scale, sa_ref[...] * scale, sb_ref[...] * scale
    for h in range(MLA_HEADS):
        lo = h * MLA_QK_PAD
        o_ref[:, lo:lo + LANES] = (q[:, lo:lo + LANES] * scale).astype(BF16)
        o_ref[:, lo + LANES:lo + 2 * LANES] = _rope64(q[:, lo + LANES:lo + 2 * LANES], c, sa, sb).astype(BF16)


def _mla_kv_kernel(h_ref, wt_ref, g_ref, w_ref, c_ref, sa_ref, sb_ref, k_ref, v_ref):
    t = jnp.dot(h_ref[...], wt_ref[...], preferred_element_type=F32)
    kvn = _rms_gain(t[:, :MLA_KV_RANK], g_ref).astype(BF16)
    kr = _rope64(t[:, MLA_KV_RANK:MLA_KV_RANK + LANES], c_ref[...], sa_ref[...], sb_ref[...]).astype(BF16)
    kv = jnp.dot(kvn, w_ref[...], preferred_element_type=F32)
    kn_width = MLA_HEADS * MLA_NOPE
    for h in range(MLA_HEADS):
        lo = h * MLA_QK_PAD
        k_ref[:, lo:lo + LANES] = kv[:, h * MLA_NOPE:(h + 1) * MLA_NOPE].astype(BF16)
        k_ref[:, lo + LANES:lo + 2 * LANES] = kr
    v_ref[...] = kv[:, kn_width:].astype(BF16)


def _mla_q_call(p2d, q_norm, wq, tabs, tm, tiles_per_batch):
    r = p2d.shape[0]
    tab_spec = pl.BlockSpec((tm, LANES), lambda i: (i % tiles_per_batch, 0))
    width = MLA_HEADS * MLA_QK_PAD
    return pl.pallas_call(
        _mla_q_kernel,
        grid=(r // tm,),
        in_specs=[pl.BlockSpec((tm, MLA_Q_RANK), lambda i: (i, (MAIN_WIDTH - MLA_Q_RANK) // MLA_Q_RANK)),
                  pl.BlockSpec((1, MLA_Q_RANK), lambda i: (0, 0)),
                  pl.BlockSpec((MLA_Q_RANK, width), lambda i: (0, 0)),
                  tab_spec, tab_spec, tab_spec],
        out_specs=pl.BlockSpec((tm, width), lambda i: (i, 0)),
        out_shape=jax.ShapeDtypeStruct((r, width), BF16),
        compiler_params=_cparams(("parallel",)),
        name="mla_q_proj",
    )(p2d, q_norm, wq, tabs["c64r"], tabs["sa64r"], tabs["sb64r"])


def _mla_kv_call(h2d, w_tail, kv_norm, wkv, tabs, tm, tiles_per_batch):
    r, d = h2d.shape
    tab_spec = pl.BlockSpec((tm, LANES), lambda i: (i % tiles_per_batch, 0))
    kwidth = MLA_HEADS * MLA_QK_PAD
    return pl.pallas_call(
        _mla_kv_kernel,
        grid=(r // tm,),
        in_specs=[pl.BlockSpec((tm, d), lambda i: (i, 0)),
                  pl.BlockSpec((d, TAIL_WIDTH), lambda i: (0, 0)),
                  pl.BlockSpec((1, MLA_KV_RANK), lambda i: (0, 0)),
                  pl.BlockSpec((MLA_KV_RANK, 2 * MLA_WIDTH), lambda i: (0, 0)),
                  tab_spec, tab_spec, tab_spec],
        out_specs=[pl.BlockSpec((tm, kwidth), lambda i: (i, 0)),
                   pl.BlockSpec((tm, MLA_WIDTH), lambda i: (i, 0))],
        out_shape=[jax.ShapeDtypeStruct((r, kwidth), BF16),
                   jax.ShapeDtypeStruct((r, MLA_WIDTH), BF16)],
        compiler_params=_cparams(("parallel",)),
        name="mla_kv_proj",
    )(h2d, w_tail, kv_norm, wkv, tabs["c64r"], tabs["sa64r"], tabs["sb64r"])


def _softmax_pv(s, v):
    m = jnp.max(s, axis=-1, keepdims=True)
    e = jnp.exp(s - m)
    l = jnp.sum(e, axis=-1, keepdims=True)
    return jnp.dot(e.astype(BF16), v, preferred_element_type=F32) / l


_NT = (((1,), (1,)), ((), ()))


def _diff_attn_kernel(lam_ref, sub_ref, q_ref, k_ref, v_ref, o_ref, *, n_ctx_tiles, ctx_len, lam_init):
    lv = lam_ref[...]
    lam = (jnp.exp(jnp.sum(lv[0:1] * lv[1:2], axis=-1, keepdims=True))
           - jnp.exp(jnp.sum(lv[2:3] * lv[3:4], axis=-1, keepdims=True)) + lam_init)
    sub = sub_ref[...] * (1.0 - lam_init)

    def run(m_keys):
        for h in range(DIFF_HEADS):
            sl = slice(h * HEAD_DIM, (h + 1) * HEAD_DIM)
            q = q_ref[:, sl]
            k = k_ref[0:m_keys, sl]
            v = v_ref[0:m_keys, sl]
            first = lax.broadcasted_iota(jnp.int32, q.shape, 1) < DIFF_HALF
            zero = jnp.zeros_like(q)
            s1 = lax.dot_general(jnp.where(first, q, zero), k, _NT, preferred_element_type=F32)
            s2 = lax.dot_general(jnp.where(first, zero, q), k, _NT, preferred_element_type=F32)
            o = _softmax_pv(s1, v) - lam * _softmax_pv(s2, v)
            ms = jnp.mean(o * o, axis=-1, keepdims=True)
            o_ref[:, sl] = (o * lax.rsqrt(ms + EPS) * sub).astype(BF16)

    is_ctx = pl.program_id(1) < n_ctx_tiles

    @pl.when(is_ctx)
    def _():
        run(ctx_len)

    @pl.when(jnp.logical_not(is_ctx))
    def _():
        run(k_ref.shape[0])


def _diff_attn_call(p3d, lamv, subln, lam_init, tq, n_ctx_tiles, ctx_len):
    b, s, _ = p3d.shape
    return pl.pallas_call(
        functools.partial(_diff_attn_kernel, n_ctx_tiles=n_ctx_tiles, ctx_len=ctx_len, lam_init=lam_init),
        grid=(b, s // tq),
        in_specs=[pl.BlockSpec((4, DIFF_HALF), lambda bi, i: (0, 0)),
                  pl.BlockSpec((1, HEAD_DIM), lambda bi, i: (0, 0)),
                  pl.BlockSpec((None, tq, DIFF_WIDTH), lambda bi, i: (bi, i, 0)),
                  pl.BlockSpec((None, s, DIFF_WIDTH), lambda bi, i: (bi, 0, 1)),
                  pl.BlockSpec((None, s, DIFF_WIDTH), lambda bi, i: (bi, 0, 2))],
        out_specs=pl.BlockSpec((None, tq, DIFF_WIDTH), lambda bi, i: (bi, i, 0)),
        out_shape=jax.ShapeDtypeStruct((b, s, DIFF_WIDTH), BF16),
        compiler_params=_cparams(("parallel", "arbitrary")),
        name="diff_attn",
    )(lamv, subln, p3d, p3d, p3d)


MLA_HEAD_GROUP = 4


def _mla_attn_kernel(q_ref, k_ref, v_ref, o_ref, *, n_ctx_tiles, ctx_len):
    def run(m_keys):
        for h in range(MLA_HEAD_GROUP):
            qk = slice(h * MLA_QK_PAD, (h + 1) * MLA_QK_PAD)
            vv = slice(h * MLA_V, (h + 1) * MLA_V)
            s = lax.dot_general(q_ref[:, qk], k_ref[0:m_keys, qk], _NT, preferred_element_type=F32)
            o_ref[:, vv] = _softmax_pv(s, v_ref[0:m_keys, vv]).astype(BF16)

    is_ctx = pl.program_id(2) < n_ctx_tiles

    @pl.when(is_ctx)
    def _():
        run(ctx_len)

    @pl.when(jnp.logical_not(is_ctx))
    def _():
        run(k_ref.shape[0])


def _mla_attn_call(qc, kc, vc, tq, n_ctx_tiles, ctx_len):
    b, s, _ = qc.shape
    qw = MLA_HEAD_GROUP * MLA_QK_PAD
    vw = MLA_HEAD_GROUP * MLA_V
    return pl.pallas_call(
        functools.partial(_mla_attn_kernel, n_ctx_tiles=n_ctx_tiles, ctx_len=ctx_len),
        grid=(b, MLA_HEADS // MLA_HEAD_GROUP, s // tq),
        in_specs=[pl.BlockSpec((None, tq, qw), lambda bi, g, i: (bi, i, g)),
                  pl.BlockSpec((None, s, qw), lambda bi, g, i: (bi, 0, g)),
                  pl.BlockSpec((None, s, vw), lambda bi, g, i: (bi, 0, g))],
        out_specs=pl.BlockSpec((None, tq, vw), lambda bi, g, i: (bi, i, g)),
        out_shape=jax.ShapeDtypeStruct((b, s, MLA_WIDTH), BF16),
        compiler_params=_cparams(("parallel", "parallel", "arbitrary")),
        name="mla_attn",
    )(qc, kc, vc)


RET_HEAD_GROUP = 2
_TN = (((0,), (0,)), ((), ()))


def _ret_kernel(lg_ref, q_ref, k_ref, v_ref, g_ref, gn_ref, o_ref, dec_ref, st_ref, acc_ref, *, n_ctx_chunks):
    c = RET_CHUNK
    n_chunks = q_ref.shape[0] // c
    hp = pl.program_id(1)
    row = lax.broadcasted_iota(jnp.int32, (c, c), 0).astype(F32)
    col = lax.broadcasted_iota(jnp.int32, (c, c), 1).astype(F32)
    rel = row - col

    for hh in range(RET_HEAD_GROUP):
        lgf = lg_ref[0, hp * RET_HEAD_GROUP + hh]
        lgb = lg_ref[1, hp * RET_HEAD_GROUP + hh]
        dec_ref[hh, 0] = jnp.where(rel >= 0, jnp.exp(jnp.maximum(rel, 0.0) * lgf), 0.0)
        dec_ref[hh, 1] = jnp.exp((row + 1.0) * lgf)
        dec_ref[hh, 2] = jnp.exp((c - 1.0 - row) * lgf)
        dec_ref[hh, 3] = jnp.where(rel <= 0, jnp.exp(jnp.maximum(-rel, 0.0) * lgb), 0.0)
        dec_ref[hh, 4] = jnp.exp((c - row) * lgb)
        dec_ref[hh, 5] = jnp.exp(row * lgb)
        dec_ref[hh, 6] = jnp.exp(jnp.full((c, c), float(c), F32) * lgf)
        dec_ref[hh, 7] = jnp.exp(jnp.full((c, c), float(c), F32) * lgb)

    st_ref[...] = jnp.zeros_like(st_ref)
    acc_ref[...] = jnp.zeros_like(acc_ref)

    def chunk_update(hh, direction, chunk):
        base = 3 * direction
        sl = slice(hh * RET_DK, (hh + 1) * RET_DK)
        rows = pl.ds(pl.multiple_of(chunk * c, c), c)
        qc = q_ref[rows, sl]
        kc = k_ref[rows, sl]
        vc = v_ref[rows, sl]
        state = st_ref[hh, direction]
        att = lax.dot_general(qc, kc, _NT, preferred_element_type=F32) * dec_ref[hh, base]
        o = (jnp.dot(att.astype(BF16), vc, preferred_element_type=F32)
             + jnp.dot(qc, state.astype(BF16), preferred_element_type=F32) * dec_ref[hh, base + 1])
        kd = (kc.astype(F32) * dec_ref[hh, base + 2]).astype(BF16)
        st_ref[hh, direction] = (state * dec_ref[hh, 6 + direction]
                                 + lax.dot_general(kd, vc, _TN, preferred_element_type=F32))
        acc_ref[rows, sl] += o

    def step(t, carry):
        bwd = jnp.where(t < n_ctx_chunks, n_ctx_chunks - 1 - t, n_chunks - 1 - t + n_ctx_chunks)
        for hh in range(RET_HEAD_GROUP):
            chunk_update(hh, 0, t)
            chunk_update(hh, 1, bwd)
        return carry

    lax.fori_loop(0, n_chunks, step, 0)

    for hh in range(RET_HEAD_GROUP):
        sl = slice(hh * RET_DK, (hh + 1) * RET_DK)
        o = acc_ref[:, sl]
        mu = jnp.mean(o, axis=-1, keepdims=True)
        oc = o - mu
        y = oc * lax.rsqrt(jnp.mean(oc * oc, axis=-1, keepdims=True) + 1e-5) * gn_ref[:, sl]
        g = g_ref[:, sl].astype(F32)
        o_ref[:, sl] = (y * (g * jax.nn.sigmoid(g))).astype(BF16)


def _ret_call(p3d, log_gamma, gn_gain, n_ctx_chunks):
    b, s, _ = p3d.shape
    w = RET_HEAD_GROUP * RET_DK
    base = 3 * SEG // w

    def col(seg):
        return lambda bi, hp, lg: (bi, 0, base + seg * (SEG // w) + hp)

    grid_spec = pltpu.PrefetchScalarGridSpec(
        num_scalar_prefetch=1,
        grid=(b, RET_HEADS // RET_HEAD_GROUP),
        in_specs=[pl.BlockSpec((None, s, w), col(0)),
                  pl.BlockSpec((None, s, w), col(1)),
                  pl.BlockSpec((None, s, w), col(2)),
                  pl.BlockSpec((None, s, w), col(3)),
                  pl.BlockSpec((1, w), lambda bi, hp, lg: (0, hp))],
        out_specs=pl.BlockSpec((None, s, w), lambda bi, hp, lg: (bi, 0, hp)),
        scratch_shapes=[pltpu.VMEM((RET_HEAD_GROUP, 8, RET_CHUNK, RET_CHUNK), F32),
                        pltpu.VMEM((RET_HEAD_GROUP, 2, RET_DK, RET_DK), F32),
                        pltpu.VMEM((s, w), F32)],
    )
    return pl.pallas_call(
        functools.partial(_ret_kernel, n_ctx_chunks=n_ctx_chunks),
        grid_spec=grid_spec,
        out_shape=jax.ShapeDtypeStruct((b, s, RET_WIDTH), BF16),
        compiler_params=_cparams(("parallel", "arbitrary")),
        name="retention",
    )(log_gamma, p3d, p3d, p3d, p3d, gn_gain)


def _row_gate(gate_ref, tm, ctx_len, tiles_per_batch):
    i = pl.program_id(0)
    b = i // tiles_per_batch
    n_ctx = jnp.where(i % tiles_per_batch == 0, ctx_len, 0)
    is_ctx = lax.broadcasted_iota(jnp.int32, (tm, 1), 0) < n_ctx
    return jnp.where(is_ctx, gate_ref[pl.ds(CTX_MOD_ROW, 1), :], gate_ref[pl.ds(b, 1), :])


def _out_proj_kernel(a_ref, b_ref, c_ref, w_ref, x_ref, gate_ref, o_ref, *, ctx_len, tiles_per_batch):
    acc = jnp.dot(a_ref[...], w_ref[0:DIFF_WIDTH, :], preferred_element_type=F32)
    acc += jnp.dot(b_ref[...], w_ref[DIFF_WIDTH:DIFF_WIDTH + RET_WIDTH, :], preferred_element_type=F32)
    acc += jnp.dot(c_ref[...], w_ref[DIFF_WIDTH + RET_WIDTH:, :], preferred_element_type=F32)
    o_ref[...] = x_ref[...] + _row_gate(gate_ref, acc.shape[0], ctx_len, tiles_per_batch) * acc


def _out_proj_call(a2d, b2d, c2d, w_out, x2d, mods_l, gate_blk, tm, tn, ctx_len, tiles_per_batch):
    r, d = x2d.shape
    n_j = d // tn
    return pl.pallas_call(
        functools.partial(_out_proj_kernel, ctx_len=ctx_len, tiles_per_batch=tiles_per_batch),
        grid=(r // tm, n_j),
        in_specs=[pl.BlockSpec((tm, DIFF_WIDTH), lambda i, j: (i, 0)),
                  pl.BlockSpec((tm, RET_WIDTH), lambda i, j: (i, 0)),
                  pl.BlockSpec((tm, MLA_WIDTH), lambda i, j: (i, 0)),
                  pl.BlockSpec((d, tn), lambda i, j: (0, j)),
                  pl.BlockSpec((tm, tn), lambda i, j: (i, j)),
                  pl.BlockSpec((MOD_ROWS, tn), lambda i, j: (0, gate_blk * n_j + j))],
        out_specs=pl.BlockSpec((tm, tn), lambda i, j: (i, j)),
        out_shape=jax.ShapeDtypeStruct((r, d), F32),
        compiler_params=_cparams(("parallel", "arbitrary")),
        name="out_proj",
    )(a2d, b2d, c2d, w_out, x2d, mods_l)


def _split_bf16(x):
    hi = x.astype(BF16)
    return hi, (x - hi.astype(F32)).astype(BF16)


def _moe_up_kernel(h_ref, w_ref, b_ref, gate_ref, o_ref):
    j = pl.program_id(1)
    gu = jnp.dot(h_ref[...], w_ref[...], preferred_element_type=F32) + b_ref[...]
    n_e = gu.shape[1] // (2 * EXPERT_FF)
    e_row = lax.broadcasted_iota(jnp.int32, (LANES, n_e * EXPERT_FF), 0)
    e_col = lax.broadcasted_iota(jnp.int32, (LANES, n_e * EXPERT_FF), 1) // EXPERT_FF
    expand = (e_row == j * n_e + e_col).astype(BF16)
    g_hi, g_lo = _split_bf16(gate_ref[...])
    ge = (jnp.dot(g_hi, expand, preferred_element_type=F32)
          + jnp.dot(g_lo, expand, preferred_element_type=F32))
    for e in range(n_e):
        lo = e * 2 * EXPERT_FF
        glu = jnp.minimum(gu[:, lo:lo + EXPERT_FF], SWIGLU_LIMIT)
        lin = jnp.clip(gu[:, lo + EXPERT_FF:lo + 2 * EXPERT_FF], -SWIGLU_LIMIT, SWIGLU_LIMIT)
        act = glu * jax.nn.sigmoid(SWIGLU_ALPHA * glu) * (lin + 1.0)
        sl = slice(e * EXPERT_FF, (e + 1) * EXPERT_FF)
        o_ref[:, sl] = (act * ge[:, sl]).astype(BF16)


def _moe_up_call(h2d, w_gu, b_gu, gates2d, tm):
    r, d = h2d.shape
    tn = SEG
    n = w_gu.shape[1]
    return pl.pallas_call(
        _moe_up_kernel,
        grid=(r // tm, n // tn),
        in_specs=[pl.BlockSpec((tm, d), lambda i, j: (i, 0)),
                  pl.BlockSpec((d, tn), lambda i, j: (0, j)),
                  pl.BlockSpec((1, tn), lambda i, j: (0, j)),
                  pl.BlockSpec((tm, LANES), lambda i, j: (i, 0))],
        out_specs=pl.BlockSpec((tm, tn // 2), lambda i, j: (i, j)),
        out_shape=jax.ShapeDtypeStruct((r, n // 2), BF16),
        compiler_params=_cparams(("parallel", "arbitrary")),
        name="moe_up",
    )(h2d, w_gu, b_gu, gates2d)


def _moe_down_kernel(a_ref, w_ref, gates_ref, bd_ref, x_ref, gate_ref, o_ref, *, ctx_len, tiles_per_batch):
    acc = jnp.dot(a_ref[...], w_ref[...], preferred_element_type=F32)
    g_hi, g_lo = _split_bf16(gates_ref[...])
    acc += jnp.dot(g_hi, bd_ref[...], preferred_element_type=F32)
    acc += jnp.dot(g_lo, bd_ref[...], preferred_element_type=F32)
    o_ref[...] = x_ref[...] + _row_gate(gate_ref, acc.shape[0], ctx_len, tiles_per_batch) * acc


def _moe_down_call(a2d, w_down, gates2d, b_down, x2d, mods_l, gate_blk, tm, tn, ctx_len, tiles_per_batch):
    r, d = x2d.shape
    k = a2d.shape[1]
    n_j = d // tn
    return pl.pallas_call(
        functools.partial(_moe_down_kernel, ctx_len=ctx_len, tiles_per_batch=tiles_per_batch),
        grid=(r // tm, n_j),
        in_specs=[pl.BlockSpec((tm, k), lambda i, j: (i, 0)),
                  pl.BlockSpec((k, tn), lambda i, j: (0, j)),
                  pl.BlockSpec((tm, LANES), lambda i, j: (i, 0)),
                  pl.BlockSpec((LANES, tn), lambda i, j: (0, j)),
                  pl.BlockSpec((tm, tn), lambda i, j: (i, j)),
                  pl.BlockSpec((MOD_ROWS, tn), lambda i, j: (0, gate_blk * n_j + j))],
        out_specs=pl.BlockSpec((tm, tn), lambda i, j: (i, j)),
        out_shape=jax.ShapeDtypeStruct((r, d), F32),
        compiler_params=_cparams(("parallel", "arbitrary")),
        name="moe_down",
    )(a2d, w_down, gates2d, b_down, x2d, mods_l)


def _final_norm_kernel(x_ref, g_ref, o_ref):
    o_ref[...] = _rms_gain(x_ref[...], g_ref)


def _final_norm_call(xs, gain, ctx_len, tr):
    b, s, d = xs.shape
    n = s - ctx_len
    off = ctx_len // tr
    return pl.pallas_call(
        _final_norm_kernel,
        grid=(b, n // tr),
        in_specs=[pl.BlockSpec((None, tr, d), lambda bi, i: (bi, i + off, 0)),
                  pl.BlockSpec((1, d), lambda bi, i: (0, 0))],
        out_specs=pl.BlockSpec((None, tr, d), lambda bi, i: (bi, i, 0)),
        out_shape=jax.ShapeDtypeStruct((b, n, d), F32),
        compiler_params=_cparams(("parallel", "parallel")),
        name="final_norm",
    )(xs, gain)


def _rope_tables(n, ctx_len):
    rows = n // GRID_W
    row = jnp.repeat(jnp.arange(rows, dtype=F32), GRID_W)
    col = jnp.tile(jnp.arange(GRID_W, dtype=F32), rows)

    def cos_sin(rot_dim):
        pairs = rot_dim // 4
        inv = ROPE_BASE ** (-jnp.arange(pairs, dtype=F32) / pairs)
        ang = jnp.concatenate([row[:, None] * inv, col[:, None] * inv], axis=-1)
        cos = jnp.concatenate([jnp.ones((ctx_len, rot_dim // 2), F32), jnp.cos(ang)], axis=0)
        sin = jnp.concatenate([jnp.zeros((ctx_len, rot_dim // 2), F32), jnp.sin(ang)], axis=0)
        return cos, sin

    cos32, sin32 = cos_sin(DIFF_HALF)
    cos64, sin64 = cos_sin(RET_DK)
    z32 = jnp.zeros_like(sin32)
    one64 = jnp.ones((cos32.shape[0], 64), F32)
    z64 = jnp.zeros_like(one64)
    return {
        "c64": jnp.concatenate([cos32] * 4, axis=-1),
        "sa64": jnp.concatenate([-sin32, z32, -sin32, z32], axis=-1),
        "sb64": jnp.concatenate([z32, sin32, z32, sin32], axis=-1),
        "c64r": jnp.concatenate([cos32, cos32, one64], axis=-1),
        "sa64r": jnp.concatenate([-sin32, z32, z64], axis=-1),
        "sb64r": jnp.concatenate([z32, sin32, z64], axis=-1),
        "c128": jnp.concatenate([cos64, cos64], axis=-1),
        "s128": jnp.concatenate([-sin64, sin64], axis=-1),
    }


def _layer_weights(l, w_in, mla_wq_b, mla_wkv_b, w_out, router_w, router_b,
                   exp_w_gu, exp_b_gu, exp_w_down, exp_b_down):
    d = D_MODEL
    w_main = w_in[l, :, :MAIN_WIDTH].astype(BF16)
    w_tail = jnp.pad(w_in[l, :, MAIN_WIDTH:], ((0, 0), (0, TAIL_WIDTH - (w_in.shape[2] - MAIN_WIDTH)))).astype(BF16)
    wq = mla_wq_b[l].reshape(MLA_Q_RANK, MLA_HEADS, MLA_NOPE + MLA_ROPE)
    wq = jnp.pad(wq, ((0, 0), (0, 0), (0, MLA_QK_PAD - MLA_NOPE - MLA_ROPE)))
    wq = wq.reshape(MLA_Q_RANK, MLA_HEADS * MLA_QK_PAD).astype(BF16)
    wkv = mla_wkv_b[l].reshape(MLA_KV_RANK, MLA_HEADS, MLA_NOPE + MLA_V)
    wkv = jnp.concatenate([wkv[:, :, :MLA_NOPE].reshape(MLA_KV_RANK, -1),
                           wkv[:, :, MLA_NOPE:].reshape(MLA_KV_RANK, -1)], axis=-1).astype(BF16)
    rw = jnp.pad(router_w[l], ((0, 0), (0, LANES - N_EXPERTS))).astype(BF16)
    rb = jnp.concatenate([router_b[l], jnp.full((LANES - N_EXPERTS,), NEG_BIG, F32)]).reshape(1, LANES)
    w_gu = jnp.transpose(exp_w_gu[l], (1, 0, 2)).reshape(d, N_EXPERTS * 2 * EXPERT_FF).astype(BF16)
    b_gu = exp_b_gu[l].reshape(1, N_EXPERTS * 2 * EXPERT_FF)
    w_down = exp_w_down[l].reshape(N_EXPERTS * EXPERT_FF, d).astype(BF16)
    b_down = jnp.pad(exp_b_down[l], ((0, LANES - N_EXPERTS), (0, 0))).astype(BF16)
    return dict(w_main=w_main, w_tail=w_tail, wq=wq, wkv=wkv, w_out=w_out[l].astype(BF16),
                rw=rw, rb=rb, w_gu=w_gu, b_gu=b_gu, w_down=w_down, b_down=b_down)


def kernel(x, c, ctx, c_ctx, ada_w, ada_b, norm1, norm2, w_in, diff_lq1, diff_lk1, diff_lq2, diff_lk2, diff_subln, ret_decay, ret_norm, mla_q_norm, mla_kv_norm, mla_wq_b, mla_wkv_b, w_out, router_w, router_b, exp_w_gu, exp_b_gu, exp_w_down, exp_b_down, final_norm):
    batch, n, d = x.shape
    ctx_len = ctx.shape[1]
    depth = ada_w.shape[0]
    s = ctx_len + n
    r = batch * s
    assert d == D_MODEL and batch <= CTX_MOD_ROW and ctx_len % RET_CHUNK == 0 and n % RET_CHUNK == 0

    tq = _pick_tile(ctx_len, (256, 128))
    n_ctx_tiles = ctx_len // tq
    tm = _pick_tile(s, (768, 576, 384, 256, 128), at_least=ctx_len)
    tm_mla = _pick_tile(s, (384, 256, 128))
    tiles_per_batch = s // tm
    tn = 1024

    xs = jnp.concatenate([ctx, x], axis=1)
    mods = _ada_tables(_stack_conditioning(c, c_ctx), ada_w, ada_b)
    tabs = _rope_tables(n, ctx_len)
    log_gamma = jnp.log1p(-jnp.exp2(-ret_decay.astype(F32)))

    for l in range(depth):
        lam_init = 0.8 - 0.6 * math.exp(-0.3 * l)
        wl = _layer_weights(l, w_in, mla_wq_b, mla_wkv_b, w_out, router_w, router_b,
                            exp_w_gu, exp_b_gu, exp_w_down, exp_b_down)
        mods_l = mods[l]
        h = _norm_call(xs, norm1[l].reshape(1, d), mods_l, 0, 1, n_ctx_tiles, tq)
        h2d = h.reshape(r, d)
        p2d = _proj_call(h2d, wl["w_main"], tabs, tm, tiles_per_batch)
        p3d = p2d.reshape(batch, s, MAIN_WIDTH)
        qc = _mla_q_call(p2d, mla_q_norm[l].reshape(1, -1), wl["wq"], tabs, tm_mla, s // tm_mla)
        kc, vc = _mla_kv_call(h2d, wl["w_tail"], mla_kv_norm[l].reshape(1, -1), wl["wkv"], tabs, tm_mla,
                              s // tm_mla)
        lamv = jnp.stack([diff_lq1[l], diff_lk1[l], diff_lq2[l], diff_lk2[l]]).astype(F32)
        a = _diff_attn_call(p3d, lamv, diff_subln[l].reshape(1, -1), lam_init, tq, n_ctx_tiles, ctx_len)
        bb = _ret_call(p3d, log_gamma[l], ret_norm[l].reshape(1, -1), ctx_len // RET_CHUNK)
        cc = _mla_attn_call(qc.reshape(batch, s, -1), kc.reshape(batch, s, -1), vc.reshape(batch, s, -1),
                            tq, n_ctx_tiles, ctx_len)
        x2d = _out_proj_call(a.reshape(r, -1), bb.reshape(r, -1), cc.reshape(r, -1), wl["w_out"],
                             xs.reshape(r, d), mods_l, 2, tm, tn, ctx_len, tiles_per_batch)
        xs = x2d.reshape(batch, s, d)
        h2, gates = _norm_call(xs, norm2[l].reshape(1, d), mods_l, 3, 4, n_ctx_tiles, tq,
                               router=(wl["rw"], wl["rb"]))
        gates2d = gates.reshape(r, LANES)
        act = _moe_up_call(h2.reshape(r, d), wl["w_gu"], wl["b_gu"], gates2d, tm)
        x2d = _moe_down_call(act, wl["w_down"], gates2d, wl["b_down"], x2d, mods_l, 5, tm, tn,
                             ctx_len, tiles_per_batch)
        xs = x2d.reshape(batch, s, d)

    return _final_norm_call(xs, final_norm.reshape(1, d), ctx_len, tq)
```

```python
import functools
import math

import jax
import jax.numpy as jnp
from jax import lax
from jax.experimental import pallas as pl
from jax.experimental.pallas import tpu as pltpu

F32 = jnp.float32
BF16 = jnp.bfloat16

D_MODEL = 4096
GRID_W = 64
RET_CHUNK = 128
ROPE_BASE = 10000.0
EPS = 1e-6
HEAD_DIM = 128
DIFF_HEADS = 8
DIFF_HALF = 64
DIFF_WIDTH = 1024
RET_HEADS = 8
RET_DK = 128
RET_WIDTH = 1024
MLA_HEADS = 16
MLA_Q_RANK = 1024
MLA_KV_RANK = 512
MLA_NOPE = 128
MLA_ROPE = 64
MLA_V = 128
MLA_WIDTH = 2048
MLA_QK_PAD = 256
N_EXPERTS = 32
TOP_K = 4
EXPERT_FF = 128
SWIGLU_LIMIT = 7.0
SWIGLU_ALPHA = 1.702
MAIN_WIDTH = 8192
TAIL_WIDTH = 640
SEG = 1024

LANES = 128
VMEM_LIMIT_BYTES = 56 * 1024 * 1024
NEG_BIG = -1e30
CTX_MOD_ROW = 8
MOD_ROWS = 16


def _cparams(sem):
    return pltpu.CompilerParams(dimension_semantics=sem, vmem_limit_bytes=VMEM_LIMIT_BYTES)


def _pick_tile(total, candidates, at_least=1):
    for c in candidates:
        if total % c == 0 and c >= at_least:
            return c
    raise ValueError(f"no tile for {total}")


def _ada_kernel(c_ref, w_ref, b_ref, o_ref):
    c = c_ref[...]
    cs = (c * jax.nn.sigmoid(c)).astype(BF16)
    o_ref[...] = jnp.dot(cs, w_ref[...].astype(BF16), preferred_element_type=F32) + b_ref[...]


def _stack_conditioning(c, c_ctx):
    batch, d = c.shape
    return jnp.concatenate([c, jnp.zeros((CTX_MOD_ROW - batch, d), F32), c_ctx[None, :],
                            jnp.zeros((MOD_ROWS - CTX_MOD_ROW - 1, d), F32)], axis=0)


def _ada_tables(c_all, ada_w, ada_b):
    depth, d, n6 = ada_w.shape
    tn = 512
    return pl.pallas_call(
        _ada_kernel,
        grid=(depth, n6 // tn),
        in_specs=[
            pl.BlockSpec((MOD_ROWS, d), lambda l, j: (0, 0)),
            pl.BlockSpec((None, d, tn), lambda l, j: (l, 0, j)),
            pl.BlockSpec((None, 1, tn), lambda l, j: (l, 0, j)),
        ],
        out_specs=pl.BlockSpec((None, MOD_ROWS, tn), lambda l, j: (l, 0, j)),
        out_shape=jax.ShapeDtypeStruct((depth, MOD_ROWS, n6), F32),
        compiler_params=_cparams(("arbitrary", "arbitrary")),
        name="ada_tables",
    )(c_all, ada_w, ada_b.reshape(depth, 1, n6))


def _mod_norm(x, g_ref, sh_ref, sc_ref, row):
    ms = jnp.mean(x * x, axis=-1, keepdims=True)
    y = x * lax.rsqrt(ms + EPS) * g_ref[...]
    return y * (1.0 + sc_ref[pl.ds(row, 1), :]) + sh_ref[pl.ds(row, 1), :]


def _norm_kernel(x_ref, g_ref, sh_ref, sc_ref, o_ref, *, n_ctx_tiles):
    row = jnp.where(pl.program_id(1) < n_ctx_tiles, CTX_MOD_ROW, pl.program_id(0))
    o_ref[...] = _mod_norm(x_ref[...], g_ref, sh_ref, sc_ref, row).astype(BF16)


def _norm_router_kernel(x_ref, g_ref, sh_ref, sc_ref, rw_ref, rb_ref, o_ref, gate_ref, *, n_ctx_tiles):
    row = jnp.where(pl.program_id(1) < n_ctx_tiles, CTX_MOD_ROW, pl.program_id(0))
    h = _mod_norm(x_ref[...], g_ref, sh_ref, sc_ref, row).astype(BF16)
    o_ref[...] = h
    logits = jnp.dot(h, rw_ref[...], preferred_element_type=F32) + rb_ref[...]
    lane = lax.broadcasted_iota(jnp.int32, logits.shape, 1)
    work = logits
    num = jnp.zeros_like(logits)
    den = jnp.zeros((logits.shape[0], 1), F32)
    m0 = None
    for k in range(TOP_K):
        m = jnp.max(work, axis=-1, keepdims=True)
        idx = jnp.min(jnp.where(work == m, lane, LANES), axis=-1, keepdims=True)
        pick = lane == idx
        if k == 0:
            m0 = m
        e = jnp.exp(m - m0)
        num = num + jnp.where(pick, e, 0.0)
        den = den + e
        work = jnp.where(pick, -jnp.inf, work)
    gate_ref[...] = num / den


def _norm_call(xs, gain, mods_l, shift_blk, scale_blk, n_ctx_tiles, tr, router=None):
    b, s, d = xs.shape
    grid = (b, s // tr)
    x_spec = pl.BlockSpec((None, tr, d), lambda bi, i: (bi, i, 0))
    g_spec = pl.BlockSpec((1, d), lambda bi, i: (0, 0))
    sh_spec = pl.BlockSpec((MOD_ROWS, d), lambda bi, i: (0, shift_blk))
    sc_spec = pl.BlockSpec((MOD_ROWS, d), lambda bi, i: (0, scale_blk))
    h_shape = jax.ShapeDtypeStruct((b, s, d), BF16)
    if router is None:
        return pl.pallas_call(
            functools.partial(_norm_kernel, n_ctx_tiles=n_ctx_tiles),
            grid=grid,
            in_specs=[x_spec, g_spec, sh_spec, sc_spec],
            out_specs=x_spec,
            out_shape=h_shape,
            compiler_params=_cparams(("parallel", "parallel")),
            name="mod_norm",
        )(xs, gain, mods_l, mods_l)
    rw, rb, layer = router
    return pl.pallas_call(
        functools.partial(_norm_router_kernel, n_ctx_tiles=n_ctx_tiles),
        grid=grid,
        in_specs=[x_spec, g_spec, sh_spec, sc_spec,
                  pl.BlockSpec((None, d, LANES), lambda bi, i: (layer, 0, 0)),
                  pl.BlockSpec((None, 1, LANES), lambda bi, i: (layer, 0, 0))],
        out_specs=[x_spec, pl.BlockSpec((None, tr, LANES), lambda bi, i: (bi, i, 0))],
        out_shape=[h_shape, jax.ShapeDtypeStruct((b, s, LANES), F32)],
        compiler_params=_cparams(("parallel", "parallel")),
        name="mod_norm_router",
    )(xs, gain, mods_l, mods_l, rw, rb)


def _rope64(blk, c, sa, sb):
    return blk * c + pltpu.roll(blk, 96, 1) * sa + pltpu.roll(blk, 32, 1) * sb


def _proj_kernel(h_ref, w_ref, c64_ref, sa64_ref, sb64_ref, c128_ref, s128_ref, o_ref):
    j = pl.program_id(1)
    acc = jnp.dot(h_ref[...], w_ref[...], preferred_element_type=F32)
    n_blk = acc.shape[1] // LANES

    @pl.when((j == 0) | (j == 1))
    def _():
        scale = jnp.where(j == 0, DIFF_HALF ** -0.5, 1.0).astype(F32)
        c, sa, sb = c64_ref[...] * scale, sa64_ref[...] * scale, sb64_ref[...] * scale
        for t in range(n_blk):
            sl = slice(t * LANES, (t + 1) * LANES)
            o_ref[:, sl] = _rope64(acc[:, sl], c, sa, sb).astype(BF16)

    @pl.when((j == 3) | (j == 4))
    def _():
        scale = jnp.where(j == 4, RET_DK ** -0.5, 1.0).astype(F32)
        c, s = c128_ref[...] * scale, s128_ref[...] * scale
        for t in range(n_blk):
            sl = slice(t * LANES, (t + 1) * LANES)
            blk = acc[:, sl]
            o_ref[:, sl] = (blk * c + pltpu.roll(blk, 64, 1) * s).astype(BF16)

    @pl.when((j == 2) | (j >= 5))
    def _():
        o_ref[...] = acc.astype(BF16)


def _proj_call(h2d, w_in_bf, l, tabs, tm, tiles_per_batch):
    r, d = h2d.shape
    tab_spec = pl.BlockSpec((tm, LANES), lambda i, j: (i % tiles_per_batch, 0))
    return pl.pallas_call(
        _proj_kernel,
        grid=(r // tm, MAIN_WIDTH // SEG),
        in_specs=[pl.BlockSpec((tm, d), lambda i, j: (i, 0)),
                  pl.BlockSpec((None, d, SEG), lambda i, j: (l, 0, j)),
                  tab_spec, tab_spec, tab_spec, tab_spec, tab_spec],
        out_specs=pl.BlockSpec((tm, SEG), lambda i, j: (i, j)),
        out_shape=jax.ShapeDtypeStruct((r, MAIN_WIDTH), BF16),
        compiler_params=_cparams(("parallel", "arbitrary")),
        name="in_proj",
    )(h2d, w_in_bf, tabs["c64"], tabs["sa64"], tabs["sb64"], tabs["c128"], tabs["s128"])


def _rms_gain(x, g_ref):
    ms = jnp.mean(x * x, axis=-1, keepdims=True)
    return x * lax.rsqrt(ms + EPS) * g_ref[...]


def _mla_q_kernel(qa_ref, g_ref, w_ref, c_ref, sa_ref, sb_ref, o_ref):
    qn = _rms_gain(qa_ref[...].astype(F32), g_ref).astype(BF16)
    q = jnp.dot(qn, w_ref[...], preferred_element_type=F32)
    scale = (MLA_NOPE + MLA_ROPE) ** -0.5
    c, sa, sb = c_ref[...] * scale, sa_ref[...] * scale, sb_ref[...] * scale
    for h in range(MLA_HEADS):
        lo = h * MLA_QK_PAD
        o_ref[:, lo:lo + LANES] = (q[:, lo:lo + LANES] * scale).astype(BF16)
        o_ref[:, lo + LANES:lo + 2 * LANES] = _rope64(q[:, lo + LANES:lo + 2 * LANES], c, sa, sb).astype(BF16)


def _mla_kv_kernel(h_ref, wt_ref, g_ref, w_ref, c_ref, sa_ref, sb_ref, k_ref, v_ref):
    t = jnp.dot(h_ref[...], wt_ref[...], preferred_element_type=F32)
    kvn = _rms_gain(t[:, :MLA_KV_RANK], g_ref).astype(BF16)
    kr = _rope64(t[:, MLA_KV_RANK:MLA_KV_RANK + LANES], c_ref[...], sa_ref[...], sb_ref[...]).astype(BF16)
    kv = jnp.dot(kvn, w_ref[...], preferred_element_type=F32)
    kn_width = MLA_HEADS * MLA_NOPE
    for h in range(MLA_HEADS):
        lo = h * MLA_QK_PAD
        k_ref[:, lo:lo + LANES] = kv[:, h * MLA_NOPE:(h + 1) * MLA_NOPE].astype(BF16)
        k_ref[:, lo + LANES:lo + 2 * LANES] = kr
    v_ref[...] = kv[:, kn_width:].astype(BF16)


def _mla_q_call(p2d, q_norm, wq, l, tabs, tm, tiles_per_batch):
    r = p2d.shape[0]
    tab_spec = pl.BlockSpec((tm, LANES), lambda i: (i % tiles_per_batch, 0))
    width = MLA_HEADS * MLA_QK_PAD
    return pl.pallas_call(
        _mla_q_kernel,
        grid=(r // tm,),
        in_specs=[pl.BlockSpec((tm, MLA_Q_RANK), lambda i: (i, (MAIN_WIDTH - MLA_Q_RANK) // MLA_Q_RANK)),
                  pl.BlockSpec((None, 1, MLA_Q_RANK), lambda i: (l, 0, 0)),
                  pl.BlockSpec((None, MLA_Q_RANK, width), lambda i: (l, 0, 0)),
                  tab_spec, tab_spec, tab_spec],
        out_specs=pl.BlockSpec((tm, width), lambda i: (i, 0)),
        out_shape=jax.ShapeDtypeStruct((r, width), BF16),
        compiler_params=_cparams(("parallel",)),
        name="mla_q_proj",
    )(p2d, q_norm, wq, tabs["c64r"], tabs["sa64r"], tabs["sb64r"])


def _mla_kv_call(h2d, w_tail, kv_norm, wkv, l, tabs, tm, tiles_per_batch):
    r, d = h2d.shape
    tab_spec = pl.BlockSpec((tm, LANES), lambda i: (i % tiles_per_batch, 0))
    kwidth = MLA_HEADS * MLA_QK_PAD
    return pl.pallas_call(
        _mla_kv_kernel,
        grid=(r // tm,),
        in_specs=[pl.BlockSpec((tm, d), lambda i: (i, 0)),
                  pl.BlockSpec((None, d, TAIL_WIDTH), lambda i: (l, 0, 0)),
                  pl.BlockSpec((None, 1, MLA_KV_RANK), lambda i: (l, 0, 0)),
                  pl.BlockSpec((None, MLA_KV_RANK, 2 * MLA_WIDTH), lambda i: (l, 0, 0)),
                  tab_spec, tab_spec, tab_spec],
        out_specs=[pl.BlockSpec((tm, kwidth), lambda i: (i, 0)),
                   pl.BlockSpec((tm, MLA_WIDTH), lambda i: (i, 0))],
        out_shape=[jax.ShapeDtypeStruct((r, kwidth), BF16),
                   jax.ShapeDtypeStruct((r, MLA_WIDTH), BF16)],
        compiler_params=_cparams(("parallel",)),
        name="mla_kv_proj",
    )(h2d, w_tail, kv_norm, wkv, tabs["c64r"], tabs["sa64r"], tabs["sb64r"])


def _softmax_pv(s, v_ones):
    m = jnp.max(s, axis=-1, keepdims=True)
    e = jnp.exp((s - m).astype(BF16))
    pv = jnp.dot(e, v_ones, preferred_element_type=F32)
    half = pv.shape[1] // 2
    return pv[:, :half] / pv[:, half:]


def _fill_v_ones(v_ref, vo_ref, n_heads, width):
    ones = jnp.ones((v_ref.shape[0], width), BF16)
    for h in range(n_heads):
        vo_ref[:, 2 * width * h:2 * width * h + width] = v_ref[:, width * h:width * (h + 1)]
        vo_ref[:, 2 * width * h + width:2 * width * (h + 1)] = ones


_NT = (((1,), (1,)), ((), ()))


def _diff_attn_kernel(lam_ref, sub_ref, q_ref, k_ref, v_ref, o_ref, vo_ref, *, n_ctx_tiles, ctx_len, lam_init):
    @pl.when(pl.program_id(1) == 0)
    def _():
        _fill_v_ones(v_ref, vo_ref, DIFF_HEADS, HEAD_DIM)

    lv = lam_ref[...]
    lam = (jnp.exp(jnp.sum(lv[0:1] * lv[1:2], axis=-1, keepdims=True))
           - jnp.exp(jnp.sum(lv[2:3] * lv[3:4], axis=-1, keepdims=True)) + lam_init)
    sub = sub_ref[...] * (1.0 - lam_init)

    def run(m_keys):
        for h in range(DIFF_HEADS):
            sl = slice(h * HEAD_DIM, (h + 1) * HEAD_DIM)
            q = q_ref[:, sl]
            k = k_ref[0:m_keys, sl]
            v = vo_ref[0:m_keys, 2 * h * HEAD_DIM:2 * (h + 1) * HEAD_DIM]
            first = lax.broadcasted_iota(jnp.int32, q.shape, 1) < DIFF_HALF
            zero = jnp.zeros_like(q)
            s1 = lax.dot_general(jnp.where(first, q, zero), k, _NT, preferred_element_type=F32)
            s2 = lax.dot_general(jnp.where(first, zero, q), k, _NT, preferred_element_type=F32)
            o = _softmax_pv(s1, v) - lam * _softmax_pv(s2, v)
            ms = jnp.mean(o * o, axis=-1, keepdims=True)
            o_ref[:, sl] = (o * lax.rsqrt(ms + EPS) * sub).astype(BF16)

    is_ctx = pl.program_id(1) < n_ctx_tiles

    @pl.when(is_ctx)
    def _():
        run(ctx_len)

    @pl.when(jnp.logical_not(is_ctx))
    def _():
        run(k_ref.shape[0])


def _diff_attn_call(p3d, lamv, subln, lam_init, tq, n_ctx_tiles, ctx_len):
    b, s, _ = p3d.shape
    return pl.pallas_call(
        functools.partial(_diff_attn_kernel, n_ctx_tiles=n_ctx_tiles, ctx_len=ctx_len, lam_init=lam_init),
        grid=(b, s // tq),
        in_specs=[pl.BlockSpec((4, DIFF_HALF), lambda bi, i: (0, 0)),
                  pl.BlockSpec((1, HEAD_DIM), lambda bi, i: (0, 0)),
                  pl.BlockSpec((None, tq, DIFF_WIDTH), lambda bi, i: (bi, i, 0)),
                  pl.BlockSpec((None, s, DIFF_WIDTH), lambda bi, i: (bi, 0, 1)),
                  pl.BlockSpec((None, s, DIFF_WIDTH), lambda bi, i: (bi, 0, 2))],
        out_specs=pl.BlockSpec((None, tq, DIFF_WIDTH), lambda bi, i: (bi, i, 0)),
        out_shape=jax.ShapeDtypeStruct((b, s, DIFF_WIDTH), BF16),
        scratch_shapes=[pltpu.VMEM((s, 2 * DIFF_WIDTH), BF16)],
        compiler_params=_cparams(("parallel", "arbitrary")),
        name="diff_attn",
    )(lamv, subln, p3d, p3d, p3d)


MLA_HEAD_GROUP = 4


def _mla_attn_kernel(q_ref, k_ref, v_ref, o_ref, vo_ref, *, n_ctx_tiles, ctx_len):
    @pl.when(pl.program_id(2) == 0)
    def _():
        _fill_v_ones(v_ref, vo_ref, MLA_HEAD_GROUP, MLA_V)

    def run(m_keys):
        for h in range(MLA_HEAD_GROUP):
            qk = slice(h * MLA_QK_PAD, (h + 1) * MLA_QK_PAD)
            vv = slice(h * MLA_V, (h + 1) * MLA_V)
            s = lax.dot_general(q_ref[:, qk], k_ref[0:m_keys, qk], _NT, preferred_element_type=F32)
            o_ref[:, vv] = _softmax_pv(s, vo_ref[0:m_keys, 2 * h * MLA_V:2 * (h + 1) * MLA_V]).astype(BF16)

    is_ctx = pl.program_id(2) < n_ctx_tiles

    @pl.when(is_ctx)
    def _():
        run(ctx_len)

    @pl.when(jnp.logical_not(is_ctx))
    def _():
        run(k_ref.shape[0])


def _mla_attn_call(qc, kc, vc, tq, n_ctx_tiles, ctx_len):
    b, s, _ = qc.shape
    qw = MLA_HEAD_GROUP * MLA_QK_PAD
    vw = MLA_HEAD_GROUP * MLA_V
    return pl.pallas_call(
        functools.partial(_mla_attn_kernel, n_ctx_tiles=n_ctx_tiles, ctx_len=ctx_len),
        grid=(b, MLA_HEADS // MLA_HEAD_GROUP, s // tq),
        in_specs=[pl.BlockSpec((None, tq, qw), lambda bi, g, i: (bi, i, g)),
                  pl.BlockSpec((None, s, qw), lambda bi, g, i: (bi, 0, g)),
                  pl.BlockSpec((None, s, vw), lambda bi, g, i: (bi, 0, g))],
        out_specs=pl.BlockSpec((None, tq, vw), lambda bi, g, i: (bi, i, g)),
        out_shape=jax.ShapeDtypeStruct((b, s, MLA_WIDTH), BF16),
        scratch_shapes=[pltpu.VMEM((s, 2 * vw), BF16)],
        compiler_params=_cparams(("parallel", "parallel", "arbitrary")),
        name="mla_attn",
    )(qc, kc, vc)


RET_HEAD_GROUP = 2
_TN = (((0,), (0,)), ((), ()))


def _ret_kernel(lg_ref, q_ref, k_ref, v_ref, g_ref, gn_ref, o_ref, dec_ref, st_ref, acc_ref, *, n_ctx_chunks):
    c = RET_CHUNK
    n_chunks = q_ref.shape[0] // c
    hp = pl.program_id(1)
    row = lax.broadcasted_iota(jnp.int32, (c, c), 0).astype(F32)
    col = lax.broadcasted_iota(jnp.int32, (c, c), 1).astype(F32)
    rel = row - col

    for hh in range(RET_HEAD_GROUP):
        lgf = lg_ref[0, hp * RET_HEAD_GROUP + hh]
        lgb = lg_ref[1, hp * RET_HEAD_GROUP + hh]
        dec_ref[hh, 0] = jnp.where(rel >= 0, jnp.exp(jnp.maximum(rel, 0.0) * lgf), 0.0)
        dec_ref[hh, 1] = jnp.exp((row + 1.0) * lgf)
        dec_ref[hh, 2] = jnp.exp((c - 1.0 - row) * lgf)
        dec_ref[hh, 3] = jnp.where(rel <= 0, jnp.exp(jnp.maximum(-rel, 0.0) * lgb), 0.0)
        dec_ref[hh, 4] = jnp.exp((c - row) * lgb)
        dec_ref[hh, 5] = jnp.exp(row * lgb)
        dec_ref[hh, 6] = jnp.exp(jnp.full((c, c), float(c), F32) * lgf)
        dec_ref[hh, 7] = jnp.exp(jnp.full((c, c), float(c), F32) * lgb)

    st_ref[...] = jnp.zeros_like(st_ref)
    acc_ref[...] = jnp.zeros_like(acc_ref)

    def chunk_update(hh, direction, chunk):
        base = 3 * direction
        sl = slice(hh * RET_DK, (hh + 1) * RET_DK)
        rows = pl.ds(pl.multiple_of(chunk * c, c), c)
        qc = q_ref[rows, sl]
        kc = k_ref[rows, sl]
        vc = v_ref[rows, sl]
        state = st_ref[hh, direction]
        att = lax.dot_general(qc, kc, _NT, preferred_element_type=F32) * dec_ref[hh, base]
        o = (jnp.dot(att.astype(BF16), vc, preferred_element_type=F32)
             + jnp.dot(qc, state.astype(BF16), preferred_element_type=F32) * dec_ref[hh, base + 1])
        kd = (kc.astype(F32) * dec_ref[hh, base + 2]).astype(BF16)
        st_ref[hh, direction] = (state * dec_ref[hh, 6 + direction]
                                 + lax.dot_general(kd, vc, _TN, preferred_element_type=F32))
        acc_ref[rows, sl] += o

    def step(t, carry):
        bwd = jnp.where(t < n_ctx_chunks, n_ctx_chunks - 1 - t, n_chunks - 1 - t + n_ctx_chunks)
        for hh in range(RET_HEAD_GROUP):
            chunk_update(hh, 0, t)
            chunk_update(hh, 1, bwd)
        return carry

    lax.fori_loop(0, n_chunks, step, 0)

    for hh in range(RET_HEAD_GROUP):
        sl = slice(hh * RET_DK, (hh + 1) * RET_DK)
        o = acc_ref[:, sl]
        mu = jnp.mean(o, axis=-1, keepdims=True)
        oc = o - mu
        y = oc * lax.rsqrt(jnp.mean(oc * oc, axis=-1, keepdims=True) + 1e-5) * gn_ref[:, sl]
        g = g_ref[:, sl].astype(F32)
        o_ref[:, sl] = (y * (g * jax.nn.sigmoid(g))).astype(BF16)


def _ret_call(p3d, log_gamma, gn_gain, n_ctx_chunks):
    b, s, _ = p3d.shape
    w = RET_HEAD_GROUP * RET_DK
    base = 3 * SEG // w

    def col(seg):
        return lambda bi, hp, lg: (bi, 0, base + seg * (SEG // w) + hp)

    grid_spec = pltpu.PrefetchScalarGridSpec(
        num_scalar_prefetch=1,
        grid=(b, RET_HEADS // RET_HEAD_GROUP),
        in_specs=[pl.BlockSpec((None, s, w), col(0)),
                  pl.BlockSpec((None, s, w), col(1)),
                  pl.BlockSpec((None, s, w), col(2)),
                  pl.BlockSpec((None, s, w), col(3)),
                  pl.BlockSpec((1, w), lambda bi, hp, lg: (0, hp))],
        out_specs=pl.BlockSpec((None, s, w), lambda bi, hp, lg: (bi, 0, hp)),
        scratch_shapes=[pltpu.VMEM((RET_HEAD_GROUP, 8, RET_CHUNK, RET_CHUNK), F32),
                        pltpu.VMEM((RET_HEAD_GROUP, 2, RET_DK, RET_DK), F32),
                        pltpu.VMEM((s, w), F32)],
    )
    return pl.pallas_call(
        functools.partial(_ret_kernel, n_ctx_chunks=n_ctx_chunks),
        grid_spec=grid_spec,
        out_shape=jax.ShapeDtypeStruct((b, s, RET_WIDTH), BF16),
        compiler_params=_cparams(("parallel", "arbitrary")),
        name="retention",
    )(log_gamma, p3d, p3d, p3d, p3d, gn_gain)


def _row_gate(gate_ref, tm, ctx_len, tiles_per_batch):
    i = pl.program_id(0)
    b = i // tiles_per_batch
    n_ctx = jnp.where(i % tiles_per_batch == 0, ctx_len, 0)
    is_ctx = lax.broadcasted_iota(jnp.int32, (tm, 1), 0) < n_ctx
    return jnp.where(is_ctx, gate_ref[pl.ds(CTX_MOD_ROW, 1), :], gate_ref[pl.ds(b, 1), :])


def _out_proj_kernel(a_ref, b_ref, c_ref, w_ref, x_ref, gate_ref, o_ref, *, ctx_len, tiles_per_batch):
    acc = jnp.dot(a_ref[...], w_ref[0:DIFF_WIDTH, :], preferred_element_type=F32)
    acc += jnp.dot(b_ref[...], w_ref[DIFF_WIDTH:DIFF_WIDTH + RET_WIDTH, :], preferred_element_type=F32)
    acc += jnp.dot(c_ref[...], w_ref[DIFF_WIDTH + RET_WIDTH:, :], preferred_element_type=F32)
    o_ref[...] = x_ref[...] + _row_gate(gate_ref, acc.shape[0], ctx_len, tiles_per_batch) * acc


def _out_proj_call(a2d, b2d, c2d, w_out, l, x2d, mods_l, gate_blk, tm, tn, ctx_len, tiles_per_batch):
    r, d = x2d.shape
    n_j = d // tn
    return pl.pallas_call(
        functools.partial(_out_proj_kernel, ctx_len=ctx_len, tiles_per_batch=tiles_per_batch),
        grid=(r // tm, n_j),
        in_specs=[pl.BlockSpec((tm, DIFF_WIDTH), lambda i, j: (i, 0)),
                  pl.BlockSpec((tm, RET_WIDTH), lambda i, j: (i, 0)),
                  pl.BlockSpec((tm, MLA_WIDTH), lambda i, j: (i, 0)),
                  pl.BlockSpec((None, d, tn), lambda i, j: (l, 0, j)),
                  pl.BlockSpec((tm, tn), lambda i, j: (i, j)),
                  pl.BlockSpec((MOD_ROWS, tn), lambda i, j: (0, gate_blk * n_j + j))],
        out_specs=pl.BlockSpec((tm, tn), lambda i, j: (i, j)),
        out_shape=jax.ShapeDtypeStruct((r, d), F32),
        compiler_params=_cparams(("parallel", "arbitrary")),
        name="out_proj",
    )(a2d, b2d, c2d, w_out, x2d, mods_l)


def _split_bf16(x):
    hi = x.astype(BF16)
    return hi, (x - hi.astype(F32)).astype(BF16)


def _moe_up_kernel(h_ref, w_ref, b_ref, gate_ref, o_ref):
    j = pl.program_id(1)
    n_e = w_ref.shape[0]
    e_row = lax.broadcasted_iota(jnp.int32, (LANES, n_e * EXPERT_FF), 0)
    e_col = lax.broadcasted_iota(jnp.int32, (LANES, n_e * EXPERT_FF), 1) // EXPERT_FF
    expand = (e_row == j * n_e + e_col).astype(BF16)
    g_hi, g_lo = _split_bf16(gate_ref[...])
    ge = (jnp.dot(g_hi, expand, preferred_element_type=F32)
          + jnp.dot(g_lo, expand, preferred_element_type=F32))
    h = h_ref[...]
    for e in range(n_e):
        lo = e * 2 * EXPERT_FF
        gu = jnp.dot(h, w_ref[e], preferred_element_type=F32) + b_ref[:, lo:lo + 2 * EXPERT_FF]
        glu = jnp.minimum(gu[:, :EXPERT_FF], SWIGLU_LIMIT)
        lin = jnp.clip(gu[:, EXPERT_FF:], -SWIGLU_LIMIT, SWIGLU_LIMIT)
        act = glu * jax.nn.sigmoid(SWIGLU_ALPHA * glu) * (lin + 1.0)
        sl = slice(e * EXPERT_FF, (e + 1) * EXPERT_FF)
        o_ref[:, sl] = (act * ge[:, sl]).astype(BF16)


def _moe_up_call(h2d, w_gu, b_gu, l, gates2d, tm):
    r, d = h2d.shape
    tn = SEG
    n_e = tn // (2 * EXPERT_FF)
    n = N_EXPERTS * 2 * EXPERT_FF
    return pl.pallas_call(
        _moe_up_kernel,
        grid=(r // tm, n // tn),
        in_specs=[pl.BlockSpec((tm, d), lambda i, j: (i, 0)),
                  pl.BlockSpec((None, n_e, d, 2 * EXPERT_FF), lambda i, j: (l, j, 0, 0)),
                  pl.BlockSpec((None, 1, tn), lambda i, j: (l, 0, j)),
                  pl.BlockSpec((tm, LANES), lambda i, j: (i, 0))],
        out_specs=pl.BlockSpec((tm, tn // 2), lambda i, j: (i, j)),
        out_shape=jax.ShapeDtypeStruct((r, n // 2), BF16),
        compiler_params=_cparams(("parallel", "arbitrary")),
        name="moe_up",
    )(h2d, w_gu, b_gu, gates2d)


def _moe_down_kernel(a_ref, w_ref, gates_ref, bd_ref, x_ref, gate_ref, o_ref, *, ctx_len, tiles_per_batch):
    acc = jnp.dot(a_ref[...], w_ref[...], preferred_element_type=F32)
    g_hi, g_lo = _split_bf16(gates_ref[...])
    acc += jnp.dot(g_hi, bd_ref[...], preferred_element_type=F32)
    acc += jnp.dot(g_lo, bd_ref[...], preferred_element_type=F32)
    o_ref[...] = x_ref[...] + _row_gate(gate_ref, acc.shape[0], ctx_len, tiles_per_batch) * acc


def _moe_down_call(a2d, w_down, gates2d, b_down, l, x2d, mods_l, gate_blk, tm, tn, ctx_len, tiles_per_batch):
    r, d = x2d.shape
    k = a2d.shape[1]
    n_j = d // tn
    return pl.pallas_call(
        functools.partial(_moe_down_kernel, ctx_len=ctx_len, tiles_per_batch=tiles_per_batch),
        grid=(r // tm, n_j),
        in_specs=[pl.BlockSpec((tm, k), lambda i, j: (i, 0)),
                  pl.BlockSpec((None, k, tn), lambda i, j: (l, 0, j)),
                  pl.BlockSpec((tm, LANES), lambda i, j: (i, 0)),
                  pl.BlockSpec((None, LANES, tn), lambda i, j: (l, 0, j)),
                  pl.BlockSpec((tm, tn), lambda i, j: (i, j)),
                  pl.BlockSpec((MOD_ROWS, tn), lambda i, j: (0, gate_blk * n_j + j))],
        out_specs=pl.BlockSpec((tm, tn), lambda i, j: (i, j)),
        out_shape=jax.ShapeDtypeStruct((r, d), F32),
        compiler_params=_cparams(("parallel", "arbitrary")),
        name="moe_down",
    )(a2d, w_down, gates2d, b_down, x2d, mods_l)


def _final_norm_kernel(x_ref, g_ref, o_ref):
    o_ref[...] = _rms_gain(x_ref[...], g_ref)


def _final_norm_call(xs, gain, ctx_len, tr):
    b, s, d = xs.shape
    n = s - ctx_len
    off = ctx_len // tr
    return pl.pallas_call(
        _final_norm_kernel,
        grid=(b, n // tr),
        in_specs=[pl.BlockSpec((None, tr, d), lambda bi, i: (bi, i + off, 0)),
                  pl.BlockSpec((1, d), lambda bi, i: (0, 0))],
        out_specs=pl.BlockSpec((None, tr, d), lambda bi, i: (bi, i, 0)),
        out_shape=jax.ShapeDtypeStruct((b, n, d), F32),
        compiler_params=_cparams(("parallel", "parallel")),
        name="final_norm",
    )(xs, gain)


def _rope_tables(n, ctx_len):
    rows = n // GRID_W
    row = jnp.repeat(jnp.arange(rows, dtype=F32), GRID_W)
    col = jnp.tile(jnp.arange(GRID_W, dtype=F32), rows)

    def cos_sin(rot_dim):
        pairs = rot_dim // 4
        inv = ROPE_BASE ** (-jnp.arange(pairs, dtype=F32) / pairs)
        ang = jnp.concatenate([row[:, None] * inv, col[:, None] * inv], axis=-1)
        cos = jnp.concatenate([jnp.ones((ctx_len, rot_dim // 2), F32), jnp.cos(ang)], axis=0)
        sin = jnp.concatenate([jnp.zeros((ctx_len, rot_dim // 2), F32), jnp.sin(ang)], axis=0)
        return cos, sin

    cos32, sin32 = cos_sin(DIFF_HALF)
    cos64, sin64 = cos_sin(RET_DK)
    z32 = jnp.zeros_like(sin32)
    one64 = jnp.ones((cos32.shape[0], 64), F32)
    z64 = jnp.zeros_like(one64)
    return {
        "c64": jnp.concatenate([cos32] * 4, axis=-1),
        "sa64": jnp.concatenate([-sin32, z32, -sin32, z32], axis=-1),
        "sb64": jnp.concatenate([z32, sin32, z32, sin32], axis=-1),
        "c64r": jnp.concatenate([cos32, cos32, one64], axis=-1),
        "sa64r": jnp.concatenate([-sin32, z32, z64], axis=-1),
        "sb64r": jnp.concatenate([z32, sin32, z64], axis=-1),
        "c128": jnp.concatenate([cos64, cos64], axis=-1),
        "s128": jnp.concatenate([-sin64, sin64], axis=-1),
    }


def _prep_weights(w_in, mla_wq_b, mla_wkv_b, w_out, router_w, router_b,
                  exp_w_gu, exp_b_gu, exp_w_down, exp_b_down):
    depth, d, in_width = w_in.shape
    w_tail = jnp.pad(w_in[:, :, MAIN_WIDTH:], ((0, 0), (0, 0), (0, TAIL_WIDTH - (in_width - MAIN_WIDTH)))).astype(BF16)
    wq = mla_wq_b.reshape(depth, MLA_Q_RANK, MLA_HEADS, MLA_NOPE + MLA_ROPE)
    wq = jnp.pad(wq, ((0, 0), (0, 0), (0, 0), (0, MLA_QK_PAD - MLA_NOPE - MLA_ROPE)))
    wq = wq.reshape(depth, MLA_Q_RANK, MLA_HEADS * MLA_QK_PAD).astype(BF16)
    wkv = mla_wkv_b.reshape(depth, MLA_KV_RANK, MLA_HEADS, MLA_NOPE + MLA_V)
    wkv = jnp.concatenate([wkv[..., :MLA_NOPE].reshape(depth, MLA_KV_RANK, -1),
                           wkv[..., MLA_NOPE:].reshape(depth, MLA_KV_RANK, -1)], axis=-1).astype(BF16)
    rw = jnp.pad(router_w, ((0, 0), (0, 0), (0, LANES - N_EXPERTS))).astype(BF16)
    rb = jnp.concatenate([router_b, jnp.full((depth, LANES - N_EXPERTS), NEG_BIG, F32)], axis=-1)
    return dict(
        w_in=w_in.astype(BF16), w_tail=w_tail, wq=wq, wkv=wkv, w_out=w_out.astype(BF16),
        rw=rw, rb=rb.reshape(depth, 1, LANES),
        w_gu=exp_w_gu.astype(BF16),
        b_gu=exp_b_gu.reshape(depth, 1, N_EXPERTS * 2 * EXPERT_FF),
        w_down=exp_w_down.astype(BF16).reshape(depth, N_EXPERTS * EXPERT_FF, d),
        b_down=jnp.pad(exp_b_down, ((0, 0), (0, LANES - N_EXPERTS), (0, 0))).astype(BF16))


def kernel(x, c, ctx, c_ctx, ada_w, ada_b, norm1, norm2, w_in, diff_lq1, diff_lk1, diff_lq2, diff_lk2, diff_subln, ret_decay, ret_norm, mla_q_norm, mla_kv_norm, mla_wq_b, mla_wkv_b, w_out, router_w, router_b, exp_w_gu, exp_b_gu, exp_w_down, exp_b_down, final_norm):
    batch, n, d = x.shape
    ctx_len = ctx.shape[1]
    depth = ada_w.shape[0]
    s = ctx_len + n
    r = batch * s
    assert d == D_MODEL and batch <= CTX_MOD_ROW and ctx_len % RET_CHUNK == 0 and n % RET_CHUNK == 0

    tq = _pick_tile(ctx_len, (256, 128))
    n_ctx_tiles = ctx_len // tq
    tm = _pick_tile(s, (768, 576, 384, 256, 128), at_least=ctx_len)
    tm_mla = _pick_tile(s, (384, 256, 128))
    tiles_per_batch = s // tm
    tn = 1024

    xs = jnp.concatenate([ctx, x], axis=1)
    mods = _ada_tables(_stack_conditioning(c, c_ctx), ada_w, ada_b)
    tabs = _rope_tables(n, ctx_len)
    log_gamma = jnp.log1p(-jnp.exp2(-ret_decay.astype(F32)))

    wts = _prep_weights(w_in, mla_wq_b, mla_wkv_b, w_out, router_w, router_b,
                        exp_w_gu, exp_b_gu, exp_w_down, exp_b_down)
    q_norm = mla_q_norm.reshape(depth, 1, -1)
    kv_norm = mla_kv_norm.reshape(depth, 1, -1)

    for l in range(depth):
        lam_init = 0.8 - 0.6 * math.exp(-0.3 * l)
        mods_l = mods[l]
        h = _norm_call(xs, norm1[l].reshape(1, d), mods_l, 0, 1, n_ctx_tiles, tq)
        h2d = h.reshape(r, d)
        p2d = _proj_call(h2d, wts["w_in"], l, tabs, tm, tiles_per_batch)
        p3d = p2d.reshape(batch, s, MAIN_WIDTH)
        qc = _mla_q_call(p2d, q_norm, wts["wq"], l, tabs, tm_mla, s // tm_mla)
        kc, vc = _mla_kv_call(h2d, wts["w_tail"], kv_norm, wts["wkv"], l, tabs, tm_mla, s // tm_mla)
        lamv = jnp.stack([diff_lq1[l], diff_lk1[l], diff_lq2[l], diff_lk2[l]]).astype(F32)
        a = _diff_attn_call(p3d, lamv, diff_subln[l].reshape(1, -1), lam_init, tq, n_ctx_tiles, ctx_len)
        bb = _ret_call(p3d, log_gamma[l], ret_norm[l].reshape(1, -1), ctx_len // RET_CHUNK)
        cc = _mla_attn_call(qc.reshape(batch, s, -1), kc.reshape(batch, s, -1), vc.reshape(batch, s, -1),
                            tq, n_ctx_tiles, ctx_len)
        x2d = _out_proj_call(a.reshape(r, -1), bb.reshape(r, -1), cc.reshape(r, -1), wts["w_out"], l,
                             xs.reshape(r, d), mods_l, 2, tm, tn, ctx_len, tiles_per_batch)
        xs = x2d.reshape(batch, s, d)
        h2, gates = _norm_call(xs, norm2[l].reshape(1, d), mods_l, 3, 4, n_ctx_tiles, tq,
                               router=(wts["rw"], wts["rb"], l))
        gates2d = gates.reshape(r, LANES)
        act = _moe_up_call(h2.reshape(r, d), wts["w_gu"], wts["b_gu"], l, gates2d, tm)
        x2d = _moe_down_call(act, wts["w_down"], gates2d, wts["b_down"], l, x2d, mods_l, 5, tm, tn,
                             ctx_len, tiles_per_batch)
        xs = x2d.reshape(batch, s, d)

    return _final_norm_call(xs, final_norm.reshape(1, d), ctx_len, tq)
```

```python
import functools
import math

import jax
import jax.numpy as jnp
from jax import lax
from jax.experimental import pallas as pl
from jax.experimental.pallas import tpu as pltpu

F32 = jnp.float32
BF16 = jnp.bfloat16

D_MODEL = 4096
GRID_W = 64
RET_CHUNK = 128
ROPE_BASE = 10000.0
EPS = 1e-6
HEAD_DIM = 128
DIFF_HEADS = 8
DIFF_HALF = 64
DIFF_WIDTH = 1024
RET_HEADS = 8
RET_DK = 128
RET_WIDTH = 1024
MLA_HEADS = 16
MLA_Q_RANK = 1024
MLA_KV_RANK = 512
MLA_NOPE = 128
MLA_ROPE = 64
MLA_V = 128
MLA_WIDTH = 2048
MLA_QK_PAD = 256
N_EXPERTS = 32
TOP_K = 4
EXPERT_FF = 128
SWIGLU_LIMIT = 7.0
SWIGLU_ALPHA = 1.702
MAIN_WIDTH = 8192
TAIL_WIDTH = 640
SEG = 1024

LANES = 128
VMEM_LIMIT_BYTES = 56 * 1024 * 1024
NEG_BIG = -1e30
CTX_MOD_ROW = 8
MOD_ROWS = 16


def _cparams(sem):
    return pltpu.CompilerParams(dimension_semantics=sem, vmem_limit_bytes=VMEM_LIMIT_BYTES)


def _pick_tile(total, candidates, at_least=1):
    for c in candidates:
        if total % c == 0 and c >= at_least:
            return c
    raise ValueError(f"no tile for {total}")


def _ada_kernel(c_ref, w_ref, b_ref, o_ref):
    c = c_ref[...]
    cs = (c * jax.nn.sigmoid(c)).astype(BF16)
    o_ref[...] = jnp.dot(cs, w_ref[...].astype(BF16), preferred_element_type=F32) + b_ref[...]


def _stack_conditioning(c, c_ctx):
    batch, d = c.shape
    return jnp.concatenate([c, jnp.zeros((CTX_MOD_ROW - batch, d), F32), c_ctx[None, :],
                            jnp.zeros((MOD_ROWS - CTX_MOD_ROW - 1, d), F32)], axis=0)


def _ada_tables(c_all, ada_w, ada_b):
    depth, d, n6 = ada_w.shape
    tn = 512
    return pl.pallas_call(
        _ada_kernel,
        grid=(depth, n6 // tn),
        in_specs=[
            pl.BlockSpec((MOD_ROWS, d), lambda l, j: (0, 0)),
            pl.BlockSpec((None, d, tn), lambda l, j: (l, 0, j)),
            pl.BlockSpec((None, 1, tn), lambda l, j: (l, 0, j)),
        ],
        out_specs=pl.BlockSpec((None, MOD_ROWS, tn), lambda l, j: (l, 0, j)),
        out_shape=jax.ShapeDtypeStruct((depth, MOD_ROWS, n6), F32),
        compiler_params=_cparams(("arbitrary", "arbitrary")),
        name="ada_tables",
    )(c_all, ada_w, ada_b.reshape(depth, 1, n6))


def _mod_norm(x, g_ref, sh_ref, sc_ref, row):
    ms = jnp.mean(x * x, axis=-1, keepdims=True)
    y = x * lax.rsqrt(ms + EPS) * g_ref[...]
    return y * (1.0 + sc_ref[pl.ds(row, 1), :]) + sh_ref[pl.ds(row, 1), :]


def _norm_kernel(x_ref, g_ref, sh_ref, sc_ref, o_ref, *, n_ctx_tiles):
    row = jnp.where(pl.program_id(1) < n_ctx_tiles, CTX_MOD_ROW, pl.program_id(0))
    o_ref[...] = _mod_norm(x_ref[...], g_ref, sh_ref, sc_ref, row).astype(BF16)


def _norm_router_kernel(x_ref, g_ref, sh_ref, sc_ref, rw_ref, rb_ref, o_ref, gate_ref, *, n_ctx_tiles):
    row = jnp.where(pl.program_id(1) < n_ctx_tiles, CTX_MOD_ROW, pl.program_id(0))
    h = _mod_norm(x_ref[...], g_ref, sh_ref, sc_ref, row).astype(BF16)
    o_ref[...] = h
    logits = jnp.dot(h, rw_ref[...], preferred_element_type=F32) + rb_ref[...]
    lane = lax.broadcasted_iota(jnp.int32, logits.shape, 1)
    work = logits
    num = jnp.zeros_like(logits)
    den = jnp.zeros((logits.shape[0], 1), F32)
    m0 = None
    for k in range(TOP_K):
        m = jnp.max(work, axis=-1, keepdims=True)
        idx = jnp.min(jnp.where(work == m, lane, LANES), axis=-1, keepdims=True)
        pick = lane == idx
        if k == 0:
            m0 = m
        e = jnp.exp(m - m0)
        num = num + jnp.where(pick, e, 0.0)
        den = den + e
        work = jnp.where(pick, -jnp.inf, work)
    gate_ref[...] = num / den


def _norm_call(xs, gain, mods_l, shift_blk, scale_blk, n_ctx_tiles, tr, router=None):
    b, s, d = xs.shape
    grid = (b, s // tr)
    x_spec = pl.BlockSpec((None, tr, d), lambda bi, i: (bi, i, 0))
    g_spec = pl.BlockSpec((1, d), lambda bi, i: (0, 0))
    sh_spec = pl.BlockSpec((MOD_ROWS, d), lambda bi, i: (0, shift_blk))
    sc_spec = pl.BlockSpec((MOD_ROWS, d), lambda bi, i: (0, scale_blk))
    h_shape = jax.ShapeDtypeStruct((b, s, d), BF16)
    if router is None:
        return pl.pallas_call(
            functools.partial(_norm_kernel, n_ctx_tiles=n_ctx_tiles),
            grid=grid,
            in_specs=[x_spec, g_spec, sh_spec, sc_spec],
            out_specs=x_spec,
            out_shape=h_shape,
            compiler_params=_cparams(("parallel", "parallel")),
            name="mod_norm",
        )(xs, gain, mods_l, mods_l)
    rw, rb, layer = router
    return pl.pallas_call(
        functools.partial(_norm_router_kernel, n_ctx_tiles=n_ctx_tiles),
        grid=grid,
        in_specs=[x_spec, g_spec, sh_spec, sc_spec,
                  pl.BlockSpec((None, d, LANES), lambda bi, i: (layer, 0, 0)),
                  pl.BlockSpec((None, 1, LANES), lambda bi, i: (layer, 0, 0))],
        out_specs=[x_spec, pl.BlockSpec((None, tr, LANES), lambda bi, i: (bi, i, 0))],
        out_shape=[h_shape, jax.ShapeDtypeStruct((b, s, LANES), F32)],
        compiler_params=_cparams(("parallel", "parallel")),
        name="mod_norm_router",
    )(xs, gain, mods_l, mods_l, rw, rb)


def _rope64(blk, c, sa, sb):
    return blk * c + pltpu.roll(blk, 96, 1) * sa + pltpu.roll(blk, 32, 1) * sb


def _proj_kernel(h_ref, w_ref, c64_ref, sa64_ref, sb64_ref, c128_ref, s128_ref, o_ref):
    j = pl.program_id(1)
    acc = jnp.dot(h_ref[...], w_ref[...], preferred_element_type=F32)
    n_blk = acc.shape[1] // LANES

    @pl.when((j == 0) | (j == 1))
    def _():
        scale = jnp.where(j == 0, DIFF_HALF ** -0.5, 1.0).astype(F32)
        c, sa, sb = c64_ref[...] * scale, sa64_ref[...] * scale, sb64_ref[...] * scale
        for t in range(n_blk):
            sl = slice(t * LANES, (t + 1) * LANES)
            o_ref[:, sl] = _rope64(acc[:, sl], c, sa, sb).astype(BF16)

    @pl.when((j == 3) | (j == 4))
    def _():
        scale = jnp.where(j == 4, RET_DK ** -0.5, 1.0).astype(F32)
        c, s = c128_ref[...] * scale, s128_ref[...] * scale
        for t in range(n_blk):
            sl = slice(t * LANES, (t + 1) * LANES)
            blk = acc[:, sl]
            o_ref[:, sl] = (blk * c + pltpu.roll(blk, 64, 1) * s).astype(BF16)

    @pl.when((j == 2) | (j >= 5))
    def _():
        o_ref[...] = acc.astype(BF16)


def _proj_call(h2d, w_in_bf, l, tabs, tm, tiles_per_batch):
    r, d = h2d.shape
    tab_spec = pl.BlockSpec((tm, LANES), lambda i, j: (i % tiles_per_batch, 0))
    return pl.pallas_call(
        _proj_kernel,
        grid=(r // tm, MAIN_WIDTH // SEG),
        in_specs=[pl.BlockSpec((tm, d), lambda i, j: (i, 0)),
                  pl.BlockSpec((None, d, SEG), lambda i, j: (l, 0, j)),
                  tab_spec, tab_spec, tab_spec, tab_spec, tab_spec],
        out_specs=pl.BlockSpec((tm, SEG), lambda i, j: (i, j)),
        out_shape=jax.ShapeDtypeStruct((r, MAIN_WIDTH), BF16),
        compiler_params=_cparams(("parallel", "arbitrary")),
        name="in_proj",
    )(h2d, w_in_bf, tabs["c64"], tabs["sa64"], tabs["sb64"], tabs["c128"], tabs["s128"])


def _rms_gain(x, g_ref):
    ms = jnp.mean(x * x, axis=-1, keepdims=True)
    return x * lax.rsqrt(ms + EPS) * g_ref[...]


def _mla_q_kernel(qa_ref, g_ref, w_ref, c_ref, sa_ref, sb_ref, o_ref):
    qn = _rms_gain(qa_ref[...].astype(F32), g_ref).astype(BF16)
    q = jnp.dot(qn, w_ref[...], preferred_element_type=F32)
    scale = (MLA_NOPE + MLA_ROPE) ** -0.5
    c, sa, sb = c_ref[...] * scale, sa_ref[...] * scale, sb_ref[...] * scale
    for h in range(MLA_HEADS):
        lo = h * MLA_QK_PAD
        o_ref[:, lo:lo + LANES] = (q[:, lo:lo + LANES] * scale).astype(BF16)
        o_ref[:, lo + LANES:lo + 2 * LANES] = _rope64(q[:, lo + LANES:lo + 2 * LANES], c, sa, sb).astype(BF16)


def _mla_kv_kernel(h_ref, wt_ref, g_ref, w_ref, c_ref, sa_ref, sb_ref, k_ref, v_ref):
    t = jnp.dot(h_ref[...], wt_ref[...], preferred_element_type=F32)
    kvn = _rms_gain(t[:, :MLA_KV_RANK], g_ref).astype(BF16)
    kr = _rope64(t[:, MLA_KV_RANK:MLA_KV_RANK + LANES], c_ref[...], sa_ref[...], sb_ref[...]).astype(BF16)
    kv = jnp.dot(kvn, w_ref[...], preferred_element_type=F32)
    kn_width = MLA_HEADS * MLA_NOPE
    for h in range(MLA_HEADS):
        lo = h * MLA_QK_PAD
        k_ref[:, lo:lo + LANES] = kv[:, h * MLA_NOPE:(h + 1) * MLA_NOPE].astype(BF16)
        k_ref[:, lo + LANES:lo + 2 * LANES] = kr
    v_ref[...] = kv[:, kn_width:].astype(BF16)


def _mla_q_call(p2d, q_norm, wq, l, tabs, tm, tiles_per_batch):
    r = p2d.shape[0]
    tab_spec = pl.BlockSpec((tm, LANES), lambda i: (i % tiles_per_batch, 0))
    width = MLA_HEADS * MLA_QK_PAD
    return pl.pallas_call(
        _mla_q_kernel,
        grid=(r // tm,),
        in_specs=[pl.BlockSpec((tm, MLA_Q_RANK), lambda i: (i, (MAIN_WIDTH - MLA_Q_RANK) // MLA_Q_RANK)),
                  pl.BlockSpec((None, 1, MLA_Q_RANK), lambda i: (l, 0, 0)),
                  pl.BlockSpec((None, MLA_Q_RANK, width), lambda i: (l, 0, 0)),
                  tab_spec, tab_spec, tab_spec],
        out_specs=pl.BlockSpec((tm, width), lambda i: (i, 0)),
        out_shape=jax.ShapeDtypeStruct((r, width), BF16),
        compiler_params=_cparams(("parallel",)),
        name="mla_q_proj",
    )(p2d, q_norm, wq, tabs["c64r"], tabs["sa64r"], tabs["sb64r"])


def _mla_kv_call(h2d, w_tail, kv_norm, wkv, l, tabs, tm, tiles_per_batch):
    r, d = h2d.shape
    tab_spec = pl.BlockSpec((tm, LANES), lambda i: (i % tiles_per_batch, 0))
    kwidth = MLA_HEADS * MLA_QK_PAD
    return pl.pallas_call(
        _mla_kv_kernel,
        grid=(r // tm,),
        in_specs=[pl.BlockSpec((tm, d), lambda i: (i, 0)),
                  pl.BlockSpec((None, d, TAIL_WIDTH), lambda i: (l, 0, 0)),
                  pl.BlockSpec((None, 1, MLA_KV_RANK), lambda i: (l, 0, 0)),
                  pl.BlockSpec((None, MLA_KV_RANK, 2 * MLA_WIDTH), lambda i: (l, 0, 0)),
                  tab_spec, tab_spec, tab_spec],
        out_specs=[pl.BlockSpec((tm, kwidth), lambda i: (i, 0)),
                   pl.BlockSpec((tm, MLA_WIDTH), lambda i: (i, 0))],
        out_shape=[jax.ShapeDtypeStruct((r, kwidth), BF16),
                   jax.ShapeDtypeStruct((r, MLA_WIDTH), BF16)],
        compiler_params=_cparams(("parallel",)),
        name="mla_kv_proj",
    )(h2d, w_tail, kv_norm, wkv, tabs["c64r"], tabs["sa64r"], tabs["sb64r"])


def _softmax_pv(s, v_ones):
    m = jnp.max(s, axis=-1, keepdims=True)
    e = jnp.exp((s - m).astype(BF16))
    pv = jnp.dot(e, v_ones, preferred_element_type=F32)
    half = pv.shape[1] // 2
    return pv[:, :half] / pv[:, half:]


def _fill_v_ones(v_ref, vo_ref, n_heads, width):
    ones = jnp.ones((v_ref.shape[0], width), BF16)
    for h in range(n_heads):
        vo_ref[:, 2 * width * h:2 * width * h + width] = v_ref[:, width * h:width * (h + 1)]
        vo_ref[:, 2 * width * h + width:2 * width * (h + 1)] = ones


_NT = (((1,), (1,)), ((), ()))


def _diff_attn_kernel(lam_ref, sub_ref, q_ref, k_ref, v_ref, o_ref, vo_ref, *, n_ctx_tiles, ctx_len, lam_init):
    @pl.when(pl.program_id(1) == 0)
    def _():
        _fill_v_ones(v_ref, vo_ref, DIFF_HEADS, HEAD_DIM)

    lv = lam_ref[...]
    lam = (jnp.exp(jnp.sum(lv[0:1] * lv[1:2], axis=-1, keepdims=True))
           - jnp.exp(jnp.sum(lv[2:3] * lv[3:4], axis=-1, keepdims=True)) + lam_init)
    sub = sub_ref[...] * (1.0 - lam_init)

    def run(m_keys):
        for h in range(DIFF_HEADS):
            sl = slice(h * HEAD_DIM, (h + 1) * HEAD_DIM)
            q = q_ref[:, sl]
            k = k_ref[0:m_keys, sl]
            v = vo_ref[0:m_keys, 2 * h * HEAD_DIM:2 * (h + 1) * HEAD_DIM]
            first = lax.broadcasted_iota(jnp.int32, q.shape, 1) < DIFF_HALF
            zero = jnp.zeros_like(q)
            s1 = lax.dot_general(jnp.where(first, q, zero), k, _NT, preferred_element_type=F32)
            s2 = lax.dot_general(jnp.where(first, zero, q), k, _NT, preferred_element_type=F32)
            o = _softmax_pv(s1, v) - lam * _softmax_pv(s2, v)
            ms = jnp.mean(o * o, axis=-1, keepdims=True)
            o_ref[:, sl] = (o * lax.rsqrt(ms + EPS) * sub).astype(BF16)

    is_ctx = pl.program_id(1) < n_ctx_tiles

    @pl.when(is_ctx)
    def _():
        run(ctx_len)

    @pl.when(jnp.logical_not(is_ctx))
    def _():
        run(k_ref.shape[0])


def _diff_attn_call(p3d, lamv, subln, lam_init, tq, n_ctx_tiles, ctx_len):
    b, s, _ = p3d.shape
    return pl.pallas_call(
        functools.partial(_diff_attn_kernel, n_ctx_tiles=n_ctx_tiles, ctx_len=ctx_len, lam_init=lam_init),
        grid=(b, s // tq),
        in_specs=[pl.BlockSpec((4, DIFF_HALF), lambda bi, i: (0, 0)),
                  pl.BlockSpec((1, HEAD_DIM), lambda bi, i: (0, 0)),
                  pl.BlockSpec((None, tq, DIFF_WIDTH), lambda bi, i: (bi, i, 0)),
                  pl.BlockSpec((None, s, DIFF_WIDTH), lambda bi, i: (bi, 0, 1)),
                  pl.BlockSpec((None, s, DIFF_WIDTH), lambda bi, i: (bi, 0, 2))],
        out_specs=pl.BlockSpec((None, tq, DIFF_WIDTH), lambda bi, i: (bi, i, 0)),
        out_shape=jax.ShapeDtypeStruct((b, s, DIFF_WIDTH), BF16),
        scratch_shapes=[pltpu.VMEM((s, 2 * DIFF_WIDTH), BF16)],
        compiler_params=_cparams(("parallel", "arbitrary")),
        name="diff_attn",
    )(lamv, subln, p3d, p3d, p3d)


MLA_HEAD_GROUP = 8


def _mla_attn_kernel(q_ref, k_ref, v_ref, o_ref, vo_ref, *, n_ctx_tiles, ctx_len):
    @pl.when(pl.program_id(2) == 0)
    def _():
        _fill_v_ones(v_ref, vo_ref, MLA_HEAD_GROUP, MLA_V)

    def run(m_keys):
        for h in range(MLA_HEAD_GROUP):
            qk = slice(h * MLA_QK_PAD, (h + 1) * MLA_QK_PAD)
            vv = slice(h * MLA_V, (h + 1) * MLA_V)
            s = lax.dot_general(q_ref[:, qk], k_ref[0:m_keys, qk], _NT, preferred_element_type=F32)
            o_ref[:, vv] = _softmax_pv(s, vo_ref[0:m_keys, 2 * h * MLA_V:2 * (h + 1) * MLA_V]).astype(BF16)

    is_ctx = pl.program_id(2) < n_ctx_tiles

    @pl.when(is_ctx)
    def _():
        run(ctx_len)

    @pl.when(jnp.logical_not(is_ctx))
    def _():
        run(k_ref.shape[0])


def _mla_attn_call(qc, kc, vc, tq, n_ctx_tiles, ctx_len):
    b, s, _ = qc.shape
    qw = MLA_HEAD_GROUP * MLA_QK_PAD
    vw = MLA_HEAD_GROUP * MLA_V
    return pl.pallas_call(
        functools.partial(_mla_attn_kernel, n_ctx_tiles=n_ctx_tiles, ctx_len=ctx_len),
        grid=(b, MLA_HEADS // MLA_HEAD_GROUP, s // tq),
        in_specs=[pl.BlockSpec((None, tq, qw), lambda bi, g, i: (bi, i, g)),
                  pl.BlockSpec((None, s, qw), lambda bi, g, i: (bi, 0, g)),
                  pl.BlockSpec((None, s, vw), lambda bi, g, i: (bi, 0, g))],
        out_specs=pl.BlockSpec((None, tq, vw), lambda bi, g, i: (bi, i, g)),
        out_shape=jax.ShapeDtypeStruct((b, s, MLA_WIDTH), BF16),
        scratch_shapes=[pltpu.VMEM((s, 2 * vw), BF16)],
        compiler_params=_cparams(("parallel", "parallel", "arbitrary")),
        name="mla_attn",
    )(qc, kc, vc)


RET_HEAD_GROUP = 4
_TN = (((0,), (0,)), ((), ()))


def _ret_kernel(lg_ref, q_ref, k_ref, v_ref, g_ref, gn_ref, o_ref, intra_ref, rowdec_ref, cdec_ref, st_ref, acc_ref,
                *, chunk, n_ctx_chunks):
    c = chunk
    n_chunks = q_ref.shape[0] // c
    hp = pl.program_id(1)
    row = lax.broadcasted_iota(jnp.int32, (c, c), 0).astype(F32)
    col = lax.broadcasted_iota(jnp.int32, (c, c), 1).astype(F32)
    rel = row - col
    pos = lax.broadcasted_iota(jnp.int32, (c, RET_DK), 0).astype(F32)
    span = jnp.full((RET_DK, RET_DK), float(c), F32)

    for hh in range(RET_HEAD_GROUP):
        lgf = lg_ref[0, hp * RET_HEAD_GROUP + hh]
        lgb = lg_ref[1, hp * RET_HEAD_GROUP + hh]
        intra_ref[hh, 0] = jnp.where(rel >= 0, jnp.exp(jnp.maximum(rel, 0.0) * lgf), 0.0)
        intra_ref[hh, 1] = jnp.where(rel <= 0, jnp.exp(jnp.maximum(-rel, 0.0) * lgb), 0.0)
        rowdec_ref[hh, 0] = jnp.exp((pos + 1.0) * lgf)
        rowdec_ref[hh, 1] = jnp.exp((c - 1.0 - pos) * lgf)
        rowdec_ref[hh, 2] = jnp.exp((c - pos) * lgb)
        rowdec_ref[hh, 3] = jnp.exp(pos * lgb)
        cdec_ref[hh, 0] = jnp.exp(span * lgf)
        cdec_ref[hh, 1] = jnp.exp(span * lgb)

    st_ref[...] = jnp.zeros_like(st_ref)
    acc_ref[...] = jnp.zeros_like(acc_ref)

    def chunk_update(hh, direction, idx):
        sl = slice(hh * RET_DK, (hh + 1) * RET_DK)
        rows = pl.ds(pl.multiple_of(idx * c, c), c)
        qc = q_ref[rows, sl]
        kc = k_ref[rows, sl]
        vc = v_ref[rows, sl]
        state = st_ref[hh, direction]
        att = lax.dot_general(qc, kc, _NT, preferred_element_type=F32) * intra_ref[hh, direction]
        o = (jnp.dot(att.astype(BF16), vc, preferred_element_type=F32)
             + jnp.dot(qc, state.astype(BF16), preferred_element_type=F32) * rowdec_ref[hh, 2 * direction])
        kd = (kc.astype(F32) * rowdec_ref[hh, 2 * direction + 1]).astype(BF16)
        st_ref[hh, direction] = (state * cdec_ref[hh, direction]
                                 + lax.dot_general(kd, vc, _TN, preferred_element_type=F32))
        acc_ref[rows, sl] += o

    def step(t, carry):
        bwd = jnp.where(t < n_ctx_chunks, n_ctx_chunks - 1 - t, n_chunks - 1 - t + n_ctx_chunks)
        for hh in range(RET_HEAD_GROUP):
            chunk_update(hh, 0, t)
            chunk_update(hh, 1, bwd)
        return carry

    lax.fori_loop(0, n_chunks, step, 0)

    for hh in range(RET_HEAD_GROUP):
        sl = slice(hh * RET_DK, (hh + 1) * RET_DK)
        o = acc_ref[:, sl]
        mu = jnp.mean(o, axis=-1, keepdims=True)
        oc = o - mu
        y = oc * lax.rsqrt(jnp.mean(oc * oc, axis=-1, keepdims=True) + 1e-5) * gn_ref[:, sl]
        g = g_ref[:, sl].astype(F32)
        o_ref[:, sl] = (y * (g * jax.nn.sigmoid(g))).astype(BF16)


def _ret_call(p3d, log_gamma, gn_gain, chunk, n_ctx_chunks):
    b, s, _ = p3d.shape
    w = RET_HEAD_GROUP * RET_DK
    base = 3 * SEG // w

    def col(seg):
        return lambda bi, hp, lg: (bi, 0, base + seg * (SEG // w) + hp)

    grid_spec = pltpu.PrefetchScalarGridSpec(
        num_scalar_prefetch=1,
        grid=(b, RET_HEADS // RET_HEAD_GROUP),
        in_specs=[pl.BlockSpec((None, s, w), col(0)),
                  pl.BlockSpec((None, s, w), col(1)),
                  pl.BlockSpec((None, s, w), col(2)),
                  pl.BlockSpec((None, s, w), col(3)),
                  pl.BlockSpec((1, w), lambda bi, hp, lg: (0, hp))],
        out_specs=pl.BlockSpec((None, s, w), lambda bi, hp, lg: (bi, 0, hp)),
        scratch_shapes=[pltpu.VMEM((RET_HEAD_GROUP, 2, chunk, chunk), F32),
                        pltpu.VMEM((RET_HEAD_GROUP, 4, chunk, RET_DK), F32),
                        pltpu.VMEM((RET_HEAD_GROUP, 2, RET_DK, RET_DK), F32),
                        pltpu.VMEM((RET_HEAD_GROUP, 2, RET_DK, RET_DK), F32),
                        pltpu.VMEM((s, w), F32)],
    )
    return pl.pallas_call(
        functools.partial(_ret_kernel, chunk=chunk, n_ctx_chunks=n_ctx_chunks),
        grid_spec=grid_spec,
        out_shape=jax.ShapeDtypeStruct((b, s, RET_WIDTH), BF16),
        compiler_params=_cparams(("parallel", "arbitrary")),
        name="retention",
    )(log_gamma, p3d, p3d, p3d, p3d, gn_gain)


def _row_gate(gate_ref, tm, ctx_len, tiles_per_batch):
    i = pl.program_id(0)
    b = i // tiles_per_batch
    n_ctx = jnp.where(i % tiles_per_batch == 0, ctx_len, 0)
    is_ctx = lax.broadcasted_iota(jnp.int32, (tm, 1), 0) < n_ctx
    return jnp.where(is_ctx, gate_ref[pl.ds(CTX_MOD_ROW, 1), :], gate_ref[pl.ds(b, 1), :])


def _out_proj_kernel(a_ref, b_ref, c_ref, w_ref, x_ref, gate_ref, o_ref, *, ctx_len, tiles_per_batch):
    acc = jnp.dot(a_ref[...], w_ref[0:DIFF_WIDTH, :], preferred_element_type=F32)
    acc += jnp.dot(b_ref[...], w_ref[DIFF_WIDTH:DIFF_WIDTH + RET_WIDTH, :], preferred_element_type=F32)
    acc += jnp.dot(c_ref[...], w_ref[DIFF_WIDTH + RET_WIDTH:, :], preferred_element_type=F32)
    o_ref[...] = x_ref[...] + _row_gate(gate_ref, acc.shape[0], ctx_len, tiles_per_batch) * acc


def _out_proj_call(a2d, b2d, c2d, w_out, l, x2d, mods_l, gate_blk, tm, tn, ctx_len, tiles_per_batch):
    r, d = x2d.shape
    n_j = d // tn
    return pl.pallas_call(
        functools.partial(_out_proj_kernel, ctx_len=ctx_len, tiles_per_batch=tiles_per_batch),
        grid=(r // tm, n_j),
        in_specs=[pl.BlockSpec((tm, DIFF_WIDTH), lambda i, j: (i, 0)),
                  pl.BlockSpec((tm, RET_WIDTH), lambda i, j: (i, 0)),
                  pl.BlockSpec((tm, MLA_WIDTH), lambda i, j: (i, 0)),
                  pl.BlockSpec((None, d, tn), lambda i, j: (l, 0, j)),
                  pl.BlockSpec((tm, tn), lambda i, j: (i, j)),
                  pl.BlockSpec((MOD_ROWS, tn), lambda i, j: (0, gate_blk * n_j + j))],
        out_specs=pl.BlockSpec((tm, tn), lambda i, j: (i, j)),
        out_shape=jax.ShapeDtypeStruct((r, d), F32),
        compiler_params=_cparams(("parallel", "arbitrary")),
        name="out_proj",
    )(a2d, b2d, c2d, w_out, x2d, mods_l)


def _split_bf16(x):
    hi = x.astype(BF16)
    return hi, (x - hi.astype(F32)).astype(BF16)


def _moe_up_kernel(h_ref, w_ref, b_ref, gate_ref, o_ref):
    j = pl.program_id(1)
    n_e = w_ref.shape[0]
    e_row = lax.broadcasted_iota(jnp.int32, (LANES, n_e * EXPERT_FF), 0)
    e_col = lax.broadcasted_iota(jnp.int32, (LANES, n_e * EXPERT_FF), 1) // EXPERT_FF
    expand = (e_row == j * n_e + e_col).astype(BF16)
    g_hi, g_lo = _split_bf16(gate_ref[...])
    ge = (jnp.dot(g_hi, expand, preferred_element_type=F32)
          + jnp.dot(g_lo, expand, preferred_element_type=F32))
    h = h_ref[...]
    for e in range(n_e):
        lo = e * 2 * EXPERT_FF
        gu = jnp.dot(h, w_ref[e], preferred_element_type=F32) + b_ref[:, lo:lo + 2 * EXPERT_FF]
        glu = jnp.minimum(gu[:, :EXPERT_FF], SWIGLU_LIMIT)
        lin = jnp.clip(gu[:, EXPERT_FF:], -SWIGLU_LIMIT, SWIGLU_LIMIT)
        act = glu * jax.nn.sigmoid(SWIGLU_ALPHA * glu) * (lin + 1.0)
        sl = slice(e * EXPERT_FF, (e + 1) * EXPERT_FF)
        o_ref[:, sl] = (act * ge[:, sl]).astype(BF16)


def _moe_up_call(h2d, w_gu, b_gu, l, gates2d, tm):
    r, d = h2d.shape
    tn = SEG
    n_e = tn // (2 * EXPERT_FF)
    n = N_EXPERTS * 2 * EXPERT_FF
    return pl.pallas_call(
        _moe_up_kernel,
        grid=(r // tm, n // tn),
        in_specs=[pl.BlockSpec((tm, d), lambda i, j: (i, 0)),
                  pl.BlockSpec((None, n_e, d, 2 * EXPERT_FF), lambda i, j: (l, j, 0, 0)),
                  pl.BlockSpec((None, 1, tn), lambda i, j: (l, 0, j)),
                  pl.BlockSpec((tm, LANES), lambda i, j: (i, 0))],
        out_specs=pl.BlockSpec((tm, tn // 2), lambda i, j: (i, j)),
        out_shape=jax.ShapeDtypeStruct((r, n // 2), BF16),
        compiler_params=_cparams(("parallel", "arbitrary")),
        name="moe_up",
    )(h2d, w_gu, b_gu, gates2d)


def _moe_down_kernel(a_ref, w_ref, gates_ref, bd_ref, x_ref, gate_ref, o_ref, *, ctx_len, tiles_per_batch):
    acc = jnp.dot(a_ref[...], w_ref[...], preferred_element_type=F32)
    g_hi, g_lo = _split_bf16(gates_ref[...])
    acc += jnp.dot(g_hi, bd_ref[...], preferred_element_type=F32)
    acc += jnp.dot(g_lo, bd_ref[...], preferred_element_type=F32)
    o_ref[...] = x_ref[...] + _row_gate(gate_ref, acc.shape[0], ctx_len, tiles_per_batch) * acc


def _moe_down_call(a2d, w_down, gates2d, b_down, l, x2d, mods_l, gate_blk, tm, tn, ctx_len, tiles_per_batch):
    r, d = x2d.shape
    k = a2d.shape[1]
    n_j = d // tn
    return pl.pallas_call(
        functools.partial(_moe_down_kernel, ctx_len=ctx_len, tiles_per_batch=tiles_per_batch),
        grid=(r // tm, n_j),
        in_specs=[pl.BlockSpec((tm, k), lambda i, j: (i, 0)),
                  pl.BlockSpec((None, k, tn), lambda i, j: (l, 0, j)),
                  pl.BlockSpec((tm, LANES), lambda i, j: (i, 0)),
                  pl.BlockSpec((None, LANES, tn), lambda i, j: (l, 0, j)),
                  pl.BlockSpec((tm, tn), lambda i, j: (i, j)),
                  pl.BlockSpec((MOD_ROWS, tn), lambda i, j: (0, gate_blk * n_j + j))],
        out_specs=pl.BlockSpec((tm, tn), lambda i, j: (i, j)),
        out_shape=jax.ShapeDtypeStruct((r, d), F32),
        compiler_params=_cparams(("parallel", "arbitrary")),
        name="moe_down",
    )(a2d, w_down, gates2d, b_down, x2d, mods_l)


def _final_norm_kernel(x_ref, g_ref, o_ref):
    o_ref[...] = _rms_gain(x_ref[...], g_ref)


def _final_norm_call(xs, gain, ctx_len, tr):
    b, s, d = xs.shape
    n = s - ctx_len
    off = ctx_len // tr
    return pl.pallas_call(
        _final_norm_kernel,
        grid=(b, n // tr),
        in_specs=[pl.BlockSpec((None, tr, d), lambda bi, i: (bi, i + off, 0)),
                  pl.BlockSpec((1, d), lambda bi, i: (0, 0))],
        out_specs=pl.BlockSpec((None, tr, d), lambda bi, i: (bi, i, 0)),
        out_shape=jax.ShapeDtypeStruct((b, n, d), F32),
        compiler_params=_cparams(("parallel", "parallel")),
        name="final_norm",
    )(xs, gain)


def _rope_tables(n, ctx_len):
    rows = n // GRID_W
    row = jnp.repeat(jnp.arange(rows, dtype=F32), GRID_W)
    col = jnp.tile(jnp.arange(GRID_W, dtype=F32), rows)

    def cos_sin(rot_dim):
        pairs = rot_dim // 4
        inv = ROPE_BASE ** (-jnp.arange(pairs, dtype=F32) / pairs)
        ang = jnp.concatenate([row[:, None] * inv, col[:, None] * inv], axis=-1)
        cos = jnp.concatenate([jnp.ones((ctx_len, rot_dim // 2), F32), jnp.cos(ang)], axis=0)
        sin = jnp.concatenate([jnp.zeros((ctx_len, rot_dim // 2), F32), jnp.sin(ang)], axis=0)
        return cos, sin

    cos32, sin32 = cos_sin(DIFF_HALF)
    cos64, sin64 = cos_sin(RET_DK)
    z32 = jnp.zeros_like(sin32)
    one64 = jnp.ones((cos32.shape[0], 64), F32)
    z64 = jnp.zeros_like(one64)
    return {
        "c64": jnp.concatenate([cos32] * 4, axis=-1),
        "sa64": jnp.concatenate([-sin32, z32, -sin32, z32], axis=-1),
        "sb64": jnp.concatenate([z32, sin32, z32, sin32], axis=-1),
        "c64r": jnp.concatenate([cos32, cos32, one64], axis=-1),
        "sa64r": jnp.concatenate([-sin32, z32, z64], axis=-1),
        "sb64r": jnp.concatenate([z32, sin32, z64], axis=-1),
        "c128": jnp.concatenate([cos64, cos64], axis=-1),
        "s128": jnp.concatenate([-sin64, sin64], axis=-1),
    }


def _prep_weights(w_in, mla_wq_b, mla_wkv_b, w_out, router_w, router_b,
                  exp_w_gu, exp_b_gu, exp_w_down, exp_b_down):
    depth, d, in_width = w_in.shape
    w_tail = jnp.pad(w_in[:, :, MAIN_WIDTH:], ((0, 0), (0, 0), (0, TAIL_WIDTH - (in_width - MAIN_WIDTH)))).astype(BF16)
    wq = mla_wq_b.reshape(depth, MLA_Q_RANK, MLA_HEADS, MLA_NOPE + MLA_ROPE)
    wq = jnp.pad(wq, ((0, 0), (0, 0), (0, 0), (0, MLA_QK_PAD - MLA_NOPE - MLA_ROPE)))
    wq = wq.reshape(depth, MLA_Q_RANK, MLA_HEADS * MLA_QK_PAD).astype(BF16)
    wkv = mla_wkv_b.reshape(depth, MLA_KV_RANK, MLA_HEADS, MLA_NOPE + MLA_V)
    wkv = jnp.concatenate([wkv[..., :MLA_NOPE].reshape(depth, MLA_KV_RANK, -1),
                           wkv[..., MLA_NOPE:].reshape(depth, MLA_KV_RANK, -1)], axis=-1).astype(BF16)
    rw = jnp.pad(router_w, ((0, 0), (0, 0), (0, LANES - N_EXPERTS))).astype(BF16)
    rb = jnp.concatenate([router_b, jnp.full((depth, LANES - N_EXPERTS), NEG_BIG, F32)], axis=-1)
    return dict(
        w_in=w_in.astype(BF16), w_tail=w_tail, wq=wq, wkv=wkv, w_out=w_out.astype(BF16),
        rw=rw, rb=rb.reshape(depth, 1, LANES),
        w_gu=exp_w_gu.astype(BF16),
        b_gu=exp_b_gu.reshape(depth, 1, N_EXPERTS * 2 * EXPERT_FF),
        w_down=exp_w_down.astype(BF16).reshape(depth, N_EXPERTS * EXPERT_FF, d),
        b_down=jnp.pad(exp_b_down, ((0, 0), (0, LANES - N_EXPERTS), (0, 0))).astype(BF16))


def kernel(x, c, ctx, c_ctx, ada_w, ada_b, norm1, norm2, w_in, diff_lq1, diff_lk1, diff_lq2, diff_lk2, diff_subln, ret_decay, ret_norm, mla_q_norm, mla_kv_norm, mla_wq_b, mla_wkv_b, w_out, router_w, router_b, exp_w_gu, exp_b_gu, exp_w_down, exp_b_down, final_norm):
    batch, n, d = x.shape
    ctx_len = ctx.shape[1]
    depth = ada_w.shape[0]
    s = ctx_len + n
    r = batch * s
    assert d == D_MODEL and batch <= CTX_MOD_ROW and ctx_len % RET_CHUNK == 0 and n % RET_CHUNK == 0

    tq = _pick_tile(ctx_len, (256, 128))
    n_ctx_tiles = ctx_len // tq
    tm = _pick_tile(s, (768, 576, 384, 256, 128), at_least=ctx_len)
    tm_mla = _pick_tile(s, (384, 256, 128))
    tiles_per_batch = s // tm
    tn = 1024
    ret_chunk = _pick_tile(math.gcd(ctx_len, n), (2 * RET_CHUNK, RET_CHUNK))

    xs = jnp.concatenate([ctx, x], axis=1)
    mods = _ada_tables(_stack_conditioning(c, c_ctx), ada_w, ada_b)
    tabs = _rope_tables(n, ctx_len)
    log_gamma = jnp.log1p(-jnp.exp2(-ret_decay.astype(F32)))

    wts = _prep_weights(w_in, mla_wq_b, mla_wkv_b, w_out, router_w, router_b,
                        exp_w_gu, exp_b_gu, exp_w_down, exp_b_down)
    q_norm = mla_q_norm.reshape(depth, 1, -1)
    kv_norm = mla_kv_norm.reshape(depth, 1, -1)

    for l in range(depth):
        lam_init = 0.8 - 0.6 * math.exp(-0.3 * l)
        mods_l = mods[l]
        h = _norm_call(xs, norm1[l].reshape(1, d), mods_l, 0, 1, n_ctx_tiles, tq)
        h2d = h.reshape(r, d)
        p2d = _proj_call(h2d, wts["w_in"], l, tabs, tm, tiles_per_batch)
        p3d = p2d.reshape(batch, s, MAIN_WIDTH)
        qc = _mla_q_call(p2d, q_norm, wts["wq"], l, tabs, tm_mla, s // tm_mla)
        kc, vc = _mla_kv_call(h2d, wts["w_tail"], kv_norm, wts["wkv"], l, tabs, tm_mla, s // tm_mla)
        lamv = jnp.stack([diff_lq1[l], diff_lk1[l], diff_lq2[l], diff_lk2[l]]).astype(F32)
        a = _diff_attn_call(p3d, lamv, diff_subln[l].reshape(1, -1), lam_init, tq, n_ctx_tiles, ctx_len)
        bb = _ret_call(p3d, log_gamma[l], ret_norm[l].reshape(1, -1), ret_chunk, ctx_len // ret_chunk)
        cc = _mla_attn_call(qc.reshape(batch, s, -1), kc.reshape(batch, s, -1), vc.reshape(batch, s, -1),
                            tq, n_ctx_tiles, ctx_len)
        x2d = _out_proj_call(a.reshape(r, -1), bb.reshape(r, -1), cc.reshape(r, -1), wts["w_out"], l,
                             xs.reshape(r, d), mods_l, 2, tm, tn, ctx_len, tiles_per_batch)
        xs = x2d.reshape(batch, s, d)
        h2, gates = _norm_call(xs, norm2[l].reshape(1, d), mods_l, 3, 4, n_ctx_tiles, tq,
                               router=(wts["rw"], wts["rb"], l))
        gates2d = gates.reshape(r, LANES)
        act = _moe_up_call(h2.reshape(r, d), wts["w_gu"], wts["b_gu"], l, gates2d, tm)
        x2d = _moe_down_call(act, wts["w_down"], gates2d, wts["b_down"], l, x2d, mods_l, 5, tm, tn,
                             ctx_len, tiles_per_batch)
        xs = x2d.reshape(batch, s, d)

    return _final_norm_call(xs, final_norm.reshape(1, d), ctx_len, tq)
```

```python
import functools
import math

import jax
import jax.numpy as jnp
from jax import lax
from jax.experimental import pallas as pl
from jax.experimental.pallas import tpu as pltpu

F32 = jnp.float32
BF16 = jnp.bfloat16

D_MODEL = 4096
GRID_W = 64
RET_CHUNK = 128
ROPE_BASE = 10000.0
EPS = 1e-6
HEAD_DIM = 128
DIFF_HEADS = 8
DIFF_HALF = 64
DIFF_WIDTH = 1024
RET_HEADS = 8
RET_DK = 128
RET_WIDTH = 1024
MLA_HEADS = 16
MLA_Q_RANK = 1024
MLA_KV_RANK = 512
MLA_NOPE = 128
MLA_ROPE = 64
MLA_V = 128
MLA_WIDTH = 2048
MLA_QK_PAD = 256
N_EXPERTS = 32
TOP_K = 4
EXPERT_FF = 128
SWIGLU_LIMIT = 7.0
SWIGLU_ALPHA = 1.702
MAIN_WIDTH = 8192
TAIL_WIDTH = 640
SEG = 1024

LANES = 128
VMEM_LIMIT_BYTES = 56 * 1024 * 1024
NEG_BIG = -1e30
CTX_MOD_ROW = 8
MOD_ROWS = 16


def _cparams(sem):
    return pltpu.CompilerParams(dimension_semantics=sem, vmem_limit_bytes=VMEM_LIMIT_BYTES)


def _pick_tile(total, candidates, at_least=1):
    for c in candidates:
        if total % c == 0 and c >= at_least:
            return c
    raise ValueError(f"no tile for {total}")


def _ada_kernel(c_ref, w_ref, b_ref, o_ref):
    c = c_ref[...]
    cs = (c * jax.nn.sigmoid(c)).astype(BF16)
    o_ref[...] = jnp.dot(cs, w_ref[...].astype(BF16), preferred_element_type=F32) + b_ref[...]


def _stack_conditioning(c, c_ctx):
    batch, d = c.shape
    return jnp.concatenate([c, jnp.zeros((CTX_MOD_ROW - batch, d), F32), c_ctx[None, :],
                            jnp.zeros((MOD_ROWS - CTX_MOD_ROW - 1, d), F32)], axis=0)


def _ada_tables(c_all, ada_w, ada_b):
    depth, d, n6 = ada_w.shape
    tn = 512
    return pl.pallas_call(
        _ada_kernel,
        grid=(depth, n6 // tn),
        in_specs=[
            pl.BlockSpec((MOD_ROWS, d), lambda l, j: (0, 0)),
            pl.BlockSpec((None, d, tn), lambda l, j: (l, 0, j)),
            pl.BlockSpec((None, 1, tn), lambda l, j: (l, 0, j)),
        ],
        out_specs=pl.BlockSpec((None, MOD_ROWS, tn), lambda l, j: (l, 0, j)),
        out_shape=jax.ShapeDtypeStruct((depth, MOD_ROWS, n6), F32),
        compiler_params=_cparams(("arbitrary", "arbitrary")),
        name="ada_tables",
    )(c_all, ada_w, ada_b.reshape(depth, 1, n6))


def _mod_norm(x, g_ref, sh_ref, sc_ref, row):
    ms = jnp.mean(x * x, axis=-1, keepdims=True)
    y = x * lax.rsqrt(ms + EPS) * g_ref[...]
    return y * (1.0 + sc_ref[pl.ds(row, 1), :]) + sh_ref[pl.ds(row, 1), :]


def _norm_kernel(x_ref, g_ref, sh_ref, sc_ref, o_ref, *, n_ctx_tiles):
    row = jnp.where(pl.program_id(1) < n_ctx_tiles, CTX_MOD_ROW, pl.program_id(0))
    o_ref[...] = _mod_norm(x_ref[...], g_ref, sh_ref, sc_ref, row).astype(BF16)


def _norm_router_kernel(x_ref, g_ref, sh_ref, sc_ref, rw_ref, rb_ref, o_ref, gate_ref, *, n_ctx_tiles):
    row = jnp.where(pl.program_id(1) < n_ctx_tiles, CTX_MOD_ROW, pl.program_id(0))
    h = _mod_norm(x_ref[...], g_ref, sh_ref, sc_ref, row).astype(BF16)
    o_ref[...] = h
    logits = jnp.dot(h, rw_ref[...], preferred_element_type=F32) + rb_ref[...]
    lane = lax.broadcasted_iota(jnp.int32, logits.shape, 1)
    work = logits
    num = jnp.zeros_like(logits)
    den = jnp.zeros((logits.shape[0], 1), F32)
    m0 = None
    for k in range(TOP_K):
        m = jnp.max(work, axis=-1, keepdims=True)
        idx = jnp.min(jnp.where(work == m, lane, LANES), axis=-1, keepdims=True)
        pick = lane == idx
        if k == 0:
            m0 = m
        e = jnp.exp(m - m0)
        num = num + jnp.where(pick, e, 0.0)
        den = den + e
        work = jnp.where(pick, -jnp.inf, work)
    gate_ref[...] = num / den


def _norm_call(xs, gain, mods_l, shift_blk, scale_blk, n_ctx_tiles, tr, router=None):
    b, s, d = xs.shape
    grid = (b, s // tr)
    x_spec = pl.BlockSpec((None, tr, d), lambda bi, i: (bi, i, 0))
    g_spec = pl.BlockSpec((1, d), lambda bi, i: (0, 0))
    sh_spec = pl.BlockSpec((MOD_ROWS, d), lambda bi, i: (0, shift_blk))
    sc_spec = pl.BlockSpec((MOD_ROWS, d), lambda bi, i: (0, scale_blk))
    h_shape = jax.ShapeDtypeStruct((b, s, d), BF16)
    if router is None:
        return pl.pallas_call(
            functools.partial(_norm_kernel, n_ctx_tiles=n_ctx_tiles),
            grid=grid,
            in_specs=[x_spec, g_spec, sh_spec, sc_spec],
            out_specs=x_spec,
            out_shape=h_shape,
            compiler_params=_cparams(("parallel", "parallel")),
            name="mod_norm",
        )(xs, gain, mods_l, mods_l)
    rw, rb, layer = router
    return pl.pallas_call(
        functools.partial(_norm_router_kernel, n_ctx_tiles=n_ctx_tiles),
        grid=grid,
        in_specs=[x_spec, g_spec, sh_spec, sc_spec,
                  pl.BlockSpec((None, d, LANES), lambda bi, i: (layer, 0, 0)),
                  pl.BlockSpec((None, 1, LANES), lambda bi, i: (layer, 0, 0))],
        out_specs=[x_spec, pl.BlockSpec((None, tr, LANES), lambda bi, i: (bi, i, 0))],
        out_shape=[h_shape, jax.ShapeDtypeStruct((b, s, LANES), F32)],
        compiler_params=_cparams(("parallel", "parallel")),
        name="mod_norm_router",
    )(xs, gain, mods_l, mods_l, rw, rb)


def _rope64(blk, c, sa, sb):
    return blk * c + pltpu.roll(blk, 96, 1) * sa + pltpu.roll(blk, 32, 1) * sb


def _proj_kernel(h_ref, w_ref, c64_ref, sa64_ref, sb64_ref, c128_ref, s128_ref, o_ref):
    j = pl.program_id(1)
    acc = jnp.dot(h_ref[...], w_ref[...], preferred_element_type=F32)
    n_blk = acc.shape[1] // LANES

    @pl.when((j == 0) | (j == 1))
    def _():
        scale = jnp.where(j == 0, DIFF_HALF ** -0.5, 1.0).astype(F32)
        c, sa, sb = c64_ref[...] * scale, sa64_ref[...] * scale, sb64_ref[...] * scale
        for t in range(n_blk):
            sl = slice(t * LANES, (t + 1) * LANES)
            o_ref[:, sl] = _rope64(acc[:, sl], c, sa, sb).astype(BF16)

    @pl.when((j == 3) | (j == 4))
    def _():
        scale = jnp.where(j == 4, RET_DK ** -0.5, 1.0).astype(F32)
        c, s = c128_ref[...] * scale, s128_ref[...] * scale
        for t in range(n_blk):
            sl = slice(t * LANES, (t + 1) * LANES)
            blk = acc[:, sl]
            o_ref[:, sl] = (blk * c + pltpu.roll(blk, 64, 1) * s).astype(BF16)

    @pl.when((j == 2) | (j >= 5))
    def _():
        o_ref[...] = acc.astype(BF16)


def _proj_call(h2d, w_in_bf, l, tabs, tm, tiles_per_batch):
    r, d = h2d.shape
    tab_spec = pl.BlockSpec((tm, LANES), lambda i, j: (i % tiles_per_batch, 0))
    return pl.pallas_call(
        _proj_kernel,
        grid=(r // tm, MAIN_WIDTH // SEG),
        in_specs=[pl.BlockSpec((tm, d), lambda i, j: (i, 0)),
                  pl.BlockSpec((None, d, SEG), lambda i, j: (l, 0, j)),
                  tab_spec, tab_spec, tab_spec, tab_spec, tab_spec],
        out_specs=pl.BlockSpec((tm, SEG), lambda i, j: (i, j)),
        out_shape=jax.ShapeDtypeStruct((r, MAIN_WIDTH), BF16),
        compiler_params=_cparams(("parallel", "arbitrary")),
        name="in_proj",
    )(h2d, w_in_bf, tabs["c64"], tabs["sa64"], tabs["sb64"], tabs["c128"], tabs["s128"])


def _rms_gain(x, g_ref):
    ms = jnp.mean(x * x, axis=-1, keepdims=True)
    return x * lax.rsqrt(ms + EPS) * g_ref[...]


def _mla_q_kernel(qa_ref, g_ref, w_ref, c_ref, sa_ref, sb_ref, o_ref):
    qn = _rms_gain(qa_ref[...].astype(F32), g_ref).astype(BF16)
    q = jnp.dot(qn, w_ref[...], preferred_element_type=F32)
    scale = (MLA_NOPE + MLA_ROPE) ** -0.5
    c, sa, sb = c_ref[...] * scale, sa_ref[...] * scale, sb_ref[...] * scale
    for h in range(MLA_HEADS):
        lo = h * MLA_QK_PAD
        o_ref[:, lo:lo + LANES] = (q[:, lo:lo + LANES] * scale).astype(BF16)
        o_ref[:, lo + LANES:lo + 2 * LANES] = _rope64(q[:, lo + LANES:lo + 2 * LANES], c, sa, sb).astype(BF16)


def _mla_kv_kernel(h_ref, wt_ref, g_ref, w_ref, c_ref, sa_ref, sb_ref, k_ref, v_ref):
    t = jnp.dot(h_ref[...], wt_ref[...], preferred_element_type=F32)
    kvn = _rms_gain(t[:, :MLA_KV_RANK], g_ref).astype(BF16)
    kr = _rope64(t[:, MLA_KV_RANK:MLA_KV_RANK + LANES], c_ref[...], sa_ref[...], sb_ref[...]).astype(BF16)
    kv = jnp.dot(kvn, w_ref[...], preferred_element_type=F32)
    kn_width = MLA_HEADS * MLA_NOPE
    for h in range(MLA_HEADS):
        lo = h * MLA_QK_PAD
        k_ref[:, lo:lo + LANES] = kv[:, h * MLA_NOPE:(h + 1) * MLA_NOPE].astype(BF16)
        k_ref[:, lo + LANES:lo + 2 * LANES] = kr
    v_ref[...] = kv[:, kn_width:].astype(BF16)


def _mla_q_call(p2d, q_norm, wq, l, tabs, tm, tiles_per_batch):
    r = p2d.shape[0]
    tab_spec = pl.BlockSpec((tm, LANES), lambda i: (i % tiles_per_batch, 0))
    width = MLA_HEADS * MLA_QK_PAD
    return pl.pallas_call(
        _mla_q_kernel,
        grid=(r // tm,),
        in_specs=[pl.BlockSpec((tm, MLA_Q_RANK), lambda i: (i, (MAIN_WIDTH - MLA_Q_RANK) // MLA_Q_RANK)),
                  pl.BlockSpec((None, 1, MLA_Q_RANK), lambda i: (l, 0, 0)),
                  pl.BlockSpec((None, MLA_Q_RANK, width), lambda i: (l, 0, 0)),
                  tab_spec, tab_spec, tab_spec],
        out_specs=pl.BlockSpec((tm, width), lambda i: (i, 0)),
        out_shape=jax.ShapeDtypeStruct((r, width), BF16),
        compiler_params=_cparams(("parallel",)),
        name="mla_q_proj",
    )(p2d, q_norm, wq, tabs["c64r"], tabs["sa64r"], tabs["sb64r"])


def _mla_kv_call(h2d, w_tail, kv_norm, wkv, l, tabs, tm, tiles_per_batch):
    r, d = h2d.shape
    tab_spec = pl.BlockSpec((tm, LANES), lambda i: (i % tiles_per_batch, 0))
    kwidth = MLA_HEADS * MLA_QK_PAD
    return pl.pallas_call(
        _mla_kv_kernel,
        grid=(r // tm,),
        in_specs=[pl.BlockSpec((tm, d), lambda i: (i, 0)),
                  pl.BlockSpec((None, d, TAIL_WIDTH), lambda i: (l, 0, 0)),
                  pl.BlockSpec((None, 1, MLA_KV_RANK), lambda i: (l, 0, 0)),
                  pl.BlockSpec((None, MLA_KV_RANK, 2 * MLA_WIDTH), lambda i: (l, 0, 0)),
                  tab_spec, tab_spec, tab_spec],
        out_specs=[pl.BlockSpec((tm, kwidth), lambda i: (i, 0)),
                   pl.BlockSpec((tm, MLA_WIDTH), lambda i: (i, 0))],
        out_shape=[jax.ShapeDtypeStruct((r, kwidth), BF16),
                   jax.ShapeDtypeStruct((r, MLA_WIDTH), BF16)],
        compiler_params=_cparams(("parallel",)),
        name="mla_kv_proj",
    )(h2d, w_tail, kv_norm, wkv, tabs["c64r"], tabs["sa64r"], tabs["sb64r"])


KEY_CHUNK = 768


def _softmax_pv(q, k_ref, vo_ref, m_keys, k_cols, v_cols):
    chunk = _pick_tile(m_keys, (KEY_CHUNK, 512, 384, 256, 128))
    m_run = None
    acc = None
    for c0 in range(0, m_keys, chunk):
        s = lax.dot_general(q, k_ref[c0:c0 + chunk, k_cols], _NT, preferred_element_type=F32)
        m_c = jnp.max(s, axis=-1, keepdims=True)
        v_ones = vo_ref[c0:c0 + chunk, v_cols]
        if m_run is None:
            m_run = m_c
            acc = jnp.dot(jnp.exp((s - m_run).astype(BF16)), v_ones, preferred_element_type=F32)
        else:
            m_new = jnp.maximum(m_run, m_c)
            acc = (acc * jnp.exp(m_run - m_new)
                   + jnp.dot(jnp.exp((s - m_new).astype(BF16)), v_ones, preferred_element_type=F32))
            m_run = m_new
    half = acc.shape[1] // 2
    return acc[:, :half] / acc[:, half:]


def _fill_v_ones(v_ref, vo_ref, n_heads, width):
    ones = jnp.ones((v_ref.shape[0], width), BF16)
    for h in range(n_heads):
        vo_ref[:, 2 * width * h:2 * width * h + width] = v_ref[:, width * h:width * (h + 1)]
        vo_ref[:, 2 * width * h + width:2 * width * (h + 1)] = ones


_NT = (((1,), (1,)), ((), ()))


def _diff_attn_kernel(lam_ref, sub_ref, q_ref, k_ref, v_ref, o_ref, vo_ref, *, n_ctx_tiles, ctx_len, lam_init):
    @pl.when(pl.program_id(1) == 0)
    def _():
        _fill_v_ones(v_ref, vo_ref, DIFF_HEADS, HEAD_DIM)

    lv = lam_ref[...]
    lam = (jnp.exp(jnp.sum(lv[0:1] * lv[1:2], axis=-1, keepdims=True))
           - jnp.exp(jnp.sum(lv[2:3] * lv[3:4], axis=-1, keepdims=True)) + lam_init)
    sub = sub_ref[...] * (1.0 - lam_init)

    def run(m_keys):
        for h in range(DIFF_HEADS):
            sl = slice(h * HEAD_DIM, (h + 1) * HEAD_DIM)
            q = q_ref[:, sl]
            v_cols = slice(2 * h * HEAD_DIM, 2 * (h + 1) * HEAD_DIM)
            first = lax.broadcasted_iota(jnp.int32, q.shape, 1) < DIFF_HALF
            zero = jnp.zeros_like(q)
            o = (_softmax_pv(jnp.where(first, q, zero), k_ref, vo_ref, m_keys, sl, v_cols)
                 - lam * _softmax_pv(jnp.where(first, zero, q), k_ref, vo_ref, m_keys, sl, v_cols))
            ms = jnp.mean(o * o, axis=-1, keepdims=True)
            o_ref[:, sl] = (o * lax.rsqrt(ms + EPS) * sub).astype(BF16)

    is_ctx = pl.program_id(1) < n_ctx_tiles

    @pl.when(is_ctx)
    def _():
        run(ctx_len)

    @pl.when(jnp.logical_not(is_ctx))
    def _():
        run(k_ref.shape[0])


def _diff_attn_call(p3d, lamv, subln, lam_init, tq, n_ctx_tiles, ctx_len):
    b, s, _ = p3d.shape
    return pl.pallas_call(
        functools.partial(_diff_attn_kernel, n_ctx_tiles=n_ctx_tiles, ctx_len=ctx_len, lam_init=lam_init),
        grid=(b, s // tq),
        in_specs=[pl.BlockSpec((4, DIFF_HALF), lambda bi, i: (0, 0)),
                  pl.BlockSpec((1, HEAD_DIM), lambda bi, i: (0, 0)),
                  pl.BlockSpec((None, tq, DIFF_WIDTH), lambda bi, i: (bi, i, 0)),
                  pl.BlockSpec((None, s, DIFF_WIDTH), lambda bi, i: (bi, 0, 1)),
                  pl.BlockSpec((None, s, DIFF_WIDTH), lambda bi, i: (bi, 0, 2))],
        out_specs=pl.BlockSpec((None, tq, DIFF_WIDTH), lambda bi, i: (bi, i, 0)),
        out_shape=jax.ShapeDtypeStruct((b, s, DIFF_WIDTH), BF16),
        scratch_shapes=[pltpu.VMEM((s, 2 * DIFF_WIDTH), BF16)],
        compiler_params=_cparams(("parallel", "arbitrary")),
        name="diff_attn",
    )(lamv, subln, p3d, p3d, p3d)


MLA_HEAD_GROUP = 8


def _mla_attn_kernel(q_ref, k_ref, v_ref, o_ref, vo_ref, *, n_ctx_tiles, ctx_len):
    @pl.when(pl.program_id(2) == 0)
    def _():
        _fill_v_ones(v_ref, vo_ref, MLA_HEAD_GROUP, MLA_V)

    def run(m_keys):
        for h in range(MLA_HEAD_GROUP):
            qk = slice(h * MLA_QK_PAD, (h + 1) * MLA_QK_PAD)
            vv = slice(h * MLA_V, (h + 1) * MLA_V)
            v_cols = slice(2 * h * MLA_V, 2 * (h + 1) * MLA_V)
            o_ref[:, vv] = _softmax_pv(q_ref[:, qk], k_ref, vo_ref, m_keys, qk, v_cols).astype(BF16)

    is_ctx = pl.program_id(2) < n_ctx_tiles

    @pl.when(is_ctx)
    def _():
        run(ctx_len)

    @pl.when(jnp.logical_not(is_ctx))
    def _():
        run(k_ref.shape[0])


def _mla_attn_call(qc, kc, vc, tq, n_ctx_tiles, ctx_len):
    b, s, _ = qc.shape
    qw = MLA_HEAD_GROUP * MLA_QK_PAD
    vw = MLA_HEAD_GROUP * MLA_V
    return pl.pallas_call(
        functools.partial(_mla_attn_kernel, n_ctx_tiles=n_ctx_tiles, ctx_len=ctx_len),
        grid=(b, MLA_HEADS // MLA_HEAD_GROUP, s // tq),
        in_specs=[pl.BlockSpec((None, tq, qw), lambda bi, g, i: (bi, i, g)),
                  pl.BlockSpec((None, s, qw), lambda bi, g, i: (bi, 0, g)),
                  pl.BlockSpec((None, s, vw), lambda bi, g, i: (bi, 0, g))],
        out_specs=pl.BlockSpec((None, tq, vw), lambda bi, g, i: (bi, i, g)),
        out_shape=jax.ShapeDtypeStruct((b, s, MLA_WIDTH), BF16),
        scratch_shapes=[pltpu.VMEM((s, 2 * vw), BF16)],
        compiler_params=_cparams(("parallel", "parallel", "arbitrary")),
        name="mla_attn",
    )(qc, kc, vc)


RET_HEAD_GROUP = 4
_TN = (((0,), (0,)), ((), ()))


def _ret_kernel(lg_ref, q_ref, k_ref, v_ref, g_ref, gn_ref, o_ref, intra_ref, rowdec_ref, cdec_ref, st_ref, acc_ref,
                *, chunk, n_ctx_chunks):
    c = chunk
    n_chunks = q_ref.shape[0] // c
    hp = pl.program_id(1)
    row = lax.broadcasted_iota(jnp.int32, (c, c), 0).astype(F32)
    col = lax.broadcasted_iota(jnp.int32, (c, c), 1).astype(F32)
    rel = row - col
    pos = lax.broadcasted_iota(jnp.int32, (c, RET_DK), 0).astype(F32)
    span = jnp.full((RET_DK, RET_DK), float(c), F32)

    for hh in range(RET_HEAD_GROUP):
        lgf = lg_ref[0, hp * RET_HEAD_GROUP + hh]
        lgb = lg_ref[1, hp * RET_HEAD_GROUP + hh]
        intra_ref[hh, 0] = jnp.where(rel >= 0, jnp.exp(jnp.maximum(rel, 0.0) * lgf), 0.0)
        intra_ref[hh, 1] = jnp.where(rel <= 0, jnp.exp(jnp.maximum(-rel, 0.0) * lgb), 0.0)
        rowdec_ref[hh, 0] = jnp.exp((pos + 1.0) * lgf)
        rowdec_ref[hh, 1] = jnp.exp((c - 1.0 - pos) * lgf)
        rowdec_ref[hh, 2] = jnp.exp((c - pos) * lgb)
        rowdec_ref[hh, 3] = jnp.exp(pos * lgb)
        cdec_ref[hh, 0] = jnp.exp(span * lgf)
        cdec_ref[hh, 1] = jnp.exp(span * lgb)

    st_ref[...] = jnp.zeros_like(st_ref)
    acc_ref[...] = jnp.zeros_like(acc_ref)

    def chunk_update(hh, direction, idx):
        sl = slice(hh * RET_DK, (hh + 1) * RET_DK)
        rows = pl.ds(pl.multiple_of(idx * c, c), c)
        qc = q_ref[rows, sl]
        kc = k_ref[rows, sl]
        vc = v_ref[rows, sl]
        state = st_ref[hh, direction]
        att = lax.dot_general(qc, kc, _NT, preferred_element_type=F32) * intra_ref[hh, direction]
        o = (jnp.dot(att.astype(BF16), vc, preferred_element_type=F32)
             + jnp.dot(qc, state.astype(BF16), preferred_element_type=F32) * rowdec_ref[hh, 2 * direction])
        kd = (kc.astype(F32) * rowdec_ref[hh, 2 * direction + 1]).astype(BF16)
        st_ref[hh, direction] = (state * cdec_ref[hh, direction]
                                 + lax.dot_general(kd, vc, _TN, preferred_element_type=F32))
        acc_ref[rows, sl] += o

    def step(t, carry):
        bwd = jnp.where(t < n_ctx_chunks, n_ctx_chunks - 1 - t, n_chunks - 1 - t + n_ctx_chunks)
        for hh in range(RET_HEAD_GROUP):
            chunk_update(hh, 0, t)
            chunk_update(hh, 1, bwd)
        return carry

    lax.fori_loop(0, n_chunks, step, 0)

    for hh in range(RET_HEAD_GROUP):
        sl = slice(hh * RET_DK, (hh + 1) * RET_DK)
        o = acc_ref[:, sl]
        mu = jnp.mean(o, axis=-1, keepdims=True)
        oc = o - mu
        y = oc * lax.rsqrt(jnp.mean(oc * oc, axis=-1, keepdims=True) + 1e-5) * gn_ref[:, sl]
        g = g_ref[:, sl].astype(F32)
        o_ref[:, sl] = (y * (g * jax.nn.sigmoid(g))).astype(BF16)


def _ret_call(p3d, log_gamma, gn_gain, chunk, n_ctx_chunks):
    b, s, _ = p3d.shape
    w = RET_HEAD_GROUP * RET_DK
    base = 3 * SEG // w

    def col(seg):
        return lambda bi, hp, lg: (bi, 0, base + seg * (SEG // w) + hp)

    grid_spec = pltpu.PrefetchScalarGridSpec(
        num_scalar_prefetch=1,
        grid=(b, RET_HEADS // RET_HEAD_GROUP),
        in_specs=[pl.BlockSpec((None, s, w), col(0)),
                  pl.BlockSpec((None, s, w), col(1)),
                  pl.BlockSpec((None, s, w), col(2)),
                  pl.BlockSpec((None, s, w), col(3)),
                  pl.BlockSpec((1, w), lambda bi, hp, lg: (0, hp))],
        out_specs=pl.BlockSpec((None, s, w), lambda bi, hp, lg: (bi, 0, hp)),
        scratch_shapes=[pltpu.VMEM((RET_HEAD_GROUP, 2, chunk, chunk), F32),
                        pltpu.VMEM((RET_HEAD_GROUP, 4, chunk, RET_DK), F32),
                        pltpu.VMEM((RET_HEAD_GROUP, 2, RET_DK, RET_DK), F32),
                        pltpu.VMEM((RET_HEAD_GROUP, 2, RET_DK, RET_DK), F32),
                        pltpu.VMEM((s, w), F32)],
    )
    return pl.pallas_call(
        functools.partial(_ret_kernel, chunk=chunk, n_ctx_chunks=n_ctx_chunks),
        grid_spec=grid_spec,
        out_shape=jax.ShapeDtypeStruct((b, s, RET_WIDTH), BF16),
        compiler_params=_cparams(("parallel", "arbitrary")),
        name="retention",
    )(log_gamma, p3d, p3d, p3d, p3d, gn_gain)


def _row_gate(gate_ref, tm, ctx_len, tiles_per_batch):
    i = pl.program_id(0)
    b = i // tiles_per_batch
    n_ctx = jnp.where(i % tiles_per_batch == 0, ctx_len, 0)
    is_ctx = lax.broadcasted_iota(jnp.int32, (tm, 1), 0) < n_ctx
    return jnp.where(is_ctx, gate_ref[pl.ds(CTX_MOD_ROW, 1), :], gate_ref[pl.ds(b, 1), :])


def _out_proj_kernel(a_ref, b_ref, c_ref, w_ref, x_ref, gate_ref, o_ref, *, ctx_len, tiles_per_batch):
    acc = jnp.dot(a_ref[...], w_ref[0:DIFF_WIDTH, :], preferred_element_type=F32)
    acc += jnp.dot(b_ref[...], w_ref[DIFF_WIDTH:DIFF_WIDTH + RET_WIDTH, :], preferred_element_type=F32)
    acc += jnp.dot(c_ref[...], w_ref[DIFF_WIDTH + RET_WIDTH:, :], preferred_element_type=F32)
    o_ref[...] = x_ref[...] + _row_gate(gate_ref, acc.shape[0], ctx_len, tiles_per_batch) * acc


def _out_proj_call(a2d, b2d, c2d, w_out, l, x2d, mods_l, gate_blk, tm, tn, ctx_len, tiles_per_batch):
    r, d = x2d.shape
    n_j = d // tn
    return pl.pallas_call(
        functools.partial(_out_proj_kernel, ctx_len=ctx_len, tiles_per_batch=tiles_per_batch),
        grid=(r // tm, n_j),
        in_specs=[pl.BlockSpec((tm, DIFF_WIDTH), lambda i, j: (i, 0)),
                  pl.BlockSpec((tm, RET_WIDTH), lambda i, j: (i, 0)),
                  pl.BlockSpec((tm, MLA_WIDTH), lambda i, j: (i, 0)),
                  pl.BlockSpec((None, d, tn), lambda i, j: (l, 0, j)),
                  pl.BlockSpec((tm, tn), lambda i, j: (i, j)),
                  pl.BlockSpec((MOD_ROWS, tn), lambda i, j: (0, gate_blk * n_j + j))],
        out_specs=pl.BlockSpec((tm, tn), lambda i, j: (i, j)),
        out_shape=jax.ShapeDtypeStruct((r, d), F32),
        compiler_params=_cparams(("parallel", "arbitrary")),
        name="out_proj",
    )(a2d, b2d, c2d, w_out, x2d, mods_l)


def _moe_up_kernel(h_ref, w_ref, b_ref, gate_ref, o_ref):
    j = pl.program_id(1)
    n_e = w_ref.shape[0]
    e_row = lax.broadcasted_iota(jnp.int32, (LANES, n_e * EXPERT_FF), 0)
    e_col = lax.broadcasted_iota(jnp.int32, (LANES, n_e * EXPERT_FF), 1) // EXPERT_FF
    expand = (e_row == j * n_e + e_col).astype(BF16)
    ge = jnp.dot(gate_ref[...].astype(BF16), expand, preferred_element_type=F32)
    h = h_ref[...]
    for e in range(n_e):
        lo = e * 2 * EXPERT_FF
        gu = jnp.dot(h, w_ref[e], preferred_element_type=F32) + b_ref[:, lo:lo + 2 * EXPERT_FF]
        glu = jnp.minimum(gu[:, :EXPERT_FF], SWIGLU_LIMIT)
        lin = jnp.clip(gu[:, EXPERT_FF:], -SWIGLU_LIMIT, SWIGLU_LIMIT)
        act = glu * jax.nn.sigmoid(SWIGLU_ALPHA * glu) * (lin + 1.0)
        sl = slice(e * EXPERT_FF, (e + 1) * EXPERT_FF)
        o_ref[:, sl] = (act * ge[:, sl]).astype(BF16)


def _moe_up_call(h2d, w_gu, b_gu, l, gates2d, tm):
    r, d = h2d.shape
    tn = SEG
    n_e = tn // (2 * EXPERT_FF)
    n = N_EXPERTS * 2 * EXPERT_FF
    return pl.pallas_call(
        _moe_up_kernel,
        grid=(r // tm, n // tn),
        in_specs=[pl.BlockSpec((tm, d), lambda i, j: (i, 0)),
                  pl.BlockSpec((None, n_e, d, 2 * EXPERT_FF), lambda i, j: (l, j, 0, 0)),
                  pl.BlockSpec((None, 1, tn), lambda i, j: (l, 0, j)),
                  pl.BlockSpec((tm, LANES), lambda i, j: (i, 0))],
        out_specs=pl.BlockSpec((tm, tn // 2), lambda i, j: (i, j)),
        out_shape=jax.ShapeDtypeStruct((r, n // 2), BF16),
        compiler_params=_cparams(("parallel", "arbitrary")),
        name="moe_up",
    )(h2d, w_gu, b_gu, gates2d)


def _moe_down_kernel(a_ref, w_ref, gates_ref, bd_ref, x_ref, gate_ref, o_ref, *, ctx_len, tiles_per_batch):
    acc = jnp.dot(a_ref[...], w_ref[...], preferred_element_type=F32)
    acc += jnp.dot(gates_ref[...].astype(BF16), bd_ref[...], preferred_element_type=F32)
    o_ref[...] = x_ref[...] + _row_gate(gate_ref, acc.shape[0], ctx_len, tiles_per_batch) * acc


def _moe_down_call(a2d, w_down, gates2d, b_down, l, x2d, mods_l, gate_blk, tm, tn, ctx_len, tiles_per_batch):
    r, d = x2d.shape
    k = a2d.shape[1]
    n_j = d // tn
    return pl.pallas_call(
        functools.partial(_moe_down_kernel, ctx_len=ctx_len, tiles_per_batch=tiles_per_batch),
        grid=(r // tm, n_j),
        in_specs=[pl.BlockSpec((tm, k), lambda i, j: (i, 0)),
                  pl.BlockSpec((None, k, tn), lambda i, j: (l, 0, j)),
                  pl.BlockSpec((tm, LANES), lambda i, j: (i, 0)),
                  pl.BlockSpec((None, LANES, tn), lambda i, j: (l, 0, j)),
                  pl.BlockSpec((tm, tn), lambda i, j: (i, j)),
                  pl.BlockSpec((MOD_ROWS, tn), lambda i, j: (0, gate_blk * n_j + j))],
        out_specs=pl.BlockSpec((tm, tn), lambda i, j: (i, j)),
        out_shape=jax.ShapeDtypeStruct((r, d), F32),
        compiler_params=_cparams(("parallel", "arbitrary")),
        name="moe_down",
    )(a2d, w_down, gates2d, b_down, x2d, mods_l)


def _final_norm_kernel(x_ref, g_ref, o_ref):
    o_ref[...] = _rms_gain(x_ref[...], g_ref)


def _final_norm_call(xs, gain, ctx_len, tr):
    b, s, d = xs.shape
    n = s - ctx_len
    off = ctx_len // tr
    return pl.pallas_call(
        _final_norm_kernel,
        grid=(b, n // tr),
        in_specs=[pl.BlockSpec((None, tr, d), lambda bi, i: (bi, i + off, 0)),
                  pl.BlockSpec((1, d), lambda bi, i: (0, 0))],
        out_specs=pl.BlockSpec((None, tr, d), lambda bi, i: (bi, i, 0)),
        out_shape=jax.ShapeDtypeStruct((b, n, d), F32),
        compiler_params=_cparams(("parallel", "parallel")),
        name="final_norm",
    )(xs, gain)


def _rope_tables(n, ctx_len):
    rows = n // GRID_W
    row = jnp.repeat(jnp.arange(rows, dtype=F32), GRID_W)
    col = jnp.tile(jnp.arange(GRID_W, dtype=F32), rows)

    def cos_sin(rot_dim):
        pairs = rot_dim // 4
        inv = ROPE_BASE ** (-jnp.arange(pairs, dtype=F32) / pairs)
        ang = jnp.concatenate([row[:, None] * inv, col[:, None] * inv], axis=-1)
        cos = jnp.concatenate([jnp.ones((ctx_len, rot_dim // 2), F32), jnp.cos(ang)], axis=0)
        sin = jnp.concatenate([jnp.zeros((ctx_len, rot_dim // 2), F32), jnp.sin(ang)], axis=0)
        return cos, sin

    cos32, sin32 = cos_sin(DIFF_HALF)
    cos64, sin64 = cos_sin(RET_DK)
    z32 = jnp.zeros_like(sin32)
    one64 = jnp.ones((cos32.shape[0], 64), F32)
    z64 = jnp.zeros_like(one64)
    return {
        "c64": jnp.concatenate([cos32] * 4, axis=-1),
        "sa64": jnp.concatenate([-sin32, z32, -sin32, z32], axis=-1),
        "sb64": jnp.concatenate([z32, sin32, z32, sin32], axis=-1),
        "c64r": jnp.concatenate([cos32, cos32, one64], axis=-1),
        "sa64r": jnp.concatenate([-sin32, z32, z64], axis=-1),
        "sb64r": jnp.concatenate([z32, sin32, z64], axis=-1),
        "c128": jnp.concatenate([cos64, cos64], axis=-1),
        "s128": jnp.concatenate([-sin64, sin64], axis=-1),
    }


def _prep_weights(w_in, mla_wq_b, mla_wkv_b, w_out, router_w, router_b,
                  exp_w_gu, exp_b_gu, exp_w_down, exp_b_down):
    depth, d, in_width = w_in.shape
    w_tail = jnp.pad(w_in[:, :, MAIN_WIDTH:], ((0, 0), (0, 0), (0, TAIL_WIDTH - (in_width - MAIN_WIDTH)))).astype(BF16)
    wq = mla_wq_b.reshape(depth, MLA_Q_RANK, MLA_HEADS, MLA_NOPE + MLA_ROPE)
    wq = jnp.pad(wq, ((0, 0), (0, 0), (0, 0), (0, MLA_QK_PAD - MLA_NOPE - MLA_ROPE)))
    wq = wq.reshape(depth, MLA_Q_RANK, MLA_HEADS * MLA_QK_PAD).astype(BF16)
    wkv = mla_wkv_b.reshape(depth, MLA_KV_RANK, MLA_HEADS, MLA_NOPE + MLA_V)
    wkv = jnp.concatenate([wkv[..., :MLA_NOPE].reshape(depth, MLA_KV_RANK, -1),
                           wkv[..., MLA_NOPE:].reshape(depth, MLA_KV_RANK, -1)], axis=-1).astype(BF16)
    rw = jnp.pad(router_w, ((0, 0), (0, 0), (0, LANES - N_EXPERTS))).astype(BF16)
    rb = jnp.concatenate([router_b, jnp.full((depth, LANES - N_EXPERTS), NEG_BIG, F32)], axis=-1)
    return dict(
        w_in=w_in.astype(BF16), w_tail=w_tail, wq=wq, wkv=wkv, w_out=w_out.astype(BF16),
        rw=rw, rb=rb.reshape(depth, 1, LANES),
        w_gu=exp_w_gu.astype(BF16),
        b_gu=exp_b_gu.reshape(depth, 1, N_EXPERTS * 2 * EXPERT_FF),
        w_down=exp_w_down.astype(BF16).reshape(depth, N_EXPERTS * EXPERT_FF, d),
        b_down=jnp.pad(exp_b_down, ((0, 0), (0, LANES - N_EXPERTS), (0, 0))).astype(BF16))


def kernel(x, c, ctx, c_ctx, ada_w, ada_b, norm1, norm2, w_in, diff_lq1, diff_lk1, diff_lq2, diff_lk2, diff_subln, ret_decay, ret_norm, mla_q_norm, mla_kv_norm, mla_wq_b, mla_wkv_b, w_out, router_w, router_b, exp_w_gu, exp_b_gu, exp_w_down, exp_b_down, final_norm):
    batch, n, d = x.shape
    ctx_len = ctx.shape[1]
    depth = ada_w.shape[0]
    s = ctx_len + n
    r = batch * s
    assert d == D_MODEL and batch <= CTX_MOD_ROW and ctx_len % RET_CHUNK == 0 and n % RET_CHUNK == 0

    tq = _pick_tile(ctx_len, (256, 128))
    n_ctx_tiles = ctx_len // tq
    tm = _pick_tile(s, (768, 576, 384, 256, 128), at_least=ctx_len)
    tm_mla = _pick_tile(s, (384, 256, 128))
    tiles_per_batch = s // tm
    tn = 1024
    ret_chunk = _pick_tile(math.gcd(ctx_len, n), (2 * RET_CHUNK, RET_CHUNK))

    xs = jnp.concatenate([ctx, x], axis=1)
    mods = _ada_tables(_stack_conditioning(c, c_ctx), ada_w, ada_b)
    tabs = _rope_tables(n, ctx_len)
    log_gamma = jnp.log1p(-jnp.exp2(-ret_decay.astype(F32)))

    wts = _prep_weights(w_in, mla_wq_b, mla_wkv_b, w_out, router_w, router_b,
                        exp_w_gu, exp_b_gu, exp_w_down, exp_b_down)
    q_norm = mla_q_norm.reshape(depth, 1, -1)
    kv_norm = mla_kv_norm.reshape(depth, 1, -1)

    for l in range(depth):
        lam_init = 0.8 - 0.6 * math.exp(-0.3 * l)
        mods_l = mods[l]
        h = _norm_call(xs, norm1[l].reshape(1, d), mods_l, 0, 1, n_ctx_tiles, tq)
        h2d = h.reshape(r, d)
        p2d = _proj_call(h2d, wts["w_in"], l, tabs, tm, tiles_per_batch)
        p3d = p2d.reshape(batch, s, MAIN_WIDTH)
        qc = _mla_q_call(p2d, q_norm, wts["wq"], l, tabs, tm_mla, s // tm_mla)
        kc, vc = _mla_kv_call(h2d, wts["w_tail"], kv_norm, wts["wkv"], l, tabs, tm_mla, s // tm_mla)
        lamv = jnp.stack([diff_lq1[l], diff_lk1[l], diff_lq2[l], diff_lk2[l]]).astype(F32)
        a = _diff_attn_call(p3d, lamv, diff_subln[l].reshape(1, -1), lam_init, tq, n_ctx_tiles, ctx_len)
        bb = _ret_call(p3d, log_gamma[l], ret_norm[l].reshape(1, -1), ret_chunk, ctx_len // ret_chunk)
        cc = _mla_attn_call(qc.reshape(batch, s, -1), kc.reshape(batch, s, -1), vc.reshape(batch, s, -1),
                            tq, n_ctx_tiles, ctx_len)
        x2d = _out_proj_call(a.reshape(r, -1), bb.reshape(r, -1), cc.reshape(r, -1), wts["w_out"], l,
                             xs.reshape(r, d), mods_l, 2, tm, tn, ctx_len, tiles_per_batch)
        xs = x2d.reshape(batch, s, d)
        h2, gates = _norm_call(xs, norm2[l].reshape(1, d), mods_l, 3, 4, n_ctx_tiles, tq,
                               router=(wts["rw"], wts["rb"], l))
        gates2d = gates.reshape(r, LANES)
        act = _moe_up_call(h2.reshape(r, d), wts["w_gu"], wts["b_gu"], l, gates2d, tm)
        x2d = _moe_down_call(act, wts["w_down"], gates2d, wts["b_down"], l, x2d, mods_l, 5, tm, tn,
                             ctx_len, tiles_per_batch)
        xs = x2d.reshape(batch, s, d)

    return _final_norm_call(xs, final_norm.reshape(1, d), ctx_len, tq)
```

```python
import functools
import math

import jax
import jax.numpy as jnp
from jax import lax
from jax.experimental import pallas as pl
from jax.experimental.pallas import tpu as pltpu

F32 = jnp.float32
BF16 = jnp.bfloat16

D_MODEL = 4096
GRID_W = 64
RET_CHUNK = 128
ROPE_BASE = 10000.0
EPS = 1e-6
HEAD_DIM = 128
DIFF_HEADS = 8
DIFF_HALF = 64
DIFF_WIDTH = 1024
RET_HEADS = 8
RET_DK = 128
RET_WIDTH = 1024
MLA_HEADS = 16
MLA_Q_RANK = 1024
MLA_KV_RANK = 512
MLA_NOPE = 128
MLA_ROPE = 64
MLA_V = 128
MLA_WIDTH = 2048
MLA_QK_PAD = 256
N_EXPERTS = 32
TOP_K = 4
EXPERT_FF = 128
SWIGLU_LIMIT = 7.0
SWIGLU_ALPHA = 1.702
MAIN_WIDTH = 8192
TAIL_WIDTH = 640
SEG = 1024

LANES = 128
VMEM_LIMIT_BYTES = 56 * 1024 * 1024
NEG_BIG = -1e30
CTX_MOD_ROW = 8
MOD_ROWS = 16


def _cparams(sem):
    return pltpu.CompilerParams(dimension_semantics=sem, vmem_limit_bytes=VMEM_LIMIT_BYTES)


def _pick_tile(total, candidates, at_least=1):
    for c in candidates:
        if total % c == 0 and c >= at_least:
            return c
    raise ValueError(f"no tile for {total}")


def _ada_kernel(c_ref, w_ref, b_ref, o_ref):
    c = c_ref[...]
    cs = (c * jax.nn.sigmoid(c)).astype(BF16)
    o_ref[...] = jnp.dot(cs, w_ref[...].astype(BF16), preferred_element_type=F32) + b_ref[...]


def _stack_conditioning(c, c_ctx):
    batch, d = c.shape
    return jnp.concatenate([c, jnp.zeros((CTX_MOD_ROW - batch, d), F32), c_ctx[None, :],
                            jnp.zeros((MOD_ROWS - CTX_MOD_ROW - 1, d), F32)], axis=0)


def _ada_tables(c_all, ada_w, ada_b):
    depth, d, n6 = ada_w.shape
    tn = 512
    return pl.pallas_call(
        _ada_kernel,
        grid=(depth, n6 // tn),
        in_specs=[
            pl.BlockSpec((MOD_ROWS, d), lambda l, j: (0, 0)),
            pl.BlockSpec((None, d, tn), lambda l, j: (l, 0, j)),
            pl.BlockSpec((None, 1, tn), lambda l, j: (l, 0, j)),
        ],
        out_specs=pl.BlockSpec((None, MOD_ROWS, tn), lambda l, j: (l, 0, j)),
        out_shape=jax.ShapeDtypeStruct((depth, MOD_ROWS, n6), F32),
        compiler_params=_cparams(("arbitrary", "arbitrary")),
        name="ada_tables",
    )(c_all, ada_w, ada_b.reshape(depth, 1, n6))


def _mod_norm(x, gain, shift, scale):
    ms = jnp.mean(x * x, axis=-1, keepdims=True)
    return x * lax.rsqrt(ms + EPS) * gain * (1.0 + scale) + shift


def _norm_kernel(x_ref, g_ref, sh_ref, sc_ref, o_ref, *, n_ctx_tiles):
    row = jnp.where(pl.program_id(1) < n_ctx_tiles, CTX_MOD_ROW, pl.program_id(0))
    o_ref[...] = _mod_norm(x_ref[...], g_ref[...], sh_ref[pl.ds(row, 1), :], sc_ref[pl.ds(row, 1), :]).astype(BF16)


def _router_gates(h, rw_ref, rb_ref):
    logits = jnp.dot(h, rw_ref[...], preferred_element_type=F32) + rb_ref[...]
    lane = lax.broadcasted_iota(jnp.int32, logits.shape, 1)
    work = logits
    num = jnp.zeros_like(logits)
    den = jnp.zeros((logits.shape[0], 1), F32)
    m0 = None
    for k in range(TOP_K):
        m = jnp.max(work, axis=-1, keepdims=True)
        idx = jnp.min(jnp.where(work == m, lane, LANES), axis=-1, keepdims=True)
        pick = lane == idx
        if k == 0:
            m0 = m
        e = jnp.exp(m - m0)
        num = num + jnp.where(pick, e, 0.0)
        den = den + e
        work = jnp.where(pick, -jnp.inf, work)
    return num / den


def _norm_call(xs, gain, mods_l, shift_blk, scale_blk, n_ctx_tiles, tr):
    b, s, d = xs.shape
    x_spec = pl.BlockSpec((None, tr, d), lambda bi, i: (bi, i, 0))
    return pl.pallas_call(
        functools.partial(_norm_kernel, n_ctx_tiles=n_ctx_tiles),
        grid=(b, s // tr),
        in_specs=[x_spec,
                  pl.BlockSpec((1, d), lambda bi, i: (0, 0)),
                  pl.BlockSpec((MOD_ROWS, d), lambda bi, i: (0, shift_blk)),
                  pl.BlockSpec((MOD_ROWS, d), lambda bi, i: (0, scale_blk))],
        out_specs=x_spec,
        out_shape=jax.ShapeDtypeStruct((b, s, d), BF16),
        compiler_params=_cparams(("parallel", "parallel")),
        name="mod_norm",
    )(xs, gain, mods_l, mods_l)


def _rope64(blk, c, sa, sb):
    return blk * c + pltpu.roll(blk, 96, 1) * sa + pltpu.roll(blk, 32, 1) * sb


def _proj_kernel(h_ref, w_ref, c64_ref, sa64_ref, sb64_ref, c128_ref, s128_ref, o_ref):
    j = pl.program_id(1)
    acc = jnp.dot(h_ref[...], w_ref[...], preferred_element_type=F32)
    n_blk = acc.shape[1] // LANES

    @pl.when((j == 0) | (j == 1))
    def _():
        scale = jnp.where(j == 0, DIFF_HALF ** -0.5, 1.0).astype(F32)
        c, sa, sb = c64_ref[...] * scale, sa64_ref[...] * scale, sb64_ref[...] * scale
        for t in range(n_blk):
            sl = slice(t * LANES, (t + 1) * LANES)
            o_ref[:, sl] = _rope64(acc[:, sl], c, sa, sb).astype(BF16)

    @pl.when((j == 3) | (j == 4))
    def _():
        scale = jnp.where(j == 4, RET_DK ** -0.5, 1.0).astype(F32)
        c, s = c128_ref[...] * scale, s128_ref[...] * scale
        for t in range(n_blk):
            sl = slice(t * LANES, (t + 1) * LANES)
            blk = acc[:, sl]
            o_ref[:, sl] = (blk * c + pltpu.roll(blk, 64, 1) * s).astype(BF16)

    @pl.when((j == 2) | (j >= 5))
    def _():
        o_ref[...] = acc.astype(BF16)


def _proj_call(h2d, w_in_bf, l, tabs, tm, tiles_per_batch):
    r, d = h2d.shape
    tab_spec = pl.BlockSpec((tm, LANES), lambda i, j: (i % tiles_per_batch, 0))
    return pl.pallas_call(
        _proj_kernel,
        grid=(r // tm, MAIN_WIDTH // SEG),
        in_specs=[pl.BlockSpec((tm, d), lambda i, j: (i, 0)),
                  pl.BlockSpec((None, d, SEG), lambda i, j: (l, 0, j)),
                  tab_spec, tab_spec, tab_spec, tab_spec, tab_spec],
        out_specs=pl.BlockSpec((tm, SEG), lambda i, j: (i, j)),
        out_shape=jax.ShapeDtypeStruct((r, MAIN_WIDTH), BF16),
        compiler_params=_cparams(("parallel", "arbitrary")),
        name="in_proj",
    )(h2d, w_in_bf, tabs["c64"], tabs["sa64"], tabs["sb64"], tabs["c128"], tabs["s128"])


def _rms_gain(x, g_ref):
    ms = jnp.mean(x * x, axis=-1, keepdims=True)
    return x * lax.rsqrt(ms + EPS) * g_ref[...]


def _mla_q_kernel(qa_ref, g_ref, w_ref, c_ref, sa_ref, sb_ref, o_ref):
    qn = _rms_gain(qa_ref[...].astype(F32), g_ref).astype(BF16)
    q = jnp.dot(qn, w_ref[...], preferred_element_type=F32)
    scale = (MLA_NOPE + MLA_ROPE) ** -0.5
    c, sa, sb = c_ref[...] * scale, sa_ref[...] * scale, sb_ref[...] * scale
    for h in range(MLA_HEADS):
        lo = h * MLA_QK_PAD
        o_ref[:, lo:lo + LANES] = (q[:, lo:lo + LANES] * scale).astype(BF16)
        o_ref[:, lo + LANES:lo + 2 * LANES] = _rope64(q[:, lo + LANES:lo + 2 * LANES], c, sa, sb).astype(BF16)


def _mla_kv_kernel(h_ref, wt_ref, g_ref, w_ref, c_ref, sa_ref, sb_ref, k_ref, v_ref):
    t = jnp.dot(h_ref[...], wt_ref[...], preferred_element_type=F32)
    kvn = _rms_gain(t[:, :MLA_KV_RANK], g_ref).astype(BF16)
    kr = _rope64(t[:, MLA_KV_RANK:MLA_KV_RANK + LANES], c_ref[...], sa_ref[...], sb_ref[...]).astype(BF16)
    kv = jnp.dot(kvn, w_ref[...], preferred_element_type=F32)
    kn_width = MLA_HEADS * MLA_NOPE
    for h in range(MLA_HEADS):
        lo = h * MLA_QK_PAD
        k_ref[:, lo:lo + LANES] = kv[:, h * MLA_NOPE:(h + 1) * MLA_NOPE].astype(BF16)
        k_ref[:, lo + LANES:lo + 2 * LANES] = kr
    v_ref[...] = kv[:, kn_width:].astype(BF16)


def _mla_q_call(p2d, q_norm, wq, l, tabs, tm, tiles_per_batch):
    r = p2d.shape[0]
    tab_spec = pl.BlockSpec((tm, LANES), lambda i: (i % tiles_per_batch, 0))
    width = MLA_HEADS * MLA_QK_PAD
    return pl.pallas_call(
        _mla_q_kernel,
        grid=(r // tm,),
        in_specs=[pl.BlockSpec((tm, MLA_Q_RANK), lambda i: (i, (MAIN_WIDTH - MLA_Q_RANK) // MLA_Q_RANK)),
                  pl.BlockSpec((None, 1, MLA_Q_RANK), lambda i: (l, 0, 0)),
                  pl.BlockSpec((None, MLA_Q_RANK, width), lambda i: (l, 0, 0)),
                  tab_spec, tab_spec, tab_spec],
        out_specs=pl.BlockSpec((tm, width), lambda i: (i, 0)),
        out_shape=jax.ShapeDtypeStruct((r, width), BF16),
        compiler_params=_cparams(("parallel",)),
        name="mla_q_proj",
    )(p2d, q_norm, wq, tabs["c64r"], tabs["sa64r"], tabs["sb64r"])


def _mla_kv_call(h2d, w_tail, kv_norm, wkv, l, tabs, tm, tiles_per_batch):
    r, d = h2d.shape
    tab_spec = pl.BlockSpec((tm, LANES), lambda i: (i % tiles_per_batch, 0))
    kwidth = MLA_HEADS * MLA_QK_PAD
    return pl.pallas_call(
        _mla_kv_kernel,
        grid=(r // tm,),
        in_specs=[pl.BlockSpec((tm, d), lambda i: (i, 0)),
                  pl.BlockSpec((None, d, TAIL_WIDTH), lambda i: (l, 0, 0)),
                  pl.BlockSpec((None, 1, MLA_KV_RANK), lambda i: (l, 0, 0)),
                  pl.BlockSpec((None, MLA_KV_RANK, 2 * MLA_WIDTH), lambda i: (l, 0, 0)),
                  tab_spec, tab_spec, tab_spec],
        out_specs=[pl.BlockSpec((tm, kwidth), lambda i: (i, 0)),
                   pl.BlockSpec((tm, MLA_WIDTH), lambda i: (i, 0))],
        out_shape=[jax.ShapeDtypeStruct((r, kwidth), BF16),
                   jax.ShapeDtypeStruct((r, MLA_WIDTH), BF16)],
        compiler_params=_cparams(("parallel",)),
        name="mla_kv_proj",
    )(h2d, w_tail, kv_norm, wkv, tabs["c64r"], tabs["sa64r"], tabs["sb64r"])


KEY_CHUNK = 768


def _softmax_pv(q, k_ref, vo_ref, m_keys, k_cols, v_cols):
    chunk = _pick_tile(m_keys, (KEY_CHUNK, 512, 384, 256, 128))
    m_run = None
    acc = None
    for c0 in range(0, m_keys, chunk):
        s = lax.dot_general(q, k_ref[c0:c0 + chunk, k_cols], _NT, preferred_element_type=F32)
        m_c = jnp.max(s, axis=-1, keepdims=True)
        v_ones = vo_ref[c0:c0 + chunk, v_cols]
        if m_run is None:
            m_run = m_c
            acc = jnp.dot(jnp.exp((s - m_run).astype(BF16)), v_ones, preferred_element_type=F32)
        else:
            m_new = jnp.maximum(m_run, m_c)
            acc = (acc * jnp.exp(m_run - m_new)
                   + jnp.dot(jnp.exp((s - m_new).astype(BF16)), v_ones, preferred_element_type=F32))
            m_run = m_new
    half = acc.shape[1] // 2
    return acc[:, :half] / acc[:, half:]


def _fill_v_ones(v_ref, vo_ref, n_heads, width):
    ones = jnp.ones((v_ref.shape[0], width), BF16)
    for h in range(n_heads):
        vo_ref[:, 2 * width * h:2 * width * h + width] = v_ref[:, width * h:width * (h + 1)]
        vo_ref[:, 2 * width * h + width:2 * width * (h + 1)] = ones


_NT = (((1,), (1,)), ((), ()))


def _diff_attn_kernel(lam_ref, sub_ref, q_ref, k_ref, v_ref, o_ref, vo_ref, *, n_ctx_tiles, ctx_len, lam_init):
    @pl.when(pl.program_id(1) == 0)
    def _():
        _fill_v_ones(v_ref, vo_ref, DIFF_HEADS, HEAD_DIM)

    lv = lam_ref[...]
    lam = (jnp.exp(jnp.sum(lv[0:1] * lv[1:2], axis=-1, keepdims=True))
           - jnp.exp(jnp.sum(lv[2:3] * lv[3:4], axis=-1, keepdims=True)) + lam_init)
    sub = sub_ref[...] * (1.0 - lam_init)

    def run(m_keys):
        for h in range(DIFF_HEADS):
            sl = slice(h * HEAD_DIM, (h + 1) * HEAD_DIM)
            q = q_ref[:, sl]
            v_cols = slice(2 * h * HEAD_DIM, 2 * (h + 1) * HEAD_DIM)
            first = lax.broadcasted_iota(jnp.int32, q.shape, 1) < DIFF_HALF
            zero = jnp.zeros_like(q)
            o = (_softmax_pv(jnp.where(first, q, zero), k_ref, vo_ref, m_keys, sl, v_cols)
                 - lam * _softmax_pv(jnp.where(first, zero, q), k_ref, vo_ref, m_keys, sl, v_cols))
            ms = jnp.mean(o * o, axis=-1, keepdims=True)
            o_ref[:, sl] = (o * lax.rsqrt(ms + EPS) * sub).astype(BF16)

    is_ctx = pl.program_id(1) < n_ctx_tiles

    @pl.when(is_ctx)
    def _():
        run(ctx_len)

    @pl.when(jnp.logical_not(is_ctx))
    def _():
        run(k_ref.shape[0])


def _diff_attn_call(p3d, lamv, subln, lam_init, tq, n_ctx_tiles, ctx_len):
    b, s, _ = p3d.shape
    return pl.pallas_call(
        functools.partial(_diff_attn_kernel, n_ctx_tiles=n_ctx_tiles, ctx_len=ctx_len, lam_init=lam_init),
        grid=(b, s // tq),
        in_specs=[pl.BlockSpec((4, DIFF_HALF), lambda bi, i: (0, 0)),
                  pl.BlockSpec((1, HEAD_DIM), lambda bi, i: (0, 0)),
                  pl.BlockSpec((None, tq, DIFF_WIDTH), lambda bi, i: (bi, i, 0)),
                  pl.BlockSpec((None, s, DIFF_WIDTH), lambda bi, i: (bi, 0, 1)),
                  pl.BlockSpec((None, s, DIFF_WIDTH), lambda bi, i: (bi, 0, 2))],
        out_specs=pl.BlockSpec((None, tq, DIFF_WIDTH), lambda bi, i: (bi, i, 0)),
        out_shape=jax.ShapeDtypeStruct((b, s, DIFF_WIDTH), BF16),
        scratch_shapes=[pltpu.VMEM((s, 2 * DIFF_WIDTH), BF16)],
        compiler_params=_cparams(("parallel", "arbitrary")),
        name="diff_attn",
    )(lamv, subln, p3d, p3d, p3d)


MLA_HEAD_GROUP = 8


def _mla_attn_kernel(q_ref, k_ref, v_ref, o_ref, vo_ref, *, n_ctx_tiles, ctx_len):
    @pl.when(pl.program_id(2) == 0)
    def _():
        _fill_v_ones(v_ref, vo_ref, MLA_HEAD_GROUP, MLA_V)

    def run(m_keys):
        for h in range(MLA_HEAD_GROUP):
            qk = slice(h * MLA_QK_PAD, (h + 1) * MLA_QK_PAD)
            vv = slice(h * MLA_V, (h + 1) * MLA_V)
            v_cols = slice(2 * h * MLA_V, 2 * (h + 1) * MLA_V)
            o_ref[:, vv] = _softmax_pv(q_ref[:, qk], k_ref, vo_ref, m_keys, qk, v_cols).astype(BF16)

    is_ctx = pl.program_id(2) < n_ctx_tiles

    @pl.when(is_ctx)
    def _():
        run(ctx_len)

    @pl.when(jnp.logical_not(is_ctx))
    def _():
        run(k_ref.shape[0])


def _mla_attn_call(qc, kc, vc, tq, n_ctx_tiles, ctx_len):
    b, s, _ = qc.shape
    qw = MLA_HEAD_GROUP * MLA_QK_PAD
    vw = MLA_HEAD_GROUP * MLA_V
    return pl.pallas_call(
        functools.partial(_mla_attn_kernel, n_ctx_tiles=n_ctx_tiles, ctx_len=ctx_len),
        grid=(b, MLA_HEADS // MLA_HEAD_GROUP, s // tq),
        in_specs=[pl.BlockSpec((None, tq, qw), lambda bi, g, i: (bi, i, g)),
                  pl.BlockSpec((None, s, qw), lambda bi, g, i: (bi, 0, g)),
                  pl.BlockSpec((None, s, vw), lambda bi, g, i: (bi, 0, g))],
        out_specs=pl.BlockSpec((None, tq, vw), lambda bi, g, i: (bi, i, g)),
        out_shape=jax.ShapeDtypeStruct((b, s, MLA_WIDTH), BF16),
        scratch_shapes=[pltpu.VMEM((s, 2 * vw), BF16)],
        compiler_params=_cparams(("parallel", "parallel", "arbitrary")),
        name="mla_attn",
    )(qc, kc, vc)


RET_HEAD_GROUP = 4
_TN = (((0,), (0,)), ((), ()))


def _ret_kernel(lg_ref, q_ref, k_ref, v_ref, g_ref, gn_ref, o_ref, intra_ref, rowdec_ref, cdec_ref, st_ref, acc_ref,
                *, chunk, n_ctx_chunks):
    c = chunk
    n_chunks = q_ref.shape[0] // c
    hp = pl.program_id(1)
    row = lax.broadcasted_iota(jnp.int32, (c, c), 0).astype(F32)
    col = lax.broadcasted_iota(jnp.int32, (c, c), 1).astype(F32)
    rel = row - col
    pos = lax.broadcasted_iota(jnp.int32, (c, RET_DK), 0).astype(F32)
    span = jnp.full((RET_DK, RET_DK), float(c), F32)

    for hh in range(RET_HEAD_GROUP):
        lgf = lg_ref[0, hp * RET_HEAD_GROUP + hh]
        lgb = lg_ref[1, hp * RET_HEAD_GROUP + hh]
        intra_ref[hh, 0] = jnp.where(rel >= 0, jnp.exp(jnp.maximum(rel, 0.0) * lgf), 0.0)
        intra_ref[hh, 1] = jnp.where(rel <= 0, jnp.exp(jnp.maximum(-rel, 0.0) * lgb), 0.0)
        rowdec_ref[hh, 0] = jnp.exp((pos + 1.0) * lgf)
        rowdec_ref[hh, 1] = jnp.exp((c - 1.0 - pos) * lgf)
        rowdec_ref[hh, 2] = jnp.exp((c - pos) * lgb)
        rowdec_ref[hh, 3] = jnp.exp(pos * lgb)
        cdec_ref[hh, 0] = jnp.exp(span * lgf)
        cdec_ref[hh, 1] = jnp.exp(span * lgb)

    st_ref[...] = jnp.zeros_like(st_ref)
    acc_ref[...] = jnp.zeros_like(acc_ref)

    def chunk_update(hh, direction, idx):
        sl = slice(hh * RET_DK, (hh + 1) * RET_DK)
        rows = pl.ds(pl.multiple_of(idx * c, c), c)
        qc = q_ref[rows, sl]
        kc = k_ref[rows, sl]
        vc = v_ref[rows, sl]
        state = st_ref[hh, direction]
        att = lax.dot_general(qc, kc, _NT, preferred_element_type=F32) * intra_ref[hh, direction]
        o = (jnp.dot(att.astype(BF16), vc, preferred_element_type=F32)
             + jnp.dot(qc, state.astype(BF16), preferred_element_type=F32) * rowdec_ref[hh, 2 * direction])
        kd = (kc.astype(F32) * rowdec_ref[hh, 2 * direction + 1]).astype(BF16)
        st_ref[hh, direction] = (state * cdec_ref[hh, direction]
                                 + lax.dot_general(kd, vc, _TN, preferred_element_type=F32))
        acc_ref[rows, sl] += o

    def step(t, carry):
        bwd = jnp.where(t < n_ctx_chunks, n_ctx_chunks - 1 - t, n_chunks - 1 - t + n_ctx_chunks)
        for hh in range(RET_HEAD_GROUP):
            chunk_update(hh, 0, t)
            chunk_update(hh, 1, bwd)
        return carry

    lax.fori_loop(0, n_chunks, step, 0)

    for hh in range(RET_HEAD_GROUP):
        sl = slice(hh * RET_DK, (hh + 1) * RET_DK)
        o = acc_ref[:, sl]
        mu = jnp.mean(o, axis=-1, keepdims=True)
        oc = o - mu
        y = oc * lax.rsqrt(jnp.mean(oc * oc, axis=-1, keepdims=True) + 1e-5) * gn_ref[:, sl]
        g = g_ref[:, sl].astype(F32)
        o_ref[:, sl] = (y * (g * jax.nn.sigmoid(g))).astype(BF16)


def _ret_call(p3d, log_gamma, gn_gain, chunk, n_ctx_chunks):
    b, s, _ = p3d.shape
    w = RET_HEAD_GROUP * RET_DK
    base = 3 * SEG // w

    def col(seg):
        return lambda bi, hp, lg: (bi, 0, base + seg * (SEG // w) + hp)

    grid_spec = pltpu.PrefetchScalarGridSpec(
        num_scalar_prefetch=1,
        grid=(b, RET_HEADS // RET_HEAD_GROUP),
        in_specs=[pl.BlockSpec((None, s, w), col(0)),
                  pl.BlockSpec((None, s, w), col(1)),
                  pl.BlockSpec((None, s, w), col(2)),
                  pl.BlockSpec((None, s, w), col(3)),
                  pl.BlockSpec((1, w), lambda bi, hp, lg: (0, hp))],
        out_specs=pl.BlockSpec((None, s, w), lambda bi, hp, lg: (bi, 0, hp)),
        scratch_shapes=[pltpu.VMEM((RET_HEAD_GROUP, 2, chunk, chunk), F32),
                        pltpu.VMEM((RET_HEAD_GROUP, 4, chunk, RET_DK), F32),
                        pltpu.VMEM((RET_HEAD_GROUP, 2, RET_DK, RET_DK), F32),
                        pltpu.VMEM((RET_HEAD_GROUP, 2, RET_DK, RET_DK), F32),
                        pltpu.VMEM((s, w), F32)],
    )
    return pl.pallas_call(
        functools.partial(_ret_kernel, chunk=chunk, n_ctx_chunks=n_ctx_chunks),
        grid_spec=grid_spec,
        out_shape=jax.ShapeDtypeStruct((b, s, RET_WIDTH), BF16),
        compiler_params=_cparams(("parallel", "arbitrary")),
        name="retention",
    )(log_gamma, p3d, p3d, p3d, p3d, gn_gain)


ROW_TILE = 128


def _tile_mod_row(tiles_per_batch, n_ctx_tiles):
    i = pl.program_id(0)
    return jnp.where(i % tiles_per_batch < n_ctx_tiles, CTX_MOD_ROW, i // tiles_per_batch)


def _resident(shape, index_map):
    return pl.BlockSpec(shape, index_map, pipeline_mode=pl.Buffered(1))


def _out_proj_kernel(a_ref, b_ref, c_ref, w_ref, x_ref, g1_ref, gain_ref, sh_ref, sc_ref, xo_ref, h_ref,
                     *, tiles_per_batch, n_ctx_tiles):
    row = pl.ds(_tile_mod_row(tiles_per_batch, n_ctx_tiles), 1)
    acc = jnp.dot(a_ref[...], w_ref[0:DIFF_WIDTH, :], preferred_element_type=F32)
    acc += jnp.dot(b_ref[...], w_ref[DIFF_WIDTH:DIFF_WIDTH + RET_WIDTH, :], preferred_element_type=F32)
    acc += jnp.dot(c_ref[...], w_ref[DIFF_WIDTH + RET_WIDTH:, :], preferred_element_type=F32)
    x_new = x_ref[...] + g1_ref[row, :] * acc
    xo_ref[...] = x_new
    h_ref[...] = _mod_norm(x_new, gain_ref[...], sh_ref[row, :], sc_ref[row, :]).astype(BF16)


def _out_proj_call(a2d, b2d, c2d, w_out, l, x2d, mods_l, gain, tiles_per_batch, n_ctx_tiles):
    r, d = x2d.shape
    tr = ROW_TILE
    rows = lambda w: pl.BlockSpec((tr, w), lambda i: (i, 0))
    mod = lambda blk: pl.BlockSpec((MOD_ROWS, d), lambda i: (0, blk))
    return pl.pallas_call(
        functools.partial(_out_proj_kernel, tiles_per_batch=tiles_per_batch, n_ctx_tiles=n_ctx_tiles),
        grid=(r // tr,),
        in_specs=[rows(DIFF_WIDTH), rows(RET_WIDTH), rows(MLA_WIDTH),
                  _resident((None, d, d), lambda i: (l, 0, 0)),
                  rows(d), mod(2),
                  pl.BlockSpec((1, d), lambda i: (0, 0)), mod(3), mod(4)],
        out_specs=[rows(d), rows(d)],
        out_shape=[jax.ShapeDtypeStruct((r, d), F32), jax.ShapeDtypeStruct((r, d), BF16)],
        compiler_params=_cparams(("arbitrary",)),
        name="out_proj",
    )(a2d, b2d, c2d, w_out, x2d, mods_l, gain, mods_l, mods_l)


def _moe_up_kernel(h_ref, w_ref, b_ref, rw_ref, rb_ref, o_ref, gate_out_ref, gate_ref):
    j = pl.program_id(1)
    n_e = w_ref.shape[0]

    @pl.when(j == 0)
    def _():
        gates = _router_gates(h_ref[...], rw_ref, rb_ref)
        gate_ref[...] = gates
        gate_out_ref[...] = gates

    e_row = lax.broadcasted_iota(jnp.int32, (LANES, n_e * EXPERT_FF), 0)
    e_col = lax.broadcasted_iota(jnp.int32, (LANES, n_e * EXPERT_FF), 1) // EXPERT_FF
    expand = (e_row == j * n_e + e_col).astype(BF16)
    ge = jnp.dot(gate_ref[...].astype(BF16), expand, preferred_element_type=F32)
    h = h_ref[...]
    for e in range(n_e):
        lo = e * 2 * EXPERT_FF
        gu = jnp.dot(h, w_ref[e], preferred_element_type=F32) + b_ref[:, lo:lo + 2 * EXPERT_FF]
        glu = jnp.minimum(gu[:, :EXPERT_FF], SWIGLU_LIMIT)
        lin = jnp.clip(gu[:, EXPERT_FF:], -SWIGLU_LIMIT, SWIGLU_LIMIT)
        act = glu * jax.nn.sigmoid(SWIGLU_ALPHA * glu) * (lin + 1.0)
        sl = slice(e * EXPERT_FF, (e + 1) * EXPERT_FF)
        o_ref[:, sl] = (act * ge[:, sl]).astype(BF16)


def _moe_up_call(h2d, w_gu, b_gu, rw, rb, l, tm):
    r, d = h2d.shape
    tn = SEG
    n_e = tn // (2 * EXPERT_FF)
    n = N_EXPERTS * 2 * EXPERT_FF
    return pl.pallas_call(
        _moe_up_kernel,
        grid=(r // tm, n // tn),
        in_specs=[pl.BlockSpec((tm, d), lambda i, j: (i, 0)),
                  pl.BlockSpec((None, n_e, d, 2 * EXPERT_FF), lambda i, j: (l, j, 0, 0)),
                  pl.BlockSpec((None, 1, tn), lambda i, j: (l, 0, j)),
                  pl.BlockSpec((None, d, LANES), lambda i, j: (l, 0, 0)),
                  pl.BlockSpec((None, 1, LANES), lambda i, j: (l, 0, 0))],
        out_specs=[pl.BlockSpec((tm, tn // 2), lambda i, j: (i, j)),
                   pl.BlockSpec((tm, LANES), lambda i, j: (i, 0))],
        out_shape=[jax.ShapeDtypeStruct((r, n // 2), BF16), jax.ShapeDtypeStruct((r, LANES), F32)],
        scratch_shapes=[pltpu.VMEM((tm, LANES), F32)],
        compiler_params=_cparams(("parallel", "arbitrary")),
        name="moe_up",
    )(h2d, w_gu, b_gu, rw, rb)


def _moe_down_rows(a_ref, w_ref, gates_ref, bd_ref, x_ref, g2_ref, row):
    acc = jnp.dot(a_ref[...], w_ref[...], preferred_element_type=F32)
    acc += jnp.dot(gates_ref[...].astype(BF16), bd_ref[...], preferred_element_type=F32)
    return x_ref[...] + g2_ref[row, :] * acc


def _moe_down_kernel(a_ref, w_ref, gates_ref, bd_ref, x_ref, g2_ref, gain_ref, sh_ref, sc_ref, xo_ref, h_ref,
                     *, tiles_per_batch, n_ctx_tiles):
    row = pl.ds(_tile_mod_row(tiles_per_batch, n_ctx_tiles), 1)
    x_new = _moe_down_rows(a_ref, w_ref, gates_ref, bd_ref, x_ref, g2_ref, row)
    xo_ref[...] = x_new
    h_ref[...] = _mod_norm(x_new, gain_ref[...], sh_ref[row, :], sc_ref[row, :]).astype(BF16)


def _moe_down_final_kernel(a_ref, w_ref, gates_ref, bd_ref, x_ref, g2_ref, gain_ref, o_ref,
                           *, tiles_per_batch, n_ctx_tiles):
    row = pl.ds(_tile_mod_row(tiles_per_batch, n_ctx_tiles), 1)
    x_new = _moe_down_rows(a_ref, w_ref, gates_ref, bd_ref, x_ref, g2_ref, row)
    ms = jnp.mean(x_new * x_new, axis=-1, keepdims=True)
    o_ref[...] = x_new * lax.rsqrt(ms + EPS) * gain_ref[...]


def _moe_down_call(a2d, w_down, gates2d, b_down, l, x2d, mods_l, gain, mods_next, tiles_per_batch, n_ctx_tiles,
                   out_batch=None):
    r, d = x2d.shape
    k = a2d.shape[1]
    tr = ROW_TILE
    rows = lambda w: pl.BlockSpec((tr, w), lambda i: (i, 0))
    mod = lambda blk: pl.BlockSpec((MOD_ROWS, d), lambda i: (0, blk))
    in_specs = [rows(k), _resident((None, k, d), lambda i: (l, 0, 0)), rows(LANES),
                _resident((None, LANES, d), lambda i: (l, 0, 0)), rows(d), mod(5),
                pl.BlockSpec((1, d), lambda i: (0, 0))]
    static = dict(tiles_per_batch=tiles_per_batch, n_ctx_tiles=n_ctx_tiles)
    if out_batch is None:
        return pl.pallas_call(
            functools.partial(_moe_down_kernel, **static),
            grid=(r // tr,),
            in_specs=in_specs + [mod(0), mod(1)],
            out_specs=[rows(d), rows(d)],
            out_shape=[jax.ShapeDtypeStruct((r, d), F32), jax.ShapeDtypeStruct((r, d), BF16)],
            compiler_params=_cparams(("arbitrary",)),
            name="moe_down",
        )(a2d, w_down, gates2d, b_down, x2d, mods_l, gain, mods_next, mods_next)
    latent_tiles = tiles_per_batch - n_ctx_tiles
    return pl.pallas_call(
        functools.partial(_moe_down_final_kernel, **static),
        grid=(r // tr,),
        in_specs=in_specs,
        out_specs=pl.BlockSpec((None, tr, d), lambda i: (i // tiles_per_batch,
                                                          jnp.maximum(i % tiles_per_batch - n_ctx_tiles, 0), 0)),
        out_shape=jax.ShapeDtypeStruct((out_batch, latent_tiles * tr, d), F32),
        compiler_params=_cparams(("arbitrary",)),
        name="moe_down_final",
    )(a2d, w_down, gates2d, b_down, x2d, mods_l, gain)


def _rope_tables(n, ctx_len):
    rows = n // GRID_W
    row = jnp.repeat(jnp.arange(rows, dtype=F32), GRID_W)
    col = jnp.tile(jnp.arange(GRID_W, dtype=F32), rows)

    def cos_sin(rot_dim):
        pairs = rot_dim // 4
        inv = ROPE_BASE ** (-jnp.arange(pairs, dtype=F32) / pairs)
        ang = jnp.concatenate([row[:, None] * inv, col[:, None] * inv], axis=-1)
        cos = jnp.concatenate([jnp.ones((ctx_len, rot_dim // 2), F32), jnp.cos(ang)], axis=0)
        sin = jnp.concatenate([jnp.zeros((ctx_len, rot_dim // 2), F32), jnp.sin(ang)], axis=0)
        return cos, sin

    cos32, sin32 = cos_sin(DIFF_HALF)
    cos64, sin64 = cos_sin(RET_DK)
    z32 = jnp.zeros_like(sin32)
    one64 = jnp.ones((cos32.shape[0], 64), F32)
    z64 = jnp.zeros_like(one64)
    return {
        "c64": jnp.concatenate([cos32] * 4, axis=-1),
        "sa64": jnp.concatenate([-sin32, z32, -sin32, z32], axis=-1),
        "sb64": jnp.concatenate([z32, sin32, z32, sin32], axis=-1),
        "c64r": jnp.concatenate([cos32, cos32, one64], axis=-1),
        "sa64r": jnp.concatenate([-sin32, z32, z64], axis=-1),
        "sb64r": jnp.concatenate([z32, sin32, z64], axis=-1),
        "c128": jnp.concatenate([cos64, cos64], axis=-1),
        "s128": jnp.concatenate([-sin64, sin64], axis=-1),
    }


def _prep_weights(w_in, mla_wq_b, mla_wkv_b, w_out, router_w, router_b,
                  exp_w_gu, exp_b_gu, exp_w_down, exp_b_down):
    depth, d, in_width = w_in.shape
    w_tail = jnp.pad(w_in[:, :, MAIN_WIDTH:], ((0, 0), (0, 0), (0, TAIL_WIDTH - (in_width - MAIN_WIDTH)))).astype(BF16)
    wq = mla_wq_b.reshape(depth, MLA_Q_RANK, MLA_HEADS, MLA_NOPE + MLA_ROPE)
    wq = jnp.pad(wq, ((0, 0), (0, 0), (0, 0), (0, MLA_QK_PAD - MLA_NOPE - MLA_ROPE)))
    wq = wq.reshape(depth, MLA_Q_RANK, MLA_HEADS * MLA_QK_PAD).astype(BF16)
    wkv = mla_wkv_b.reshape(depth, MLA_KV_RANK, MLA_HEADS, MLA_NOPE + MLA_V)
    wkv = jnp.concatenate([wkv[..., :MLA_NOPE].reshape(depth, MLA_KV_RANK, -1),
                           wkv[..., MLA_NOPE:].reshape(depth, MLA_KV_RANK, -1)], axis=-1).astype(BF16)
    rw = jnp.pad(router_w, ((0, 0), (0, 0), (0, LANES - N_EXPERTS))).astype(BF16)
    rb = jnp.concatenate([router_b, jnp.full((depth, LANES - N_EXPERTS), NEG_BIG, F32)], axis=-1)
    return dict(
        w_in=w_in.astype(BF16), w_tail=w_tail, wq=wq, wkv=wkv, w_out=w_out.astype(BF16),
        rw=rw, rb=rb.reshape(depth, 1, LANES),
        w_gu=exp_w_gu.astype(BF16),
        b_gu=exp_b_gu.reshape(depth, 1, N_EXPERTS * 2 * EXPERT_FF),
        w_down=exp_w_down.astype(BF16).reshape(depth, N_EXPERTS * EXPERT_FF, d),
        b_down=jnp.pad(exp_b_down, ((0, 0), (0, LANES - N_EXPERTS), (0, 0))).astype(BF16))


def kernel(x, c, ctx, c_ctx, ada_w, ada_b, norm1, norm2, w_in, diff_lq1, diff_lk1, diff_lq2, diff_lk2, diff_subln, ret_decay, ret_norm, mla_q_norm, mla_kv_norm, mla_wq_b, mla_wkv_b, w_out, router_w, router_b, exp_w_gu, exp_b_gu, exp_w_down, exp_b_down, final_norm):
    batch, n, d = x.shape
    ctx_len = ctx.shape[1]
    depth = ada_w.shape[0]
    s = ctx_len + n
    r = batch * s
    assert d == D_MODEL and batch <= CTX_MOD_ROW and ctx_len % RET_CHUNK == 0 and n % RET_CHUNK == 0
    assert depth >= 1 and RET_CHUNK % ROW_TILE == 0

    tq = _pick_tile(ctx_len, (256, 128))
    n_ctx_tiles = ctx_len // tq
    tm = _pick_tile(s, (768, 576, 384, 256, 128), at_least=ctx_len)
    tm_mla = _pick_tile(s, (384, 256, 128))
    tiles_per_batch = s // tm
    ret_chunk = _pick_tile(math.gcd(ctx_len, n), (2 * RET_CHUNK, RET_CHUNK))

    xs = jnp.concatenate([ctx, x], axis=1)
    mods = _ada_tables(_stack_conditioning(c, c_ctx), ada_w, ada_b)
    tabs = _rope_tables(n, ctx_len)
    log_gamma = jnp.log1p(-jnp.exp2(-ret_decay.astype(F32)))

    wts = _prep_weights(w_in, mla_wq_b, mla_wkv_b, w_out, router_w, router_b,
                        exp_w_gu, exp_b_gu, exp_w_down, exp_b_down)
    q_norm = mla_q_norm.reshape(depth, 1, -1)
    kv_norm = mla_kv_norm.reshape(depth, 1, -1)

    n_row_tiles = s // ROW_TILE
    n_ctx_row_tiles = ctx_len // ROW_TILE
    x2d = xs.reshape(r, d)
    h2d = _norm_call(xs, norm1[0].reshape(1, d), mods[0], 0, 1, n_ctx_tiles, tq).reshape(r, d)

    for l in range(depth):
        lam_init = 0.8 - 0.6 * math.exp(-0.3 * l)
        mods_l = mods[l]
        p2d = _proj_call(h2d, wts["w_in"], l, tabs, tm, tiles_per_batch)
        p3d = p2d.reshape(batch, s, MAIN_WIDTH)
        qc = _mla_q_call(p2d, q_norm, wts["wq"], l, tabs, tm_mla, s // tm_mla)
        kc, vc = _mla_kv_call(h2d, wts["w_tail"], kv_norm, wts["wkv"], l, tabs, tm_mla, s // tm_mla)
        lamv = jnp.stack([diff_lq1[l], diff_lk1[l], diff_lq2[l], diff_lk2[l]]).astype(F32)
        a = _diff_attn_call(p3d, lamv, diff_subln[l].reshape(1, -1), lam_init, tq, n_ctx_tiles, ctx_len)
        bb = _ret_call(p3d, log_gamma[l], ret_norm[l].reshape(1, -1), ret_chunk, ctx_len // ret_chunk)
        cc = _mla_attn_call(qc.reshape(batch, s, -1), kc.reshape(batch, s, -1), vc.reshape(batch, s, -1),
                            tq, n_ctx_tiles, ctx_len)
        x2d, h2 = _out_proj_call(a.reshape(r, -1), bb.reshape(r, -1), cc.reshape(r, -1), wts["w_out"], l, x2d,
                                 mods_l, norm2[l].reshape(1, d), n_row_tiles, n_ctx_row_tiles)
        act, gates2d = _moe_up_call(h2, wts["w_gu"], wts["b_gu"], wts["rw"], wts["rb"], l, tm)
        if l + 1 < depth:
            x2d, h2d = _moe_down_call(act, wts["w_down"], gates2d, wts["b_down"], l, x2d, mods_l,
                                      norm1[l + 1].reshape(1, d), mods[l + 1], n_row_tiles, n_ctx_row_tiles)
        else:
            return _moe_down_call(act, wts["w_down"], gates2d, wts["b_down"], l, x2d, mods_l,
                                  final_norm.reshape(1, d), None, n_row_tiles, n_ctx_row_tiles, out_batch=batch)
```

```python
import functools
import math

import jax
import jax.numpy as jnp
from jax import lax
from jax.experimental import pallas as pl
from jax.experimental.pallas import tpu as pltpu

F32 = jnp.float32
BF16 = jnp.bfloat16

D_MODEL = 4096
GRID_W = 64
RET_CHUNK = 128
ROPE_BASE = 10000.0
EPS = 1e-6
HEAD_DIM = 128
DIFF_HEADS = 8
DIFF_HALF = 64
DIFF_WIDTH = 1024
RET_HEADS = 8
RET_DK = 128
RET_WIDTH = 1024
MLA_HEADS = 16
MLA_Q_RANK = 1024
MLA_KV_RANK = 512
MLA_NOPE = 128
MLA_ROPE = 64
MLA_V = 128
MLA_WIDTH = 2048
MLA_QK_PAD = 256
N_EXPERTS = 32
TOP_K = 4
EXPERT_FF = 128
SWIGLU_LIMIT = 7.0
SWIGLU_ALPHA = 1.702
MAIN_WIDTH = 8192
TAIL_WIDTH = 640
SEG = 1024

LANES = 128
VMEM_LIMIT_BYTES = 56 * 1024 * 1024
NEG_BIG = -1e30
CTX_MOD_ROW = 8
MOD_ROWS = 16


def _cparams(sem):
    return pltpu.CompilerParams(dimension_semantics=sem, vmem_limit_bytes=VMEM_LIMIT_BYTES)


def _pick_tile(total, candidates, at_least=1):
    for c in candidates:
        if total % c == 0 and c >= at_least:
            return c
    raise ValueError(f"no tile for {total}")


def _ada_kernel(c_ref, w_ref, b_ref, o_ref):
    c = c_ref[...]
    cs = (c * jax.nn.sigmoid(c)).astype(BF16)
    o_ref[...] = jnp.dot(cs, w_ref[...].astype(BF16), preferred_element_type=F32) + b_ref[...]


def _stack_conditioning(c, c_ctx):
    batch, d = c.shape
    return jnp.concatenate([c, jnp.zeros((CTX_MOD_ROW - batch, d), F32), c_ctx[None, :],
                            jnp.zeros((MOD_ROWS - CTX_MOD_ROW - 1, d), F32)], axis=0)


def _ada_tables(c_all, ada_w, ada_b):
    depth, d, n6 = ada_w.shape
    tn = 512
    return pl.pallas_call(
        _ada_kernel,
        grid=(depth, n6 // tn),
        in_specs=[
            pl.BlockSpec((MOD_ROWS, d), lambda l, j: (0, 0)),
            pl.BlockSpec((None, d, tn), lambda l, j: (l, 0, j)),
            pl.BlockSpec((None, 1, tn), lambda l, j: (l, 0, j)),
        ],
        out_specs=pl.BlockSpec((None, MOD_ROWS, tn), lambda l, j: (l, 0, j)),
        out_shape=jax.ShapeDtypeStruct((depth, MOD_ROWS, n6), F32),
        compiler_params=_cparams(("arbitrary", "arbitrary")),
        name="ada_tables",
    )(c_all, ada_w, ada_b.reshape(depth, 1, n6))


def _mod_norm(x, gain, shift, scale):
    ms = jnp.mean(x * x, axis=-1, keepdims=True)
    return x * lax.rsqrt(ms + EPS) * gain * (1.0 + scale) + shift


def _norm_kernel(x_ref, g_ref, sh_ref, sc_ref, o_ref, *, n_ctx_tiles):
    row = jnp.where(pl.program_id(1) < n_ctx_tiles, CTX_MOD_ROW, pl.program_id(0))
    o_ref[...] = _mod_norm(x_ref[...], g_ref[...], sh_ref[pl.ds(row, 1), :], sc_ref[pl.ds(row, 1), :]).astype(BF16)


def _router_gates(h, rw_ref, rb_ref):
    logits = jnp.dot(h, rw_ref[...], preferred_element_type=F32) + rb_ref[...]
    lane = lax.broadcasted_iota(jnp.int32, logits.shape, 1)
    work = logits
    num = jnp.zeros_like(logits)
    den = jnp.zeros((logits.shape[0], 1), F32)
    m0 = None
    for k in range(TOP_K):
        m = jnp.max(work, axis=-1, keepdims=True)
        idx = jnp.min(jnp.where(work == m, lane, LANES), axis=-1, keepdims=True)
        pick = lane == idx
        if k == 0:
            m0 = m
        e = jnp.exp(m - m0)
        num = num + jnp.where(pick, e, 0.0)
        den = den + e
        work = jnp.where(pick, -jnp.inf, work)
    return num / den


def _norm_call(xs, gain, mods_l, shift_blk, scale_blk, n_ctx_tiles, tr):
    b, s, d = xs.shape
    x_spec = pl.BlockSpec((None, tr, d), lambda bi, i: (bi, i, 0))
    return pl.pallas_call(
        functools.partial(_norm_kernel, n_ctx_tiles=n_ctx_tiles),
        grid=(b, s // tr),
        in_specs=[x_spec,
                  pl.BlockSpec((1, d), lambda bi, i: (0, 0)),
                  pl.BlockSpec((MOD_ROWS, d), lambda bi, i: (0, shift_blk)),
                  pl.BlockSpec((MOD_ROWS, d), lambda bi, i: (0, scale_blk))],
        out_specs=x_spec,
        out_shape=jax.ShapeDtypeStruct((b, s, d), BF16),
        compiler_params=_cparams(("parallel", "parallel")),
        name="mod_norm",
    )(xs, gain, mods_l, mods_l)


def _rope64(blk, c, sa, sb):
    return blk * c + pltpu.roll(blk, 96, 1) * sa + pltpu.roll(blk, 32, 1) * sb


PROJ_SPLIT = 4


def _proj_kernel(h_ref, w_ref, c64_ref, sa64_ref, sb64_ref, c128_ref, s128_ref, o_ref):
    j = pl.program_id(1)
    h = h_ref[...]
    part = w_ref.shape[1] // PROJ_SPLIT
    n_blk = part // LANES

    def parts():
        for p in range(PROJ_SPLIT):
            acc = jnp.dot(h, w_ref[:, p * part:(p + 1) * part], preferred_element_type=F32)
            for t in range(n_blk):
                yield slice(p * part + t * LANES, p * part + (t + 1) * LANES), acc[:, t * LANES:(t + 1) * LANES]

    @pl.when((j == 0) | (j == 1))
    def _():
        scale = jnp.where(j == 0, DIFF_HALF ** -0.5, 1.0).astype(F32)
        c, sa, sb = c64_ref[...] * scale, sa64_ref[...] * scale, sb64_ref[...] * scale
        for sl, blk in parts():
            o_ref[:, sl] = _rope64(blk, c, sa, sb).astype(BF16)

    @pl.when((j == 3) | (j == 4))
    def _():
        scale = jnp.where(j == 4, RET_DK ** -0.5, 1.0).astype(F32)
        c, s = c128_ref[...] * scale, s128_ref[...] * scale
        for sl, blk in parts():
            o_ref[:, sl] = (blk * c + pltpu.roll(blk, 64, 1) * s).astype(BF16)

    @pl.when((j == 2) | (j >= 5))
    def _():
        for sl, blk in parts():
            o_ref[:, sl] = blk.astype(BF16)


def _proj_call(h2d, w_in_bf, l, tabs, tm, tiles_per_batch):
    r, d = h2d.shape
    tab_spec = pl.BlockSpec((tm, LANES), lambda i, j: (i % tiles_per_batch, 0))
    return pl.pallas_call(
        _proj_kernel,
        grid=(r // tm, MAIN_WIDTH // SEG),
        in_specs=[pl.BlockSpec((tm, d), lambda i, j: (i, 0)),
                  pl.BlockSpec((None, d, SEG), lambda i, j: (l, 0, j)),
                  tab_spec, tab_spec, tab_spec, tab_spec, tab_spec],
        out_specs=pl.BlockSpec((tm, SEG), lambda i, j: (i, j)),
        out_shape=jax.ShapeDtypeStruct((r, MAIN_WIDTH), BF16),
        compiler_params=_cparams(("parallel", "arbitrary")),
        name="in_proj",
    )(h2d, w_in_bf, tabs["c64"], tabs["sa64"], tabs["sb64"], tabs["c128"], tabs["s128"])


def _rms_gain(x, g_ref):
    ms = jnp.mean(x * x, axis=-1, keepdims=True)
    return x * lax.rsqrt(ms + EPS) * g_ref[...]


def _mla_q_kernel(qa_ref, g_ref, w_ref, c_ref, sa_ref, sb_ref, o_ref):
    qn = _rms_gain(qa_ref[...].astype(F32), g_ref).astype(BF16)
    q = jnp.dot(qn, w_ref[...], preferred_element_type=F32)
    scale = (MLA_NOPE + MLA_ROPE) ** -0.5
    c, sa, sb = c_ref[...] * scale, sa_ref[...] * scale, sb_ref[...] * scale
    for h in range(MLA_HEADS):
        lo = h * MLA_QK_PAD
        o_ref[:, lo:lo + LANES] = (q[:, lo:lo + LANES] * scale).astype(BF16)
        o_ref[:, lo + LANES:lo + 2 * LANES] = _rope64(q[:, lo + LANES:lo + 2 * LANES], c, sa, sb).astype(BF16)


def _mla_kv_kernel(h_ref, wt_ref, g_ref, w_ref, c_ref, sa_ref, sb_ref, k_ref, v_ref):
    t = jnp.dot(h_ref[...], wt_ref[...], preferred_element_type=F32)
    kvn = _rms_gain(t[:, :MLA_KV_RANK], g_ref).astype(BF16)
    kr = _rope64(t[:, MLA_KV_RANK:MLA_KV_RANK + LANES], c_ref[...], sa_ref[...], sb_ref[...]).astype(BF16)
    kv = jnp.dot(kvn, w_ref[...], preferred_element_type=F32)
    kn_width = MLA_HEADS * MLA_NOPE
    for h in range(MLA_HEADS):
        lo = h * MLA_QK_PAD
        k_ref[:, lo:lo + LANES] = kv[:, h * MLA_NOPE:(h + 1) * MLA_NOPE].astype(BF16)
        k_ref[:, lo + LANES:lo + 2 * LANES] = kr
    v_ref[...] = kv[:, kn_width:].astype(BF16)


def _mla_q_call(p2d, q_norm, wq, l, tabs, tm, tiles_per_batch):
    r = p2d.shape[0]
    tab_spec = pl.BlockSpec((tm, LANES), lambda i: (i % tiles_per_batch, 0))
    width = MLA_HEADS * MLA_QK_PAD
    return pl.pallas_call(
        _mla_q_kernel,
        grid=(r // tm,),
        in_specs=[pl.BlockSpec((tm, MLA_Q_RANK), lambda i: (i, (MAIN_WIDTH - MLA_Q_RANK) // MLA_Q_RANK)),
                  pl.BlockSpec((None, 1, MLA_Q_RANK), lambda i: (l, 0, 0)),
                  pl.BlockSpec((None, MLA_Q_RANK, width), lambda i: (l, 0, 0)),
                  tab_spec, tab_spec, tab_spec],
        out_specs=pl.BlockSpec((tm, width), lambda i: (i, 0)),
        out_shape=jax.ShapeDtypeStruct((r, width), BF16),
        compiler_params=_cparams(("parallel",)),
        name="mla_q_proj",
    )(p2d, q_norm, wq, tabs["c64r"], tabs["sa64r"], tabs["sb64r"])


def _mla_kv_call(h2d, w_tail, kv_norm, wkv, l, tabs, tm, tiles_per_batch):
    r, d = h2d.shape
    tab_spec = pl.BlockSpec((tm, LANES), lambda i: (i % tiles_per_batch, 0))
    kwidth = MLA_HEADS * MLA_QK_PAD
    return pl.pallas_call(
        _mla_kv_kernel,
        grid=(r // tm,),
        in_specs=[pl.BlockSpec((tm, d), lambda i: (i, 0)),
                  pl.BlockSpec((None, d, TAIL_WIDTH), lambda i: (l, 0, 0)),
                  pl.BlockSpec((None, 1, MLA_KV_RANK), lambda i: (l, 0, 0)),
                  pl.BlockSpec((None, MLA_KV_RANK, 2 * MLA_WIDTH), lambda i: (l, 0, 0)),
                  tab_spec, tab_spec, tab_spec],
        out_specs=[pl.BlockSpec((tm, kwidth), lambda i: (i, 0)),
                   pl.BlockSpec((tm, MLA_WIDTH), lambda i: (i, 0))],
        out_shape=[jax.ShapeDtypeStruct((r, kwidth), BF16),
                   jax.ShapeDtypeStruct((r, MLA_WIDTH), BF16)],
        compiler_params=_cparams(("parallel",)),
        name="mla_kv_proj",
    )(h2d, w_tail, kv_norm, wkv, tabs["c64r"], tabs["sa64r"], tabs["sb64r"])


KEY_CHUNK = 768


def _softmax_pv(q, k_ref, vo_ref, m_keys, k_cols, v_cols):
    chunk = _pick_tile(m_keys, (KEY_CHUNK, 512, 384, 256, 128))
    m_run = None
    acc = None
    for c0 in range(0, m_keys, chunk):
        s = lax.dot_general(q, k_ref[c0:c0 + chunk, k_cols], _NT, preferred_element_type=F32)
        m_c = jnp.max(s, axis=-1, keepdims=True)
        v_ones = vo_ref[c0:c0 + chunk, v_cols]
        if m_run is None:
            m_run = m_c
            acc = jnp.dot(jnp.exp((s - m_run).astype(BF16)), v_ones, preferred_element_type=F32)
        else:
            m_new = jnp.maximum(m_run, m_c)
            acc = (acc * jnp.exp(m_run - m_new)
                   + jnp.dot(jnp.exp((s - m_new).astype(BF16)), v_ones, preferred_element_type=F32))
            m_run = m_new
    half = acc.shape[1] // 2
    return acc[:, :half] / acc[:, half:]


def _fill_v_ones(v_ref, vo_ref, n_heads, width):
    ones = jnp.ones((v_ref.shape[0], width), BF16)
    for h in range(n_heads):
        vo_ref[:, 2 * width * h:2 * width * h + width] = v_ref[:, width * h:width * (h + 1)]
        vo_ref[:, 2 * width * h + width:2 * width * (h + 1)] = ones


_NT = (((1,), (1,)), ((), ()))


def _diff_attn_kernel(lam_ref, sub_ref, q_ref, k_ref, v_ref, o_ref, vo_ref, *, n_ctx_tiles, ctx_len, lam_init):
    @pl.when(pl.program_id(1) == 0)
    def _():
        _fill_v_ones(v_ref, vo_ref, DIFF_HEADS, HEAD_DIM)

    lv = lam_ref[...]
    lam = (jnp.exp(jnp.sum(lv[0:1] * lv[1:2], axis=-1, keepdims=True))
           - jnp.exp(jnp.sum(lv[2:3] * lv[3:4], axis=-1, keepdims=True)) + lam_init)
    sub = sub_ref[...] * (1.0 - lam_init)

    def run(m_keys):
        for h in range(DIFF_HEADS):
            sl = slice(h * HEAD_DIM, (h + 1) * HEAD_DIM)
            q = q_ref[:, sl]
            v_cols = slice(2 * h * HEAD_DIM, 2 * (h + 1) * HEAD_DIM)
            first = lax.broadcasted_iota(jnp.int32, q.shape, 1) < DIFF_HALF
            zero = jnp.zeros_like(q)
            o = (_softmax_pv(jnp.where(first, q, zero), k_ref, vo_ref, m_keys, sl, v_cols)
                 - lam * _softmax_pv(jnp.where(first, zero, q), k_ref, vo_ref, m_keys, sl, v_cols))
            ms = jnp.mean(o * o, axis=-1, keepdims=True)
            o_ref[:, sl] = (o * lax.rsqrt(ms + EPS) * sub).astype(BF16)

    is_ctx = pl.program_id(1) < n_ctx_tiles

    @pl.when(is_ctx)
    def _():
        run(ctx_len)

    @pl.when(jnp.logical_not(is_ctx))
    def _():
        run(k_ref.shape[0])


def _diff_attn_call(p3d, lamv, subln, lam_init, tq, n_ctx_tiles, ctx_len):
    b, s, _ = p3d.shape
    return pl.pallas_call(
        functools.partial(_diff_attn_kernel, n_ctx_tiles=n_ctx_tiles, ctx_len=ctx_len, lam_init=lam_init),
        grid=(b, s // tq),
        in_specs=[pl.BlockSpec((4, DIFF_HALF), lambda bi, i: (0, 0)),
                  pl.BlockSpec((1, HEAD_DIM), lambda bi, i: (0, 0)),
                  pl.BlockSpec((None, tq, DIFF_WIDTH), lambda bi, i: (bi, i, 0)),
                  pl.BlockSpec((None, s, DIFF_WIDTH), lambda bi, i: (bi, 0, 1)),
                  pl.BlockSpec((None, s, DIFF_WIDTH), lambda bi, i: (bi, 0, 2))],
        out_specs=pl.BlockSpec((None, tq, DIFF_WIDTH), lambda bi, i: (bi, i, 0)),
        out_shape=jax.ShapeDtypeStruct((b, s, DIFF_WIDTH), BF16),
        scratch_shapes=[pltpu.VMEM((s, 2 * DIFF_WIDTH), BF16)],
        compiler_params=_cparams(("parallel", "arbitrary")),
        name="diff_attn",
    )(lamv, subln, p3d, p3d, p3d)


MLA_HEAD_GROUP = 8


def _mla_attn_kernel(q_ref, k_ref, v_ref, o_ref, vo_ref, *, n_ctx_tiles, ctx_len):
    @pl.when(pl.program_id(2) == 0)
    def _():
        _fill_v_ones(v_ref, vo_ref, MLA_HEAD_GROUP, MLA_V)

    def run(m_keys):
        for h in range(MLA_HEAD_GROUP):
            qk = slice(h * MLA_QK_PAD, (h + 1) * MLA_QK_PAD)
            vv = slice(h * MLA_V, (h + 1) * MLA_V)
            v_cols = slice(2 * h * MLA_V, 2 * (h + 1) * MLA_V)
            o_ref[:, vv] = _softmax_pv(q_ref[:, qk], k_ref, vo_ref, m_keys, qk, v_cols).astype(BF16)

    is_ctx = pl.program_id(2) < n_ctx_tiles

    @pl.when(is_ctx)
    def _():
        run(ctx_len)

    @pl.when(jnp.logical_not(is_ctx))
    def _():
        run(k_ref.shape[0])


def _mla_attn_call(qc, kc, vc, tq, n_ctx_tiles, ctx_len):
    b, s, _ = qc.shape
    qw = MLA_HEAD_GROUP * MLA_QK_PAD
    vw = MLA_HEAD_GROUP * MLA_V
    return pl.pallas_call(
        functools.partial(_mla_attn_kernel, n_ctx_tiles=n_ctx_tiles, ctx_len=ctx_len),
        grid=(b, MLA_HEADS // MLA_HEAD_GROUP, s // tq),
        in_specs=[pl.BlockSpec((None, tq, qw), lambda bi, g, i: (bi, i, g)),
                  pl.BlockSpec((None, s, qw), lambda bi, g, i: (bi, 0, g)),
                  pl.BlockSpec((None, s, vw), lambda bi, g, i: (bi, 0, g))],
        out_specs=pl.BlockSpec((None, tq, vw), lambda bi, g, i: (bi, i, g)),
        out_shape=jax.ShapeDtypeStruct((b, s, MLA_WIDTH), BF16),
        scratch_shapes=[pltpu.VMEM((s, 2 * vw), BF16)],
        compiler_params=_cparams(("parallel", "parallel", "arbitrary")),
        name="mla_attn",
    )(qc, kc, vc)


RET_HEAD_GROUP = 4
_TN = (((0,), (0,)), ((), ()))


def _ret_kernel(lg_ref, q_ref, k_ref, v_ref, g_ref, gn_ref, o_ref, intra_ref, rowdec_ref, cdec_ref, st_ref, acc_ref,
                *, chunk, n_ctx_chunks):
    c = chunk
    n_chunks = q_ref.shape[0] // c
    hp = pl.program_id(1)
    row = lax.broadcasted_iota(jnp.int32, (c, c), 0).astype(F32)
    col = lax.broadcasted_iota(jnp.int32, (c, c), 1).astype(F32)
    rel = row - col
    pos = lax.broadcasted_iota(jnp.int32, (c, RET_DK), 0).astype(F32)
    span = jnp.full((RET_DK, RET_DK), float(c), F32)

    for hh in range(RET_HEAD_GROUP):
        lgf = lg_ref[0, hp * RET_HEAD_GROUP + hh]
        lgb = lg_ref[1, hp * RET_HEAD_GROUP + hh]
        intra_ref[hh, 0] = jnp.where(rel >= 0, jnp.exp(jnp.maximum(rel, 0.0) * lgf), 0.0)
        intra_ref[hh, 1] = jnp.where(rel <= 0, jnp.exp(jnp.maximum(-rel, 0.0) * lgb), 0.0)
        rowdec_ref[hh, 0] = jnp.exp((pos + 1.0) * lgf)
        rowdec_ref[hh, 1] = jnp.exp((c - 1.0 - pos) * lgf)
        rowdec_ref[hh, 2] = jnp.exp((c - pos) * lgb)
        rowdec_ref[hh, 3] = jnp.exp(pos * lgb)
        cdec_ref[hh, 0] = jnp.exp(span * lgf)
        cdec_ref[hh, 1] = jnp.exp(span * lgb)

    st_ref[...] = jnp.zeros_like(st_ref)
    acc_ref[...] = jnp.zeros_like(acc_ref)

    def chunk_update(hh, direction, idx):
        sl = slice(hh * RET_DK, (hh + 1) * RET_DK)
        rows = pl.ds(pl.multiple_of(idx * c, c), c)
        qc = q_ref[rows, sl]
        kc = k_ref[rows, sl]
        vc = v_ref[rows, sl]
        state = st_ref[hh, direction]
        att = lax.dot_general(qc, kc, _NT, preferred_element_type=F32) * intra_ref[hh, direction]
        o = (jnp.dot(att.astype(BF16), vc, preferred_element_type=F32)
             + jnp.dot(qc, state.astype(BF16), preferred_element_type=F32) * rowdec_ref[hh, 2 * direction])
        kd = (kc.astype(F32) * rowdec_ref[hh, 2 * direction + 1]).astype(BF16)
        st_ref[hh, direction] = (state * cdec_ref[hh, direction]
                                 + lax.dot_general(kd, vc, _TN, preferred_element_type=F32))
        acc_ref[rows, sl] += o

    def step(t, carry):
        bwd = jnp.where(t < n_ctx_chunks, n_ctx_chunks - 1 - t, n_chunks - 1 - t + n_ctx_chunks)
        for hh in range(RET_HEAD_GROUP):
            chunk_update(hh, 0, t)
            chunk_update(hh, 1, bwd)
        return carry

    lax.fori_loop(0, n_chunks, step, 0)

    for hh in range(RET_HEAD_GROUP):
        sl = slice(hh * RET_DK, (hh + 1) * RET_DK)
        o = acc_ref[:, sl]
        mu = jnp.mean(o, axis=-1, keepdims=True)
        oc = o - mu
        y = oc * lax.rsqrt(jnp.mean(oc * oc, axis=-1, keepdims=True) + 1e-5) * gn_ref[:, sl]
        g = g_ref[:, sl].astype(F32)
        o_ref[:, sl] = (y * (g * jax.nn.sigmoid(g))).astype(BF16)


def _ret_call(p3d, log_gamma, gn_gain, chunk, n_ctx_chunks):
    b, s, _ = p3d.shape
    w = RET_HEAD_GROUP * RET_DK
    base = 3 * SEG // w

    def col(seg):
        return lambda bi, hp, lg: (bi, 0, base + seg * (SEG // w) + hp)

    grid_spec = pltpu.PrefetchScalarGridSpec(
        num_scalar_prefetch=1,
        grid=(b, RET_HEADS // RET_HEAD_GROUP),
        in_specs=[pl.BlockSpec((None, s, w), col(0)),
                  pl.BlockSpec((None, s, w), col(1)),
                  pl.BlockSpec((None, s, w), col(2)),
                  pl.BlockSpec((None, s, w), col(3)),
                  pl.BlockSpec((1, w), lambda bi, hp, lg: (0, hp))],
        out_specs=pl.BlockSpec((None, s, w), lambda bi, hp, lg: (bi, 0, hp)),
        scratch_shapes=[pltpu.VMEM((RET_HEAD_GROUP, 2, chunk, chunk), F32),
                        pltpu.VMEM((RET_HEAD_GROUP, 4, chunk, RET_DK), F32),
                        pltpu.VMEM((RET_HEAD_GROUP, 2, RET_DK, RET_DK), F32),
                        pltpu.VMEM((RET_HEAD_GROUP, 2, RET_DK, RET_DK), F32),
                        pltpu.VMEM((s, w), F32)],
    )
    return pl.pallas_call(
        functools.partial(_ret_kernel, chunk=chunk, n_ctx_chunks=n_ctx_chunks),
        grid_spec=grid_spec,
        out_shape=jax.ShapeDtypeStruct((b, s, RET_WIDTH), BF16),
        compiler_params=_cparams(("parallel", "arbitrary")),
        name="retention",
    )(log_gamma, p3d, p3d, p3d, p3d, gn_gain)


ROW_TILE = 128


def _tile_mod_row(tiles_per_batch, n_ctx_tiles):
    i = pl.program_id(0)
    return jnp.where(i % tiles_per_batch < n_ctx_tiles, CTX_MOD_ROW, i // tiles_per_batch)


def _resident(shape, index_map):
    return pl.BlockSpec(shape, index_map, pipeline_mode=pl.Buffered(1))


def _out_proj_kernel(a_ref, b_ref, c_ref, w_ref, x_ref, g1_ref, gain_ref, sh_ref, sc_ref, xo_ref, h_ref,
                     *, tiles_per_batch, n_ctx_tiles):
    row = pl.ds(_tile_mod_row(tiles_per_batch, n_ctx_tiles), 1)
    acc = jnp.dot(a_ref[...], w_ref[0:DIFF_WIDTH, :], preferred_element_type=F32)
    acc += jnp.dot(b_ref[...], w_ref[DIFF_WIDTH:DIFF_WIDTH + RET_WIDTH, :], preferred_element_type=F32)
    acc += jnp.dot(c_ref[...], w_ref[DIFF_WIDTH + RET_WIDTH:, :], preferred_element_type=F32)
    x_new = x_ref[...] + g1_ref[row, :] * acc
    xo_ref[...] = x_new
    h_ref[...] = _mod_norm(x_new, gain_ref[...], sh_ref[row, :], sc_ref[row, :]).astype(BF16)


def _out_proj_call(a2d, b2d, c2d, w_out, l, x2d, mods_l, gain, tiles_per_batch, n_ctx_tiles):
    r, d = x2d.shape
    tr = ROW_TILE
    rows = lambda w: pl.BlockSpec((tr, w), lambda i: (i, 0))
    mod = lambda blk: pl.BlockSpec((MOD_ROWS, d), lambda i: (0, blk))
    return pl.pallas_call(
        functools.partial(_out_proj_kernel, tiles_per_batch=tiles_per_batch, n_ctx_tiles=n_ctx_tiles),
        grid=(r // tr,),
        in_specs=[rows(DIFF_WIDTH), rows(RET_WIDTH), rows(MLA_WIDTH),
                  _resident((None, d, d), lambda i: (l, 0, 0)),
                  rows(d), mod(2),
                  pl.BlockSpec((1, d), lambda i: (0, 0)), mod(3), mod(4)],
        out_specs=[rows(d), rows(d)],
        out_shape=[jax.ShapeDtypeStruct((r, d), F32), jax.ShapeDtypeStruct((r, d), BF16)],
        compiler_params=_cparams(("arbitrary",)),
        name="out_proj",
    )(a2d, b2d, c2d, w_out, x2d, mods_l, gain, mods_l, mods_l)


def _moe_up_kernel(h_ref, w_ref, b_ref, rw_ref, rb_ref, o_ref, gate_out_ref, gate_ref):
    j = pl.program_id(1)
    n_e = w_ref.shape[0]

    @pl.when(j == 0)
    def _():
        gates = _router_gates(h_ref[...], rw_ref, rb_ref)
        gate_ref[...] = gates
        gate_out_ref[...] = gates

    e_row = lax.broadcasted_iota(jnp.int32, (LANES, n_e * EXPERT_FF), 0)
    e_col = lax.broadcasted_iota(jnp.int32, (LANES, n_e * EXPERT_FF), 1) // EXPERT_FF
    expand = (e_row == j * n_e + e_col).astype(BF16)
    ge = jnp.dot(gate_ref[...].astype(BF16), expand, preferred_element_type=F32)
    h = h_ref[...]
    for e in range(n_e):
        lo = e * 2 * EXPERT_FF
        gu = jnp.dot(h, w_ref[e], preferred_element_type=F32) + b_ref[:, lo:lo + 2 * EXPERT_FF]
        glu = jnp.minimum(gu[:, :EXPERT_FF], SWIGLU_LIMIT)
        lin = jnp.clip(gu[:, EXPERT_FF:], -SWIGLU_LIMIT, SWIGLU_LIMIT)
        act = glu * jax.nn.sigmoid(SWIGLU_ALPHA * glu) * (lin + 1.0)
        sl = slice(e * EXPERT_FF, (e + 1) * EXPERT_FF)
        o_ref[:, sl] = (act * ge[:, sl]).astype(BF16)


def _moe_up_call(h2d, w_gu, b_gu, rw, rb, l, tm):
    r, d = h2d.shape
    tn = SEG
    n_e = tn // (2 * EXPERT_FF)
    n = N_EXPERTS * 2 * EXPERT_FF
    return pl.pallas_call(
        _moe_up_kernel,
        grid=(r // tm, n // tn),
        in_specs=[pl.BlockSpec((tm, d), lambda i, j: (i, 0)),
                  pl.BlockSpec((None, n_e, d, 2 * EXPERT_FF), lambda i, j: (l, j, 0, 0)),
                  pl.BlockSpec((None, 1, tn), lambda i, j: (l, 0, j)),
                  pl.BlockSpec((None, d, LANES), lambda i, j: (l, 0, 0)),
                  pl.BlockSpec((None, 1, LANES), lambda i, j: (l, 0, 0))],
        out_specs=[pl.BlockSpec((tm, tn // 2), lambda i, j: (i, j)),
                   pl.BlockSpec((tm, LANES), lambda i, j: (i, 0))],
        out_shape=[jax.ShapeDtypeStruct((r, n // 2), BF16), jax.ShapeDtypeStruct((r, LANES), F32)],
        scratch_shapes=[pltpu.VMEM((tm, LANES), F32)],
        compiler_params=_cparams(("parallel", "arbitrary")),
        name="moe_up",
    )(h2d, w_gu, b_gu, rw, rb)


def _moe_down_rows(a_ref, w_ref, gates_ref, bd_ref, x_ref, g2_ref, row):
    acc = jnp.dot(a_ref[...], w_ref[...], preferred_element_type=F32)
    acc += jnp.dot(gates_ref[...].astype(BF16), bd_ref[...], preferred_element_type=F32)
    return x_ref[...] + g2_ref[row, :] * acc


def _moe_down_kernel(a_ref, w_ref, gates_ref, bd_ref, x_ref, g2_ref, gain_ref, sh_ref, sc_ref, xo_ref, h_ref,
                     *, tiles_per_batch, n_ctx_tiles):
    row = pl.ds(_tile_mod_row(tiles_per_batch, n_ctx_tiles), 1)
    x_new = _moe_down_rows(a_ref, w_ref, gates_ref, bd_ref, x_ref, g2_ref, row)
    xo_ref[...] = x_new
    h_ref[...] = _mod_norm(x_new, gain_ref[...], sh_ref[row, :], sc_ref[row, :]).astype(BF16)


def _moe_down_final_kernel(a_ref, w_ref, gates_ref, bd_ref, x_ref, g2_ref, gain_ref, o_ref,
                           *, tiles_per_batch, n_ctx_tiles):
    row = pl.ds(_tile_mod_row(tiles_per_batch, n_ctx_tiles), 1)
    x_new = _moe_down_rows(a_ref, w_ref, gates_ref, bd_ref, x_ref, g2_ref, row)
    ms = jnp.mean(x_new * x_new, axis=-1, keepdims=True)
    o_ref[...] = x_new * lax.rsqrt(ms + EPS) * gain_ref[...]


def _moe_down_call(a2d, w_down, gates2d, b_down, l, x2d, mods_l, gain, mods_next, tiles_per_batch, n_ctx_tiles,
                   out_batch=None):
    r, d = x2d.shape
    k = a2d.shape[1]
    tr = ROW_TILE
    rows = lambda w: pl.BlockSpec((tr, w), lambda i: (i, 0))
    mod = lambda blk: pl.BlockSpec((MOD_ROWS, d), lambda i: (0, blk))
    in_specs = [rows(k), _resident((None, k, d), lambda i: (l, 0, 0)), rows(LANES),
                _resident((None, LANES, d), lambda i: (l, 0, 0)), rows(d), mod(5),
                pl.BlockSpec((1, d), lambda i: (0, 0))]
    static = dict(tiles_per_batch=tiles_per_batch, n_ctx_tiles=n_ctx_tiles)
    if out_batch is None:
        return pl.pallas_call(
            functools.partial(_moe_down_kernel, **static),
            grid=(r // tr,),
            in_specs=in_specs + [mod(0), mod(1)],
            out_specs=[rows(d), rows(d)],
            out_shape=[jax.ShapeDtypeStruct((r, d), F32), jax.ShapeDtypeStruct((r, d), BF16)],
            compiler_params=_cparams(("arbitrary",)),
            name="moe_down",
        )(a2d, w_down, gates2d, b_down, x2d, mods_l, gain, mods_next, mods_next)
    latent_tiles = tiles_per_batch - n_ctx_tiles
    return pl.pallas_call(
        functools.partial(_moe_down_final_kernel, **static),
        grid=(r // tr,),
        in_specs=in_specs,
        out_specs=pl.BlockSpec((None, tr, d), lambda i: (i // tiles_per_batch,
                                                          jnp.maximum(i % tiles_per_batch - n_ctx_tiles, 0), 0)),
        out_shape=jax.ShapeDtypeStruct((out_batch, latent_tiles * tr, d), F32),
        compiler_params=_cparams(("arbitrary",)),
        name="moe_down_final",
    )(a2d, w_down, gates2d, b_down, x2d, mods_l, gain)


def _rope_tables(n, ctx_len):
    rows = n // GRID_W
    row = jnp.repeat(jnp.arange(rows, dtype=F32), GRID_W)
    col = jnp.tile(jnp.arange(GRID_W, dtype=F32), rows)

    def cos_sin(rot_dim):
        pairs = rot_dim // 4
        inv = ROPE_BASE ** (-jnp.arange(pairs, dtype=F32) / pairs)
        ang = jnp.concatenate([row[:, None] * inv, col[:, None] * inv], axis=-1)
        cos = jnp.concatenate([jnp.ones((ctx_len, rot_dim // 2), F32), jnp.cos(ang)], axis=0)
        sin = jnp.concatenate([jnp.zeros((ctx_len, rot_dim // 2), F32), jnp.sin(ang)], axis=0)
        return cos, sin

    cos32, sin32 = cos_sin(DIFF_HALF)
    cos64, sin64 = cos_sin(RET_DK)
    z32 = jnp.zeros_like(sin32)
    one64 = jnp.ones((cos32.shape[0], 64), F32)
    z64 = jnp.zeros_like(one64)
    return {
        "c64": jnp.concatenate([cos32] * 4, axis=-1),
        "sa64": jnp.concatenate([-sin32, z32, -sin32, z32], axis=-1),
        "sb64": jnp.concatenate([z32, sin32, z32, sin32], axis=-1),
        "c64r": jnp.concatenate([cos32, cos32, one64], axis=-1),
        "sa64r": jnp.concatenate([-sin32, z32, z64], axis=-1),
        "sb64r": jnp.concatenate([z32, sin32, z64], axis=-1),
        "c128": jnp.concatenate([cos64, cos64], axis=-1),
        "s128": jnp.concatenate([-sin64, sin64], axis=-1),
    }


def _prep_weights(w_in, mla_wq_b, mla_wkv_b, w_out, router_w, router_b,
                  exp_w_gu, exp_b_gu, exp_w_down, exp_b_down):
    depth, d, in_width = w_in.shape
    w_tail = jnp.pad(w_in[:, :, MAIN_WIDTH:], ((0, 0), (0, 0), (0, TAIL_WIDTH - (in_width - MAIN_WIDTH)))).astype(BF16)
    wq = mla_wq_b.reshape(depth, MLA_Q_RANK, MLA_HEADS, MLA_NOPE + MLA_ROPE)
    wq = jnp.pad(wq, ((0, 0), (0, 0), (0, 0), (0, MLA_QK_PAD - MLA_NOPE - MLA_ROPE)))
    wq = wq.reshape(depth, MLA_Q_RANK, MLA_HEADS * MLA_QK_PAD).astype(BF16)
    wkv = mla_wkv_b.reshape(depth, MLA_KV_RANK, MLA_HEADS, MLA_NOPE + MLA_V)
    wkv = jnp.concatenate([wkv[..., :MLA_NOPE].reshape(depth, MLA_KV_RANK, -1),
                           wkv[..., MLA_NOPE:].reshape(depth, MLA_KV_RANK, -1)], axis=-1).astype(BF16)
    rw = jnp.pad(router_w, ((0, 0), (0, 0), (0, LANES - N_EXPERTS))).astype(BF16)
    rb = jnp.concatenate([router_b, jnp.full((depth, LANES - N_EXPERTS), NEG_BIG, F32)], axis=-1)
    return dict(
        w_in=w_in.astype(BF16), w_tail=w_tail, wq=wq, wkv=wkv, w_out=w_out.astype(BF16),
        rw=rw, rb=rb.reshape(depth, 1, LANES),
        w_gu=exp_w_gu.astype(BF16),
        b_gu=exp_b_gu.reshape(depth, 1, N_EXPERTS * 2 * EXPERT_FF),
        w_down=exp_w_down.astype(BF16).reshape(depth, N_EXPERTS * EXPERT_FF, d),
        b_down=jnp.pad(exp_b_down, ((0, 0), (0, LANES - N_EXPERTS), (0, 0))).astype(BF16))


def kernel(x, c, ctx, c_ctx, ada_w, ada_b, norm1, norm2, w_in, diff_lq1, diff_lk1, diff_lq2, diff_lk2, diff_subln, ret_decay, ret_norm, mla_q_norm, mla_kv_norm, mla_wq_b, mla_wkv_b, w_out, router_w, router_b, exp_w_gu, exp_b_gu, exp_w_down, exp_b_down, final_norm):
    batch, n, d = x.shape
    ctx_len = ctx.shape[1]
    depth = ada_w.shape[0]
    s = ctx_len + n
    r = batch * s
    assert d == D_MODEL and batch <= CTX_MOD_ROW and ctx_len % RET_CHUNK == 0 and n % RET_CHUNK == 0
    assert depth >= 1 and RET_CHUNK % ROW_TILE == 0

    tq = _pick_tile(ctx_len, (256, 128))
    n_ctx_tiles = ctx_len // tq
    tm = _pick_tile(s, (768, 576, 384, 256, 128), at_least=ctx_len)
    tm_mla = _pick_tile(s, (384, 256, 128))
    tiles_per_batch = s // tm
    ret_chunk = _pick_tile(math.gcd(ctx_len, n), (2 * RET_CHUNK, RET_CHUNK))

    xs = jnp.concatenate([ctx, x], axis=1)
    mods = _ada_tables(_stack_conditioning(c, c_ctx), ada_w, ada_b)
    tabs = _rope_tables(n, ctx_len)
    log_gamma = jnp.log1p(-jnp.exp2(-ret_decay.astype(F32)))

    wts = _prep_weights(w_in, mla_wq_b, mla_wkv_b, w_out, router_w, router_b,
                        exp_w_gu, exp_b_gu, exp_w_down, exp_b_down)
    q_norm = mla_q_norm.reshape(depth, 1, -1)
    kv_norm = mla_kv_norm.reshape(depth, 1, -1)

    n_row_tiles = s // ROW_TILE
    n_ctx_row_tiles = ctx_len // ROW_TILE
    x2d = xs.reshape(r, d)
    h2d = _norm_call(xs, norm1[0].reshape(1, d), mods[0], 0, 1, n_ctx_tiles, tq).reshape(r, d)

    for l in range(depth):
        lam_init = 0.8 - 0.6 * math.exp(-0.3 * l)
        mods_l = mods[l]
        p2d = _proj_call(h2d, wts["w_in"], l, tabs, tm, tiles_per_batch)
        p3d = p2d.reshape(batch, s, MAIN_WIDTH)
        qc = _mla_q_call(p2d, q_norm, wts["wq"], l, tabs, tm_mla, s // tm_mla)
        kc, vc = _mla_kv_call(h2d, wts["w_tail"], kv_norm, wts["wkv"], l, tabs, tm_mla, s // tm_mla)
        lamv = jnp.stack([diff_lq1[l], diff_lk1[l], diff_lq2[l], diff_lk2[l]]).astype(F32)
        a = _diff_attn_call(p3d, lamv, diff_subln[l].reshape(1, -1), lam_init, tq, n_ctx_tiles, ctx_len)
        bb = _ret_call(p3d, log_gamma[l], ret_norm[l].reshape(1, -1), ret_chunk, ctx_len // ret_chunk)
        cc = _mla_attn_call(qc.reshape(batch, s, -1), kc.reshape(batch, s, -1), vc.reshape(batch, s, -1),
                            tq, n_ctx_tiles, ctx_len)
        x2d, h2 = _out_proj_call(a.reshape(r, -1), bb.reshape(r, -1), cc.reshape(r, -1), wts["w_out"], l, x2d,
                                 mods_l, norm2[l].reshape(1, d), n_row_tiles, n_ctx_row_tiles)
        act, gates2d = _moe_up_call(h2, wts["w_gu"], wts["b_gu"], wts["rw"], wts["rb"], l, tm)
        if l + 1 < depth:
            x2d, h2d = _moe_down_call(act, wts["w_down"], gates2d, wts["b_down"], l, x2d, mods_l,
                                      norm1[l + 1].reshape(1, d), mods[l + 1], n_row_tiles, n_ctx_row_tiles)
        else:
            return _moe_down_call(act, wts["w_down"], gates2d, wts["b_down"], l, x2d, mods_l,
                                  final_norm.reshape(1, d), None, n_row_tiles, n_ctx_row_tiles, out_batch=batch)
```

```python
import functools
import math

import jax
import jax.numpy as jnp
from jax import lax
from jax.experimental import pallas as pl
from jax.experimental.pallas import tpu as pltpu

F32 = jnp.float32
BF16 = jnp.bfloat16

D_MODEL = 4096
GRID_W = 64
RET_CHUNK = 128
ROPE_BASE = 10000.0
EPS = 1e-6
HEAD_DIM = 128
DIFF_HEADS = 8
DIFF_HALF = 64
DIFF_WIDTH = 1024
RET_HEADS = 8
RET_DK = 128
RET_WIDTH = 1024
MLA_HEADS = 16
MLA_Q_RANK = 1024
MLA_KV_RANK = 512
MLA_NOPE = 128
MLA_ROPE = 64
MLA_V = 128
MLA_WIDTH = 2048
MLA_QK_PAD = 256
N_EXPERTS = 32
TOP_K = 4
EXPERT_FF = 128
SWIGLU_LIMIT = 7.0
SWIGLU_ALPHA = 1.702
MAIN_WIDTH = 8192
TAIL_WIDTH = 640
SEG = 1024

LANES = 128
VMEM_LIMIT_BYTES = 56 * 1024 * 1024
NEG_BIG = -1e30
CTX_MOD_ROW = 8
MOD_ROWS = 16


def _cparams(sem):
    return pltpu.CompilerParams(dimension_semantics=sem, vmem_limit_bytes=VMEM_LIMIT_BYTES)


def _pick_tile(total, candidates, at_least=1):
    for c in candidates:
        if total % c == 0 and c >= at_least:
            return c
    raise ValueError(f"no tile for {total}")


def _ada_kernel(c_ref, w_ref, b_ref, o_ref):
    c = c_ref[...]
    cs = (c * jax.nn.sigmoid(c)).astype(BF16)
    o_ref[...] = jnp.dot(cs, w_ref[...].astype(BF16), preferred_element_type=F32) + b_ref[...]


def _stack_conditioning(c, c_ctx):
    batch, d = c.shape
    return jnp.concatenate([c, jnp.zeros((CTX_MOD_ROW - batch, d), F32), c_ctx[None, :],
                            jnp.zeros((MOD_ROWS - CTX_MOD_ROW - 1, d), F32)], axis=0)


def _ada_tables(c_all, ada_w, ada_b):
    depth, d, n6 = ada_w.shape
    tn = 512
    return pl.pallas_call(
        _ada_kernel,
        grid=(depth, n6 // tn),
        in_specs=[
            pl.BlockSpec((MOD_ROWS, d), lambda l, j: (0, 0)),
            pl.BlockSpec((None, d, tn), lambda l, j: (l, 0, j)),
            pl.BlockSpec((None, 1, tn), lambda l, j: (l, 0, j)),
        ],
        out_specs=pl.BlockSpec((None, MOD_ROWS, tn), lambda l, j: (l, 0, j)),
        out_shape=jax.ShapeDtypeStruct((depth, MOD_ROWS, n6), F32),
        compiler_params=_cparams(("arbitrary", "arbitrary")),
        name="ada_tables",
    )(c_all, ada_w, ada_b.reshape(depth, 1, n6))


def _mod_norm(x, gain, shift, scale):
    ms = jnp.mean(x * x, axis=-1, keepdims=True)
    return x * lax.rsqrt(ms + EPS) * gain * (1.0 + scale) + shift


def _norm_kernel(ctx_ref, x_ref, g_ref, sh_ref, sc_ref, xs_ref, o_ref, *, n_ctx_tiles):
    def emit(src_ref, row):
        v = src_ref[...]
        xs_ref[...] = v
        o_ref[...] = _mod_norm(v, g_ref[...], sh_ref[pl.ds(row, 1), :], sc_ref[pl.ds(row, 1), :]).astype(BF16)

    is_ctx = pl.program_id(1) < n_ctx_tiles

    @pl.when(is_ctx)
    def _():
        emit(ctx_ref, CTX_MOD_ROW)

    @pl.when(jnp.logical_not(is_ctx))
    def _():
        emit(x_ref, pl.program_id(0))


def _router_gates(h, rw_ref, rb_ref):
    logits = jnp.dot(h, rw_ref[...], preferred_element_type=F32) + rb_ref[...]
    lane = lax.broadcasted_iota(jnp.int32, logits.shape, 1)
    work = logits
    num = jnp.zeros_like(logits)
    den = jnp.zeros((logits.shape[0], 1), F32)
    m0 = None
    for k in range(TOP_K):
        m = jnp.max(work, axis=-1, keepdims=True)
        idx = jnp.min(jnp.where(work == m, lane, LANES), axis=-1, keepdims=True)
        pick = lane == idx
        if k == 0:
            m0 = m
        e = jnp.exp(m - m0)
        num = num + jnp.where(pick, e, 0.0)
        den = den + e
        work = jnp.where(pick, -jnp.inf, work)
    return num / den


def _norm_call(ctx, x, gain, mods_l, shift_blk, scale_blk, n_ctx_tiles, tr):
    b, n, d = x.shape
    s = ctx.shape[1] + n
    ctx_spec = pl.BlockSpec((None, tr, d), lambda bi, i: (bi, jnp.minimum(i, n_ctx_tiles - 1), 0))
    x_spec = pl.BlockSpec((None, tr, d), lambda bi, i: (bi, jnp.maximum(i - n_ctx_tiles, 0), 0))
    out_spec = pl.BlockSpec((None, tr, d), lambda bi, i: (bi, i, 0))
    return pl.pallas_call(
        functools.partial(_norm_kernel, n_ctx_tiles=n_ctx_tiles),
        grid=(b, s // tr),
        in_specs=[ctx_spec, x_spec,
                  pl.BlockSpec((1, d), lambda bi, i: (0, 0)),
                  pl.BlockSpec((MOD_ROWS, d), lambda bi, i: (0, shift_blk)),
                  pl.BlockSpec((MOD_ROWS, d), lambda bi, i: (0, scale_blk))],
        out_specs=[out_spec, out_spec],
        out_shape=[jax.ShapeDtypeStruct((b, s, d), F32), jax.ShapeDtypeStruct((b, s, d), BF16)],
        compiler_params=_cparams(("parallel", "arbitrary")),
        name="mod_norm",
    )(ctx, x, gain, mods_l, mods_l)


def _rope64(blk, c, sa, sb):
    return blk * c + pltpu.roll(blk, 96, 1) * sa + pltpu.roll(blk, 32, 1) * sb


PROJ_SPLIT = 4


def _proj_kernel(h_ref, w_ref, c64_ref, sa64_ref, sb64_ref, c128_ref, s128_ref, o_ref):
    j = pl.program_id(1)
    h = h_ref[...]
    part = w_ref.shape[1] // PROJ_SPLIT
    n_blk = part // LANES

    def parts():
        for p in range(PROJ_SPLIT):
            acc = jnp.dot(h, w_ref[:, p * part:(p + 1) * part], preferred_element_type=F32)
            for t in range(n_blk):
                yield slice(p * part + t * LANES, p * part + (t + 1) * LANES), acc[:, t * LANES:(t + 1) * LANES]

    @pl.when((j == 0) | (j == 1))
    def _():
        scale = jnp.where(j == 0, DIFF_HALF ** -0.5, 1.0).astype(F32)
        c, sa, sb = c64_ref[...] * scale, sa64_ref[...] * scale, sb64_ref[...] * scale
        for sl, blk in parts():
            o_ref[:, sl] = _rope64(blk, c, sa, sb).astype(BF16)

    @pl.when((j == 3) | (j == 4))
    def _():
        scale = jnp.where(j == 4, RET_DK ** -0.5, 1.0).astype(F32)
        c, s = c128_ref[...] * scale, s128_ref[...] * scale
        for sl, blk in parts():
            o_ref[:, sl] = (blk * c + pltpu.roll(blk, 64, 1) * s).astype(BF16)

    @pl.when((j == 2) | (j >= 5))
    def _():
        for sl, blk in parts():
            o_ref[:, sl] = blk.astype(BF16)


def _proj_call(h2d, w_in_bf, l, tabs, tm, tiles_per_batch):
    r, d = h2d.shape
    tab_spec = pl.BlockSpec((tm, LANES), lambda i, j: (i % tiles_per_batch, 0))
    return pl.pallas_call(
        _proj_kernel,
        grid=(r // tm, MAIN_WIDTH // SEG),
        in_specs=[pl.BlockSpec((tm, d), lambda i, j: (i, 0)),
                  pl.BlockSpec((None, d, SEG), lambda i, j: (l, 0, j)),
                  tab_spec, tab_spec, tab_spec, tab_spec, tab_spec],
        out_specs=pl.BlockSpec((tm, SEG), lambda i, j: (i, j)),
        out_shape=jax.ShapeDtypeStruct((r, MAIN_WIDTH), BF16),
        compiler_params=_cparams(("parallel", "arbitrary")),
        name="in_proj",
    )(h2d, w_in_bf, tabs["c64"], tabs["sa64"], tabs["sb64"], tabs["c128"], tabs["s128"])


def _rms_gain(x, g_ref):
    ms = jnp.mean(x * x, axis=-1, keepdims=True)
    return x * lax.rsqrt(ms + EPS) * g_ref[...]


def _mla_q_kernel(qa_ref, g_ref, w_ref, c_ref, sa_ref, sb_ref, o_ref):
    qn = _rms_gain(qa_ref[...].astype(F32), g_ref).astype(BF16)
    q = jnp.dot(qn, w_ref[...], preferred_element_type=F32)
    scale = (MLA_NOPE + MLA_ROPE) ** -0.5
    c, sa, sb = c_ref[...] * scale, sa_ref[...] * scale, sb_ref[...] * scale
    for h in range(MLA_HEADS):
        lo = h * MLA_QK_PAD
        o_ref[:, lo:lo + LANES] = (q[:, lo:lo + LANES] * scale).astype(BF16)
        o_ref[:, lo + LANES:lo + 2 * LANES] = _rope64(q[:, lo + LANES:lo + 2 * LANES], c, sa, sb).astype(BF16)


def _mla_kv_kernel(h_ref, wt_ref, g_ref, w_ref, c_ref, sa_ref, sb_ref, k_ref, v_ref):
    t = jnp.dot(h_ref[...], wt_ref[...], preferred_element_type=F32)
    kvn = _rms_gain(t[:, :MLA_KV_RANK], g_ref).astype(BF16)
    kr = _rope64(t[:, MLA_KV_RANK:MLA_KV_RANK + LANES], c_ref[...], sa_ref[...], sb_ref[...]).astype(BF16)
    kv = jnp.dot(kvn, w_ref[...], preferred_element_type=F32)
    kn_width = MLA_HEADS * MLA_NOPE
    for h in range(MLA_HEADS):
        lo = h * MLA_QK_PAD
        k_ref[:, lo:lo + LANES] = kv[:, h * MLA_NOPE:(h + 1) * MLA_NOPE].astype(BF16)
        k_ref[:, lo + LANES:lo + 2 * LANES] = kr
    v_ref[...] = kv[:, kn_width:].astype(BF16)


def _mla_q_call(p2d, q_norm, wq, l, tabs, tm, tiles_per_batch):
    r = p2d.shape[0]
    tab_spec = pl.BlockSpec((tm, LANES), lambda i: (i % tiles_per_batch, 0))
    width = MLA_HEADS * MLA_QK_PAD
    return pl.pallas_call(
        _mla_q_kernel,
        grid=(r // tm,),
        in_specs=[pl.BlockSpec((tm, MLA_Q_RANK), lambda i: (i, (MAIN_WIDTH - MLA_Q_RANK) // MLA_Q_RANK)),
                  pl.BlockSpec((None, 1, MLA_Q_RANK), lambda i: (l, 0, 0)),
                  pl.BlockSpec((None, MLA_Q_RANK, width), lambda i: (l, 0, 0)),
                  tab_spec, tab_spec, tab_spec],
        out_specs=pl.BlockSpec((tm, width), lambda i: (i, 0)),
        out_shape=jax.ShapeDtypeStruct((r, width), BF16),
        compiler_params=_cparams(("parallel",)),
        name="mla_q_proj",
    )(p2d, q_norm, wq, tabs["c64r"], tabs["sa64r"], tabs["sb64r"])


def _mla_kv_call(h2d, w_tail, kv_norm, wkv, l, tabs, tm, tiles_per_batch):
    r, d = h2d.shape
    tab_spec = pl.BlockSpec((tm, LANES), lambda i: (i % tiles_per_batch, 0))
    kwidth = MLA_HEADS * MLA_QK_PAD
    return pl.pallas_call(
        _mla_kv_kernel,
        grid=(r // tm,),
        in_specs=[pl.BlockSpec((tm, d), lambda i: (i, 0)),
                  pl.BlockSpec((None, d, TAIL_WIDTH), lambda i: (l, 0, 0)),
                  pl.BlockSpec((None, 1, MLA_KV_RANK), lambda i: (l, 0, 0)),
                  pl.BlockSpec((None, MLA_KV_RANK, 2 * MLA_WIDTH), lambda i: (l, 0, 0)),
                  tab_spec, tab_spec, tab_spec],
        out_specs=[pl.BlockSpec((tm, kwidth), lambda i: (i, 0)),
                   pl.BlockSpec((tm, MLA_WIDTH), lambda i: (i, 0))],
        out_shape=[jax.ShapeDtypeStruct((r, kwidth), BF16),
                   jax.ShapeDtypeStruct((r, MLA_WIDTH), BF16)],
        compiler_params=_cparams(("parallel",)),
        name="mla_kv_proj",
    )(h2d, w_tail, kv_norm, wkv, tabs["c64r"], tabs["sa64r"], tabs["sb64r"])


KEY_CHUNK = 768


def _softmax_pv(q, k_ref, vo_ref, m_keys, k_cols, v_cols):
    chunk = _pick_tile(m_keys, (KEY_CHUNK, 512, 384, 256, 128))
    m_run = None
    acc = None
    for c0 in range(0, m_keys, chunk):
        s = lax.dot_general(q, k_ref[c0:c0 + chunk, k_cols], _NT, preferred_element_type=F32)
        m_c = jnp.max(s, axis=-1, keepdims=True)
        v_ones = vo_ref[c0:c0 + chunk, v_cols]
        if m_run is None:
            m_run = m_c
            acc = jnp.dot(jnp.exp((s - m_run).astype(BF16)), v_ones, preferred_element_type=F32)
        else:
            m_new = jnp.maximum(m_run, m_c)
            acc = (acc * jnp.exp(m_run - m_new)
                   + jnp.dot(jnp.exp((s - m_new).astype(BF16)), v_ones, preferred_element_type=F32))
            m_run = m_new
    half = acc.shape[1] // 2
    return acc[:, :half] / acc[:, half:]


def _fill_v_ones(v_ref, vo_ref, n_heads, width):
    ones = jnp.ones((v_ref.shape[0], width), BF16)
    for h in range(n_heads):
        vo_ref[:, 2 * width * h:2 * width * h + width] = v_ref[:, width * h:width * (h + 1)]
        vo_ref[:, 2 * width * h + width:2 * width * (h + 1)] = ones


_NT = (((1,), (1,)), ((), ()))


def _diff_attn_kernel(lam_ref, sub_ref, q_ref, k_ref, v_ref, o_ref, vo_ref, *, n_ctx_tiles, ctx_len, lam_init):
    @pl.when(pl.program_id(1) == 0)
    def _():
        _fill_v_ones(v_ref, vo_ref, DIFF_HEADS, HEAD_DIM)

    lv = lam_ref[...]
    lam = (jnp.exp(jnp.sum(lv[0:1] * lv[1:2], axis=-1, keepdims=True))
           - jnp.exp(jnp.sum(lv[2:3] * lv[3:4], axis=-1, keepdims=True)) + lam_init)
    sub = sub_ref[...] * (1.0 - lam_init)

    def run(m_keys):
        for h in range(DIFF_HEADS):
            sl = slice(h * HEAD_DIM, (h + 1) * HEAD_DIM)
            q = q_ref[:, sl]
            v_cols = slice(2 * h * HEAD_DIM, 2 * (h + 1) * HEAD_DIM)
            first = lax.broadcasted_iota(jnp.int32, q.shape, 1) < DIFF_HALF
            zero = jnp.zeros_like(q)
            o = (_softmax_pv(jnp.where(first, q, zero), k_ref, vo_ref, m_keys, sl, v_cols)
                 - lam * _softmax_pv(jnp.where(first, zero, q), k_ref, vo_ref, m_keys, sl, v_cols))
            ms = jnp.mean(o * o, axis=-1, keepdims=True)
            o_ref[:, sl] = (o * lax.rsqrt(ms + EPS) * sub).astype(BF16)

    is_ctx = pl.program_id(1) < n_ctx_tiles

    @pl.when(is_ctx)
    def _():
        run(ctx_len)

    @pl.when(jnp.logical_not(is_ctx))
    def _():
        run(k_ref.shape[0])


def _diff_attn_call(p3d, lamv, subln, lam_init, tq, n_ctx_tiles, ctx_len):
    b, s, _ = p3d.shape
    return pl.pallas_call(
        functools.partial(_diff_attn_kernel, n_ctx_tiles=n_ctx_tiles, ctx_len=ctx_len, lam_init=lam_init),
        grid=(b, s // tq),
        in_specs=[pl.BlockSpec((4, DIFF_HALF), lambda bi, i: (0, 0)),
                  pl.BlockSpec((1, HEAD_DIM), lambda bi, i: (0, 0)),
                  pl.BlockSpec((None, tq, DIFF_WIDTH), lambda bi, i: (bi, i, 0)),
                  pl.BlockSpec((None, s, DIFF_WIDTH), lambda bi, i: (bi, 0, 1)),
                  pl.BlockSpec((None, s, DIFF_WIDTH), lambda bi, i: (bi, 0, 2))],
        out_specs=pl.BlockSpec((None, tq, DIFF_WIDTH), lambda bi, i: (bi, i, 0)),
        out_shape=jax.ShapeDtypeStruct((b, s, DIFF_WIDTH), BF16),
        scratch_shapes=[pltpu.VMEM((s, 2 * DIFF_WIDTH), BF16)],
        compiler_params=_cparams(("parallel", "arbitrary")),
        name="diff_attn",
    )(lamv, subln, p3d, p3d, p3d)


MLA_HEAD_GROUP = 8


def _mla_attn_kernel(q_ref, k_ref, v_ref, o_ref, vo_ref, *, n_ctx_tiles, ctx_len):
    @pl.when(pl.program_id(2) == 0)
    def _():
        _fill_v_ones(v_ref, vo_ref, MLA_HEAD_GROUP, MLA_V)

    def run(m_keys):
        for h in range(MLA_HEAD_GROUP):
            qk = slice(h * MLA_QK_PAD, (h + 1) * MLA_QK_PAD)
            vv = slice(h * MLA_V, (h + 1) * MLA_V)
            v_cols = slice(2 * h * MLA_V, 2 * (h + 1) * MLA_V)
            o_ref[:, vv] = _softmax_pv(q_ref[:, qk], k_ref, vo_ref, m_keys, qk, v_cols).astype(BF16)

    is_ctx = pl.program_id(2) < n_ctx_tiles

    @pl.when(is_ctx)
    def _():
        run(ctx_len)

    @pl.when(jnp.logical_not(is_ctx))
    def _():
        run(k_ref.shape[0])


def _mla_attn_call(qc, kc, vc, tq, n_ctx_tiles, ctx_len):
    b, s, _ = qc.shape
    qw = MLA_HEAD_GROUP * MLA_QK_PAD
    vw = MLA_HEAD_GROUP * MLA_V
    return pl.pallas_call(
        functools.partial(_mla_attn_kernel, n_ctx_tiles=n_ctx_tiles, ctx_len=ctx_len),
        grid=(b, MLA_HEADS // MLA_HEAD_GROUP, s // tq),
        in_specs=[pl.BlockSpec((None, tq, qw), lambda bi, g, i: (bi, i, g)),
                  pl.BlockSpec((None, s, qw), lambda bi, g, i: (bi, 0, g)),
                  pl.BlockSpec((None, s, vw), lambda bi, g, i: (bi, 0, g))],
        out_specs=pl.BlockSpec((None, tq, vw), lambda bi, g, i: (bi, i, g)),
        out_shape=jax.ShapeDtypeStruct((b, s, MLA_WIDTH), BF16),
        scratch_shapes=[pltpu.VMEM((s, 2 * vw), BF16)],
        compiler_params=_cparams(("parallel", "parallel", "arbitrary")),
        name="mla_attn",
    )(qc, kc, vc)


RET_HEAD_GROUP = 4
_TN = (((0,), (0,)), ((), ()))


def _ret_kernel(lg_ref, q_ref, k_ref, v_ref, g_ref, gn_ref, o_ref, intra_ref, rowdec_ref, cdec_ref, st_ref, acc_ref,
                *, chunk, n_ctx_chunks):
    c = chunk
    n_chunks = q_ref.shape[0] // c
    hp = pl.program_id(1)
    row = lax.broadcasted_iota(jnp.int32, (c, c), 0).astype(F32)
    col = lax.broadcasted_iota(jnp.int32, (c, c), 1).astype(F32)
    rel = row - col
    pos = lax.broadcasted_iota(jnp.int32, (c, RET_DK), 0).astype(F32)
    span = jnp.full((RET_DK, RET_DK), float(c), F32)

    for hh in range(RET_HEAD_GROUP):
        lgf = lg_ref[0, hp * RET_HEAD_GROUP + hh]
        lgb = lg_ref[1, hp * RET_HEAD_GROUP + hh]
        intra_ref[hh, 0] = jnp.where(rel >= 0, jnp.exp(jnp.maximum(rel, 0.0) * lgf), 0.0)
        intra_ref[hh, 1] = jnp.where(rel <= 0, jnp.exp(jnp.maximum(-rel, 0.0) * lgb), 0.0)
        rowdec_ref[hh, 0] = jnp.exp((pos + 1.0) * lgf)
        rowdec_ref[hh, 1] = jnp.exp((c - 1.0 - pos) * lgf)
        rowdec_ref[hh, 2] = jnp.exp((c - pos) * lgb)
        rowdec_ref[hh, 3] = jnp.exp(pos * lgb)
        cdec_ref[hh, 0] = jnp.exp(span * lgf)
        cdec_ref[hh, 1] = jnp.exp(span * lgb)

    st_ref[...] = jnp.zeros_like(st_ref)

    def chunk_update(hh, direction, idx):
        sl = slice(hh * RET_DK, (hh + 1) * RET_DK)
        rows = pl.ds(pl.multiple_of(idx * c, c), c)
        qc = q_ref[rows, sl]
        kc = k_ref[rows, sl]
        vc = v_ref[rows, sl]
        state = st_ref[hh, direction]
        att = lax.dot_general(qc, kc, _NT, preferred_element_type=F32) * intra_ref[hh, direction]
        o = (jnp.dot(att.astype(BF16), vc, preferred_element_type=F32)
             + jnp.dot(qc, state.astype(BF16), preferred_element_type=F32) * rowdec_ref[hh, 2 * direction])
        kd = (kc.astype(F32) * rowdec_ref[hh, 2 * direction + 1]).astype(BF16)
        st_ref[hh, direction] = (state * cdec_ref[hh, direction]
                                 + lax.dot_general(kd, vc, _TN, preferred_element_type=F32))
        acc_ref[direction, rows, sl] = o

    def step(t, carry):
        bwd = jnp.where(t < n_ctx_chunks, n_ctx_chunks - 1 - t, n_chunks - 1 - t + n_ctx_chunks)
        for hh in range(RET_HEAD_GROUP):
            chunk_update(hh, 0, t)
            chunk_update(hh, 1, bwd)
        return carry

    lax.fori_loop(0, n_chunks, step, 0)

    for hh in range(RET_HEAD_GROUP):
        sl = slice(hh * RET_DK, (hh + 1) * RET_DK)
        o = acc_ref[0, :, sl] + acc_ref[1, :, sl]
        mu = jnp.mean(o, axis=-1, keepdims=True)
        oc = o - mu
        y = oc * lax.rsqrt(jnp.mean(oc * oc, axis=-1, keepdims=True) + 1e-5) * gn_ref[:, sl]
        g = g_ref[:, sl].astype(F32)
        o_ref[:, sl] = (y * (g * jax.nn.sigmoid(g))).astype(BF16)


def _ret_call(p3d, log_gamma, gn_gain, chunk, n_ctx_chunks):
    b, s, _ = p3d.shape
    w = RET_HEAD_GROUP * RET_DK
    base = 3 * SEG // w

    def col(seg):
        return lambda bi, hp, lg: (bi, 0, base + seg * (SEG // w) + hp)

    grid_spec = pltpu.PrefetchScalarGridSpec(
        num_scalar_prefetch=1,
        grid=(b, RET_HEADS // RET_HEAD_GROUP),
        in_specs=[pl.BlockSpec((None, s, w), col(0)),
                  pl.BlockSpec((None, s, w), col(1)),
                  pl.BlockSpec((None, s, w), col(2)),
                  pl.BlockSpec((None, s, w), col(3)),
                  pl.BlockSpec((1, w), lambda bi, hp, lg: (0, hp))],
        out_specs=pl.BlockSpec((None, s, w), lambda bi, hp, lg: (bi, 0, hp)),
        scratch_shapes=[pltpu.VMEM((RET_HEAD_GROUP, 2, chunk, chunk), F32),
                        pltpu.VMEM((RET_HEAD_GROUP, 4, chunk, RET_DK), F32),
                        pltpu.VMEM((RET_HEAD_GROUP, 2, RET_DK, RET_DK), F32),
                        pltpu.VMEM((RET_HEAD_GROUP, 2, RET_DK, RET_DK), F32),
                        pltpu.VMEM((2, s, w), F32)],
    )
    return pl.pallas_call(
        functools.partial(_ret_kernel, chunk=chunk, n_ctx_chunks=n_ctx_chunks),
        grid_spec=grid_spec,
        out_shape=jax.ShapeDtypeStruct((b, s, RET_WIDTH), BF16),
        compiler_params=_cparams(("parallel", "arbitrary")),
        name="retention",
    )(log_gamma, p3d, p3d, p3d, p3d, gn_gain)


ROW_TILE = 128


def _tile_mod_row(tiles_per_batch, n_ctx_tiles):
    i = pl.program_id(0)
    return jnp.where(i % tiles_per_batch < n_ctx_tiles, CTX_MOD_ROW, i // tiles_per_batch)


def _resident(shape, index_map):
    return pl.BlockSpec(shape, index_map, pipeline_mode=pl.Buffered(1))


def _out_proj_kernel(a_ref, b_ref, c_ref, w_ref, x_ref, g1_ref, gain_ref, sh_ref, sc_ref, xo_ref, h_ref,
                     *, tiles_per_batch, n_ctx_tiles):
    row = pl.ds(_tile_mod_row(tiles_per_batch, n_ctx_tiles), 1)
    acc = jnp.dot(a_ref[...], w_ref[0:DIFF_WIDTH, :], preferred_element_type=F32)
    acc += jnp.dot(b_ref[...], w_ref[DIFF_WIDTH:DIFF_WIDTH + RET_WIDTH, :], preferred_element_type=F32)
    acc += jnp.dot(c_ref[...], w_ref[DIFF_WIDTH + RET_WIDTH:, :], preferred_element_type=F32)
    x_new = x_ref[...] + g1_ref[row, :] * acc
    xo_ref[...] = x_new
    h_ref[...] = _mod_norm(x_new, gain_ref[...], sh_ref[row, :], sc_ref[row, :]).astype(BF16)


def _out_proj_call(a2d, b2d, c2d, w_out, l, x2d, mods_l, gain, tiles_per_batch, n_ctx_tiles):
    r, d = x2d.shape
    tr = ROW_TILE
    rows = lambda w: pl.BlockSpec((tr, w), lambda i: (i, 0))
    mod = lambda blk: pl.BlockSpec((MOD_ROWS, d), lambda i: (0, blk))
    return pl.pallas_call(
        functools.partial(_out_proj_kernel, tiles_per_batch=tiles_per_batch, n_ctx_tiles=n_ctx_tiles),
        grid=(r // tr,),
        in_specs=[rows(DIFF_WIDTH), rows(RET_WIDTH), rows(MLA_WIDTH),
                  _resident((None, d, d), lambda i: (l, 0, 0)),
                  rows(d), mod(2),
                  pl.BlockSpec((1, d), lambda i: (0, 0)), mod(3), mod(4)],
        out_specs=[rows(d), rows(d)],
        out_shape=[jax.ShapeDtypeStruct((r, d), F32), jax.ShapeDtypeStruct((r, d), BF16)],
        compiler_params=_cparams(("arbitrary",)),
        name="out_proj",
    )(a2d, b2d, c2d, w_out, x2d, mods_l, gain, mods_l, mods_l)


def _moe_up_kernel(h_ref, w_ref, b_ref, rw_ref, rb_ref, o_ref, gate_out_ref, gate_ref):
    j = pl.program_id(1)
    n_e = w_ref.shape[0]

    @pl.when(j == 0)
    def _():
        gates = _router_gates(h_ref[...], rw_ref, rb_ref)
        gate_ref[...] = gates
        gate_out_ref[...] = gates

    e_row = lax.broadcasted_iota(jnp.int32, (LANES, n_e * EXPERT_FF), 0)
    e_col = lax.broadcasted_iota(jnp.int32, (LANES, n_e * EXPERT_FF), 1) // EXPERT_FF
    expand = (e_row == j * n_e + e_col).astype(BF16)
    ge = jnp.dot(gate_ref[...].astype(BF16), expand, preferred_element_type=F32)
    h = h_ref[...]
    for e in range(n_e):
        lo = e * 2 * EXPERT_FF
        gu = jnp.dot(h, w_ref[e], preferred_element_type=F32) + b_ref[:, lo:lo + 2 * EXPERT_FF]
        glu = jnp.minimum(gu[:, :EXPERT_FF], SWIGLU_LIMIT)
        lin = jnp.clip(gu[:, EXPERT_FF:], -SWIGLU_LIMIT, SWIGLU_LIMIT)
        act = glu * jax.nn.sigmoid(SWIGLU_ALPHA * glu) * (lin + 1.0)
        sl = slice(e * EXPERT_FF, (e + 1) * EXPERT_FF)
        o_ref[:, sl] = (act * ge[:, sl]).astype(BF16)


def _moe_up_call(h2d, w_gu, b_gu, rw, rb, l, tm):
    r, d = h2d.shape
    tn = SEG
    n_e = tn // (2 * EXPERT_FF)
    n = N_EXPERTS * 2 * EXPERT_FF
    return pl.pallas_call(
        _moe_up_kernel,
        grid=(r // tm, n // tn),
        in_specs=[pl.BlockSpec((tm, d), lambda i, j: (i, 0)),
                  pl.BlockSpec((None, n_e, d, 2 * EXPERT_FF), lambda i, j: (l, j, 0, 0)),
                  pl.BlockSpec((None, 1, tn), lambda i, j: (l, 0, j)),
                  pl.BlockSpec((None, d, LANES), lambda i, j: (l, 0, 0)),
                  pl.BlockSpec((None, 1, LANES), lambda i, j: (l, 0, 0))],
        out_specs=[pl.BlockSpec((tm, tn // 2), lambda i, j: (i, j)),
                   pl.BlockSpec((tm, LANES), lambda i, j: (i, 0))],
        out_shape=[jax.ShapeDtypeStruct((r, n // 2), BF16), jax.ShapeDtypeStruct((r, LANES), F32)],
        scratch_shapes=[pltpu.VMEM((tm, LANES), F32)],
        compiler_params=_cparams(("parallel", "arbitrary")),
        name="moe_up",
    )(h2d, w_gu, b_gu, rw, rb)


def _moe_down_rows(a_ref, w_ref, gates_ref, bd_ref, x_ref, g2_ref, row):
    acc = jnp.dot(a_ref[...], w_ref[...], preferred_element_type=F32)
    acc += jnp.dot(gates_ref[...].astype(BF16), bd_ref[...], preferred_element_type=F32)
    return x_ref[...] + g2_ref[row, :] * acc


def _moe_down_kernel(a_ref, w_ref, gates_ref, bd_ref, x_ref, g2_ref, gain_ref, sh_ref, sc_ref, xo_ref, h_ref,
                     *, tiles_per_batch, n_ctx_tiles):
    row = pl.ds(_tile_mod_row(tiles_per_batch, n_ctx_tiles), 1)
    x_new = _moe_down_rows(a_ref, w_ref, gates_ref, bd_ref, x_ref, g2_ref, row)
    xo_ref[...] = x_new
    h_ref[...] = _mod_norm(x_new, gain_ref[...], sh_ref[row, :], sc_ref[row, :]).astype(BF16)


def _moe_down_final_kernel(a_ref, w_ref, gates_ref, bd_ref, x_ref, g2_ref, gain_ref, o_ref,
                           *, tiles_per_batch, n_ctx_tiles):
    row = pl.ds(_tile_mod_row(tiles_per_batch, n_ctx_tiles), 1)
    x_new = _moe_down_rows(a_ref, w_ref, gates_ref, bd_ref, x_ref, g2_ref, row)
    ms = jnp.mean(x_new * x_new, axis=-1, keepdims=True)
    o_ref[...] = x_new * lax.rsqrt(ms + EPS) * gain_ref[...]


def _moe_down_call(a2d, w_down, gates2d, b_down, l, x2d, mods_l, gain, mods_next, tiles_per_batch, n_ctx_tiles,
                   out_batch=None):
    r, d = x2d.shape
    k = a2d.shape[1]
    tr = ROW_TILE
    rows = lambda w: pl.BlockSpec((tr, w), lambda i: (i, 0))
    mod = lambda blk: pl.BlockSpec((MOD_ROWS, d), lambda i: (0, blk))
    in_specs = [rows(k), _resident((None, k, d), lambda i: (l, 0, 0)), rows(LANES),
                _resident((None, LANES, d), lambda i: (l, 0, 0)), rows(d), mod(5),
                pl.BlockSpec((1, d), lambda i: (0, 0))]
    static = dict(tiles_per_batch=tiles_per_batch, n_ctx_tiles=n_ctx_tiles)
    if out_batch is None:
        return pl.pallas_call(
            functools.partial(_moe_down_kernel, **static),
            grid=(r // tr,),
            in_specs=in_specs + [mod(0), mod(1)],
            out_specs=[rows(d), rows(d)],
            out_shape=[jax.ShapeDtypeStruct((r, d), F32), jax.ShapeDtypeStruct((r, d), BF16)],
            compiler_params=_cparams(("arbitrary",)),
            name="moe_down",
        )(a2d, w_down, gates2d, b_down, x2d, mods_l, gain, mods_next, mods_next)
    latent_tiles = tiles_per_batch - n_ctx_tiles
    return pl.pallas_call(
        functools.partial(_moe_down_final_kernel, **static),
        grid=(r // tr,),
        in_specs=in_specs,
        out_specs=pl.BlockSpec((None, tr, d), lambda i: (i // tiles_per_batch,
                                                          jnp.maximum(i % tiles_per_batch - n_ctx_tiles, 0), 0)),
        out_shape=jax.ShapeDtypeStruct((out_batch, latent_tiles * tr, d), F32),
        compiler_params=_cparams(("arbitrary",)),
        name="moe_down_final",
    )(a2d, w_down, gates2d, b_down, x2d, mods_l, gain)


def _rope_tables(n, ctx_len):
    rows = n // GRID_W
    row = jnp.repeat(jnp.arange(rows, dtype=F32), GRID_W)
    col = jnp.tile(jnp.arange(GRID_W, dtype=F32), rows)

    def cos_sin(rot_dim):
        pairs = rot_dim // 4
        inv = ROPE_BASE ** (-jnp.arange(pairs, dtype=F32) / pairs)
        ang = jnp.concatenate([row[:, None] * inv, col[:, None] * inv], axis=-1)
        cos = jnp.concatenate([jnp.ones((ctx_len, rot_dim // 2), F32), jnp.cos(ang)], axis=0)
        sin = jnp.concatenate([jnp.zeros((ctx_len, rot_dim // 2), F32), jnp.sin(ang)], axis=0)
        return cos, sin

    cos32, sin32 = cos_sin(DIFF_HALF)
    cos64, sin64 = cos_sin(RET_DK)
    z32 = jnp.zeros_like(sin32)
    one64 = jnp.ones((cos32.shape[0], 64), F32)
    z64 = jnp.zeros_like(one64)
    return {
        "c64": jnp.concatenate([cos32] * 4, axis=-1),
        "sa64": jnp.concatenate([-sin32, z32, -sin32, z32], axis=-1),
        "sb64": jnp.concatenate([z32, sin32, z32, sin32], axis=-1),
        "c64r": jnp.concatenate([cos32, cos32, one64], axis=-1),
        "sa64r": jnp.concatenate([-sin32, z32, z64], axis=-1),
        "sb64r": jnp.concatenate([z32, sin32, z64], axis=-1),
        "c128": jnp.concatenate([cos64, cos64], axis=-1),
        "s128": jnp.concatenate([-sin64, sin64], axis=-1),
    }


def _prep_weights(w_in, mla_wq_b, mla_wkv_b, w_out, router_w, router_b,
                  exp_w_gu, exp_b_gu, exp_w_down, exp_b_down):
    depth, d, in_width = w_in.shape
    w_tail = jnp.pad(w_in[:, :, MAIN_WIDTH:], ((0, 0), (0, 0), (0, TAIL_WIDTH - (in_width - MAIN_WIDTH)))).astype(BF16)
    wq = mla_wq_b.reshape(depth, MLA_Q_RANK, MLA_HEADS, MLA_NOPE + MLA_ROPE)
    wq = jnp.pad(wq, ((0, 0), (0, 0), (0, 0), (0, MLA_QK_PAD - MLA_NOPE - MLA_ROPE)))
    wq = wq.reshape(depth, MLA_Q_RANK, MLA_HEADS * MLA_QK_PAD).astype(BF16)
    wkv = mla_wkv_b.reshape(depth, MLA_KV_RANK, MLA_HEADS, MLA_NOPE + MLA_V)
    wkv = jnp.concatenate([wkv[..., :MLA_NOPE].reshape(depth, MLA_KV_RANK, -1),
                           wkv[..., MLA_NOPE:].reshape(depth, MLA_KV_RANK, -1)], axis=-1).astype(BF16)
    rw = jnp.pad(router_w, ((0, 0), (0, 0), (0, LANES - N_EXPERTS))).astype(BF16)
    rb = jnp.concatenate([router_b, jnp.full((depth, LANES - N_EXPERTS), NEG_BIG, F32)], axis=-1)
    return dict(
        w_in=w_in[:, :, :MAIN_WIDTH].astype(BF16), w_tail=w_tail, wq=wq, wkv=wkv, w_out=w_out.astype(BF16),
        rw=rw, rb=rb.reshape(depth, 1, LANES),
        w_gu=exp_w_gu.astype(BF16),
        b_gu=exp_b_gu.reshape(depth, 1, N_EXPERTS * 2 * EXPERT_FF),
        w_down=exp_w_down.astype(BF16).reshape(depth, N_EXPERTS * EXPERT_FF, d),
        b_down=jnp.pad(exp_b_down, ((0, 0), (0, LANES - N_EXPERTS), (0, 0))).astype(BF16))


def kernel(x, c, ctx, c_ctx, ada_w, ada_b, norm1, norm2, w_in, diff_lq1, diff_lk1, diff_lq2, diff_lk2, diff_subln, ret_decay, ret_norm, mla_q_norm, mla_kv_norm, mla_wq_b, mla_wkv_b, w_out, router_w, router_b, exp_w_gu, exp_b_gu, exp_w_down, exp_b_down, final_norm):
    batch, n, d = x.shape
    ctx_len = ctx.shape[1]
    depth = ada_w.shape[0]
    s = ctx_len + n
    r = batch * s
    assert d == D_MODEL and batch <= CTX_MOD_ROW and ctx_len % RET_CHUNK == 0 and n % RET_CHUNK == 0
    assert depth >= 1 and RET_CHUNK % ROW_TILE == 0

    tq = _pick_tile(ctx_len, (256, 128))
    n_ctx_tiles = ctx_len // tq
    tm = _pick_tile(s, (768, 576, 384, 256, 128), at_least=ctx_len)
    tm_mla = _pick_tile(s, (384, 256, 128))
    tiles_per_batch = s // tm
    ret_chunk = _pick_tile(math.gcd(ctx_len, n), (2 * RET_CHUNK, RET_CHUNK))

    mods = _ada_tables(_stack_conditioning(c, c_ctx), ada_w, ada_b)
    tabs = _rope_tables(n, ctx_len)
    log_gamma = jnp.log1p(-jnp.exp2(-ret_decay.astype(F32)))

    wts = _prep_weights(w_in, mla_wq_b, mla_wkv_b, w_out, router_w, router_b,
                        exp_w_gu, exp_b_gu, exp_w_down, exp_b_down)
    q_norm = mla_q_norm.reshape(depth, 1, -1)
    kv_norm = mla_kv_norm.reshape(depth, 1, -1)

    n_row_tiles = s // ROW_TILE
    n_ctx_row_tiles = ctx_len // ROW_TILE
    xs, h = _norm_call(ctx, x, norm1[0].reshape(1, d), mods[0], 0, 1, n_ctx_tiles, tq)
    x2d = xs.reshape(r, d)
    h2d = h.reshape(r, d)

    for l in range(depth):
        lam_init = 0.8 - 0.6 * math.exp(-0.3 * l)
        mods_l = mods[l]
        p2d = _proj_call(h2d, wts["w_in"], l, tabs, tm, tiles_per_batch)
        p3d = p2d.reshape(batch, s, MAIN_WIDTH)
        qc = _mla_q_call(p2d, q_norm, wts["wq"], l, tabs, tm_mla, s // tm_mla)
        kc, vc = _mla_kv_call(h2d, wts["w_tail"], kv_norm, wts["wkv"], l, tabs, tm_mla, s // tm_mla)
        lamv = jnp.stack([diff_lq1[l], diff_lk1[l], diff_lq2[l], diff_lk2[l]]).astype(F32)
        a = _diff_attn_call(p3d, lamv, diff_subln[l].reshape(1, -1), lam_init, tq, n_ctx_tiles, ctx_len)
        bb = _ret_call(p3d, log_gamma[l], ret_norm[l].reshape(1, -1), ret_chunk, ctx_len // ret_chunk)
        cc = _mla_attn_call(qc.reshape(batch, s, -1), kc.reshape(batch, s, -1), vc.reshape(batch, s, -1),
                            tq, n_ctx_tiles, ctx_len)
        x2d, h2 = _out_proj_call(a.reshape(r, -1), bb.reshape(r, -1), cc.reshape(r, -1), wts["w_out"], l, x2d,
                                 mods_l, norm2[l].reshape(1, d), n_row_tiles, n_ctx_row_tiles)
        act, gates2d = _moe_up_call(h2, wts["w_gu"], wts["b_gu"], wts["rw"], wts["rb"], l, tm)
        if l + 1 < depth:
            x2d, h2d = _moe_down_call(act, wts["w_down"], gates2d, wts["b_down"], l, x2d, mods_l,
                                      norm1[l + 1].reshape(1, d), mods[l + 1], n_row_tiles, n_ctx_row_tiles)
        else:
            return _moe_down_call(act, wts["w_down"], gates2d, wts["b_down"], l, x2d, mods_l,
                                  final_norm.reshape(1, d), None, n_row_tiles, n_ctx_row_tiles, out_batch=batch)
```

```python
import functools
import math

import jax
import jax.numpy as jnp
from jax import lax
from jax.experimental import pallas as pl
from jax.experimental.pallas import tpu as pltpu

F32 = jnp.float32
BF16 = jnp.bfloat16

D_MODEL = 4096
GRID_W = 64
RET_CHUNK = 128
ROPE_BASE = 10000.0
EPS = 1e-6
HEAD_DIM = 128
DIFF_HEADS = 8
DIFF_HALF = 64
DIFF_WIDTH = 1024
RET_HEADS = 8
RET_DK = 128
RET_WIDTH = 1024
MLA_HEADS = 16
MLA_Q_RANK = 1024
MLA_KV_RANK = 512
MLA_NOPE = 128
MLA_ROPE = 64
MLA_V = 128
MLA_WIDTH = 2048
MLA_QK_PAD = 256
N_EXPERTS = 32
TOP_K = 4
EXPERT_FF = 128
SWIGLU_LIMIT = 7.0
SWIGLU_ALPHA = 1.702
MAIN_WIDTH = 8192
TAIL_WIDTH = 640
SEG = 1024

LANES = 128
VMEM_LIMIT_BYTES = 56 * 1024 * 1024
NEG_BIG = -1e30
CTX_MOD_ROW = 8
MOD_ROWS = 16


def _cparams(sem):
    return pltpu.CompilerParams(dimension_semantics=sem, vmem_limit_bytes=VMEM_LIMIT_BYTES)


def _pick_tile(total, candidates, at_least=1):
    for c in candidates:
        if total % c == 0 and c >= at_least:
            return c
    raise ValueError(f"no tile for {total}")


def _ada_kernel(c_ref, w_ref, b_ref, o_ref):
    c = c_ref[...]
    cs = (c * jax.nn.sigmoid(c)).astype(BF16)
    o_ref[...] = jnp.dot(cs, w_ref[...].astype(BF16), preferred_element_type=F32) + b_ref[...]


def _stack_conditioning(c, c_ctx):
    batch, d = c.shape
    return jnp.concatenate([c, jnp.zeros((CTX_MOD_ROW - batch, d), F32), c_ctx[None, :],
                            jnp.zeros((MOD_ROWS - CTX_MOD_ROW - 1, d), F32)], axis=0)


def _ada_tables(c_all, ada_w, ada_b):
    depth, d, n6 = ada_w.shape
    tn = 512
    return pl.pallas_call(
        _ada_kernel,
        grid=(depth, n6 // tn),
        in_specs=[
            pl.BlockSpec((MOD_ROWS, d), lambda l, j: (0, 0)),
            pl.BlockSpec((None, d, tn), lambda l, j: (l, 0, j)),
            pl.BlockSpec((None, 1, tn), lambda l, j: (l, 0, j)),
        ],
        out_specs=pl.BlockSpec((None, MOD_ROWS, tn), lambda l, j: (l, 0, j)),
        out_shape=jax.ShapeDtypeStruct((depth, MOD_ROWS, n6), F32),
        compiler_params=_cparams(("arbitrary", "arbitrary")),
        name="ada_tables",
    )(c_all, ada_w, ada_b.reshape(depth, 1, n6))


def _mod_norm(x, gain, shift, scale):
    ms = jnp.mean(x * x, axis=-1, keepdims=True)
    return x * lax.rsqrt(ms + EPS) * gain * (1.0 + scale) + shift


def _norm_kernel(ctx_ref, x_ref, g_ref, sh_ref, sc_ref, xs_ref, o_ref, *, n_ctx_tiles):
    def emit(src_ref, row):
        v = src_ref[...]
        xs_ref[...] = v
        o_ref[...] = _mod_norm(v, g_ref[...], sh_ref[pl.ds(row, 1), :], sc_ref[pl.ds(row, 1), :]).astype(BF16)

    is_ctx = pl.program_id(1) < n_ctx_tiles

    @pl.when(is_ctx)
    def _():
        emit(ctx_ref, CTX_MOD_ROW)

    @pl.when(jnp.logical_not(is_ctx))
    def _():
        emit(x_ref, pl.program_id(0))


def _router_gates(h, rw_ref, rb_ref):
    logits = jnp.dot(h, rw_ref[...], preferred_element_type=F32) + rb_ref[...]
    lane = lax.broadcasted_iota(jnp.int32, logits.shape, 1)
    work = logits
    num = jnp.zeros_like(logits)
    den = jnp.zeros((logits.shape[0], 1), F32)
    m0 = None
    for k in range(TOP_K):
        m = jnp.max(work, axis=-1, keepdims=True)
        idx = jnp.min(jnp.where(work == m, lane, LANES), axis=-1, keepdims=True)
        pick = lane == idx
        if k == 0:
            m0 = m
        e = jnp.exp(m - m0)
        num = num + jnp.where(pick, e, 0.0)
        den = den + e
        work = jnp.where(pick, -jnp.inf, work)
    return num / den


def _norm_call(ctx, x, gain, mods_l, shift_blk, scale_blk, n_ctx_tiles, tr):
    b, n, d = x.shape
    s = ctx.shape[1] + n
    ctx_spec = pl.BlockSpec((None, tr, d), lambda bi, i: (bi, jnp.minimum(i, n_ctx_tiles - 1), 0))
    x_spec = pl.BlockSpec((None, tr, d), lambda bi, i: (bi, jnp.maximum(i - n_ctx_tiles, 0), 0))
    out_spec = pl.BlockSpec((None, tr, d), lambda bi, i: (bi, i, 0))
    return pl.pallas_call(
        functools.partial(_norm_kernel, n_ctx_tiles=n_ctx_tiles),
        grid=(b, s // tr),
        in_specs=[ctx_spec, x_spec,
                  pl.BlockSpec((1, d), lambda bi, i: (0, 0)),
                  pl.BlockSpec((MOD_ROWS, d), lambda bi, i: (0, shift_blk)),
                  pl.BlockSpec((MOD_ROWS, d), lambda bi, i: (0, scale_blk))],
        out_specs=[out_spec, out_spec],
        out_shape=[jax.ShapeDtypeStruct((b, s, d), F32), jax.ShapeDtypeStruct((b, s, d), BF16)],
        compiler_params=_cparams(("parallel", "arbitrary")),
        name="mod_norm",
    )(ctx, x, gain, mods_l, mods_l)


def _rope64(blk, c, sa, sb):
    return blk * c + pltpu.roll(blk, 96, 1) * sa + pltpu.roll(blk, 32, 1) * sb


PROJ_SPLIT = 4


def _proj_kernel(h_ref, w_ref, c64_ref, sa64_ref, sb64_ref, c128_ref, s128_ref, o_ref):
    j = pl.program_id(1)
    h = h_ref[...]
    part = w_ref.shape[1] // PROJ_SPLIT
    n_blk = part // LANES

    def parts():
        for p in range(PROJ_SPLIT):
            acc = jnp.dot(h, w_ref[:, p * part:(p + 1) * part], preferred_element_type=F32)
            for t in range(n_blk):
                yield slice(p * part + t * LANES, p * part + (t + 1) * LANES), acc[:, t * LANES:(t + 1) * LANES]

    @pl.when((j == 0) | (j == 1))
    def _():
        scale = jnp.where(j == 0, DIFF_HALF ** -0.5, 1.0).astype(F32)
        c, sa, sb = c64_ref[...] * scale, sa64_ref[...] * scale, sb64_ref[...] * scale
        for sl, blk in parts():
            o_ref[:, sl] = _rope64(blk, c, sa, sb).astype(BF16)

    @pl.when((j == 3) | (j == 4))
    def _():
        scale = jnp.where(j == 4, RET_DK ** -0.5, 1.0).astype(F32)
        c, s = c128_ref[...] * scale, s128_ref[...] * scale
        for sl, blk in parts():
            o_ref[:, sl] = (blk * c + pltpu.roll(blk, 64, 1) * s).astype(BF16)

    @pl.when((j == 2) | (j >= 5))
    def _():
        for sl, blk in parts():
            o_ref[:, sl] = blk.astype(BF16)


def _proj_call(h2d, w_in_bf, l, tabs, tm, tiles_per_batch):
    r, d = h2d.shape
    tab_spec = pl.BlockSpec((tm, LANES), lambda i, j: (i % tiles_per_batch, 0))
    return pl.pallas_call(
        _proj_kernel,
        grid=(r // tm, MAIN_WIDTH // SEG),
        in_specs=[pl.BlockSpec((tm, d), lambda i, j: (i, 0)),
                  pl.BlockSpec((None, d, SEG), lambda i, j: (l, 0, j)),
                  tab_spec, tab_spec, tab_spec, tab_spec, tab_spec],
        out_specs=pl.BlockSpec((tm, SEG), lambda i, j: (i, j)),
        out_shape=jax.ShapeDtypeStruct((r, MAIN_WIDTH), BF16),
        compiler_params=_cparams(("parallel", "arbitrary")),
        name="in_proj",
    )(h2d, w_in_bf, tabs["c64"], tabs["sa64"], tabs["sb64"], tabs["c128"], tabs["s128"])


def _rms_gain(x, g_ref):
    ms = jnp.mean(x * x, axis=-1, keepdims=True)
    return x * lax.rsqrt(ms + EPS) * g_ref[...]


def _mla_q_kernel(qa_ref, g_ref, w_ref, c_ref, sa_ref, sb_ref, o_ref):
    qn = _rms_gain(qa_ref[...].astype(F32), g_ref).astype(BF16)
    q = jnp.dot(qn, w_ref[...], preferred_element_type=F32)
    scale = (MLA_NOPE + MLA_ROPE) ** -0.5
    c, sa, sb = c_ref[...] * scale, sa_ref[...] * scale, sb_ref[...] * scale
    for h in range(MLA_HEADS):
        lo = h * MLA_QK_PAD
        o_ref[:, lo:lo + LANES] = (q[:, lo:lo + LANES] * scale).astype(BF16)
        o_ref[:, lo + LANES:lo + 2 * LANES] = _rope64(q[:, lo + LANES:lo + 2 * LANES], c, sa, sb).astype(BF16)


def _mla_kv_kernel(h_ref, wt_ref, g_ref, w_ref, c_ref, sa_ref, sb_ref, k_ref, v_ref):
    t = jnp.dot(h_ref[...], wt_ref[...], preferred_element_type=F32)
    kvn = _rms_gain(t[:, :MLA_KV_RANK], g_ref).astype(BF16)
    kr = _rope64(t[:, MLA_KV_RANK:MLA_KV_RANK + LANES], c_ref[...], sa_ref[...], sb_ref[...]).astype(BF16)
    kv = jnp.dot(kvn, w_ref[...], preferred_element_type=F32)
    kn_width = MLA_HEADS * MLA_NOPE
    for h in range(MLA_HEADS):
        lo = h * MLA_QK_PAD
        k_ref[:, lo:lo + LANES] = kv[:, h * MLA_NOPE:(h + 1) * MLA_NOPE].astype(BF16)
        k_ref[:, lo + LANES:lo + 2 * LANES] = kr
    v_ref[...] = kv[:, kn_width:].astype(BF16)


def _mla_q_call(p2d, q_norm, wq, l, tabs, tm, tiles_per_batch):
    r = p2d.shape[0]
    tab_spec = pl.BlockSpec((tm, LANES), lambda i: (i % tiles_per_batch, 0))
    width = MLA_HEADS * MLA_QK_PAD
    return pl.pallas_call(
        _mla_q_kernel,
        grid=(r // tm,),
        in_specs=[pl.BlockSpec((tm, MLA_Q_RANK), lambda i: (i, (MAIN_WIDTH - MLA_Q_RANK) // MLA_Q_RANK)),
                  pl.BlockSpec((None, 1, MLA_Q_RANK), lambda i: (l, 0, 0)),
                  pl.BlockSpec((None, MLA_Q_RANK, width), lambda i: (l, 0, 0)),
                  tab_spec, tab_spec, tab_spec],
        out_specs=pl.BlockSpec((tm, width), lambda i: (i, 0)),
        out_shape=jax.ShapeDtypeStruct((r, width), BF16),
        compiler_params=_cparams(("parallel",)),
        name="mla_q_proj",
    )(p2d, q_norm, wq, tabs["c64r"], tabs["sa64r"], tabs["sb64r"])


def _mla_kv_call(h2d, w_tail, kv_norm, wkv, l, tabs, tm, tiles_per_batch):
    r, d = h2d.shape
    tab_spec = pl.BlockSpec((tm, LANES), lambda i: (i % tiles_per_batch, 0))
    kwidth = MLA_HEADS * MLA_QK_PAD
    return pl.pallas_call(
        _mla_kv_kernel,
        grid=(r // tm,),
        in_specs=[pl.BlockSpec((tm, d), lambda i: (i, 0)),
                  pl.BlockSpec((None, d, TAIL_WIDTH), lambda i: (l, 0, 0)),
                  pl.BlockSpec((None, 1, MLA_KV_RANK), lambda i: (l, 0, 0)),
                  pl.BlockSpec((None, MLA_KV_RANK, 2 * MLA_WIDTH), lambda i: (l, 0, 0)),
                  tab_spec, tab_spec, tab_spec],
        out_specs=[pl.BlockSpec((tm, kwidth), lambda i: (i, 0)),
                   pl.BlockSpec((tm, MLA_WIDTH), lambda i: (i, 0))],
        out_shape=[jax.ShapeDtypeStruct((r, kwidth), BF16),
                   jax.ShapeDtypeStruct((r, MLA_WIDTH), BF16)],
        compiler_params=_cparams(("parallel",)),
        name="mla_kv_proj",
    )(h2d, w_tail, kv_norm, wkv, tabs["c64r"], tabs["sa64r"], tabs["sb64r"])


KEY_CHUNK = 768


def _softmax_pv(q, k_ref, vo_ref, m_keys, k_cols, v_cols):
    chunk = _pick_tile(m_keys, (KEY_CHUNK, 512, 384, 256, 128))
    m_run = None
    acc = None
    for c0 in range(0, m_keys, chunk):
        s = lax.dot_general(q, k_ref[c0:c0 + chunk, k_cols], _NT, preferred_element_type=F32)
        m_c = jnp.max(s, axis=-1, keepdims=True)
        v_ones = vo_ref[c0:c0 + chunk, v_cols]
        if m_run is None:
            m_run = m_c
            acc = jnp.dot(jnp.exp((s - m_run).astype(BF16)), v_ones, preferred_element_type=F32)
        else:
            m_new = jnp.maximum(m_run, m_c)
            acc = (acc * jnp.exp(m_run - m_new)
                   + jnp.dot(jnp.exp((s - m_new).astype(BF16)), v_ones, preferred_element_type=F32))
            m_run = m_new
    half = acc.shape[1] // 2
    return acc[:, :half] / acc[:, half:]


def _fill_v_ones(v_ref, vo_ref, n_heads, width):
    ones = jnp.ones((v_ref.shape[0], width), BF16)
    for h in range(n_heads):
        vo_ref[:, 2 * width * h:2 * width * h + width] = v_ref[:, width * h:width * (h + 1)]
        vo_ref[:, 2 * width * h + width:2 * width * (h + 1)] = ones


_NT = (((1,), (1,)), ((), ()))


def _diff_attn_kernel(lam_ref, sub_ref, q_ref, k_ref, v_ref, o_ref, vo_ref, *, n_ctx_tiles, ctx_len, lam_init):
    @pl.when(pl.program_id(1) == 0)
    def _():
        _fill_v_ones(v_ref, vo_ref, DIFF_HEADS, HEAD_DIM)

    lv = lam_ref[...]
    lam = (jnp.exp(jnp.sum(lv[0:1] * lv[1:2], axis=-1, keepdims=True))
           - jnp.exp(jnp.sum(lv[2:3] * lv[3:4], axis=-1, keepdims=True)) + lam_init)
    sub = sub_ref[...] * (1.0 - lam_init)

    def run(m_keys):
        for h in range(DIFF_HEADS):
            sl = slice(h * HEAD_DIM, (h + 1) * HEAD_DIM)
            q = q_ref[:, sl]
            v_cols = slice(2 * h * HEAD_DIM, 2 * (h + 1) * HEAD_DIM)
            first = lax.broadcasted_iota(jnp.int32, q.shape, 1) < DIFF_HALF
            zero = jnp.zeros_like(q)
            o = (_softmax_pv(jnp.where(first, q, zero), k_ref, vo_ref, m_keys, sl, v_cols)
                 - lam * _softmax_pv(jnp.where(first, zero, q), k_ref, vo_ref, m_keys, sl, v_cols))
            ms = jnp.mean(o * o, axis=-1, keepdims=True)
            o_ref[:, sl] = (o * lax.rsqrt(ms + EPS) * sub).astype(BF16)

    is_ctx = pl.program_id(1) < n_ctx_tiles

    @pl.when(is_ctx)
    def _():
        run(ctx_len)

    @pl.when(jnp.logical_not(is_ctx))
    def _():
        run(k_ref.shape[0])


def _diff_attn_call(p3d, lamv, subln, lam_init, tq, n_ctx_tiles, ctx_len):
    b, s, _ = p3d.shape
    return pl.pallas_call(
        functools.partial(_diff_attn_kernel, n_ctx_tiles=n_ctx_tiles, ctx_len=ctx_len, lam_init=lam_init),
        grid=(b, s // tq),
        in_specs=[pl.BlockSpec((4, DIFF_HALF), lambda bi, i: (0, 0)),
                  pl.BlockSpec((1, HEAD_DIM), lambda bi, i: (0, 0)),
                  pl.BlockSpec((None, tq, DIFF_WIDTH), lambda bi, i: (bi, i, 0)),
                  pl.BlockSpec((None, s, DIFF_WIDTH), lambda bi, i: (bi, 0, 1)),
                  pl.BlockSpec((None, s, DIFF_WIDTH), lambda bi, i: (bi, 0, 2))],
        out_specs=pl.BlockSpec((None, tq, DIFF_WIDTH), lambda bi, i: (bi, i, 0)),
        out_shape=jax.ShapeDtypeStruct((b, s, DIFF_WIDTH), BF16),
        scratch_shapes=[pltpu.VMEM((s, 2 * DIFF_WIDTH), BF16)],
        compiler_params=_cparams(("parallel", "arbitrary")),
        name="diff_attn",
    )(lamv, subln, p3d, p3d, p3d)


MLA_HEAD_GROUP = 8


def _mla_attn_kernel(q_ref, k_ref, v_ref, o_ref, vo_ref, *, n_ctx_tiles, ctx_len):
    @pl.when(pl.program_id(2) == 0)
    def _():
        _fill_v_ones(v_ref, vo_ref, MLA_HEAD_GROUP, MLA_V)

    def run(m_keys):
        for h in range(MLA_HEAD_GROUP):
            qk = slice(h * MLA_QK_PAD, (h + 1) * MLA_QK_PAD)
            vv = slice(h * MLA_V, (h + 1) * MLA_V)
            v_cols = slice(2 * h * MLA_V, 2 * (h + 1) * MLA_V)
            o_ref[:, vv] = _softmax_pv(q_ref[:, qk], k_ref, vo_ref, m_keys, qk, v_cols).astype(BF16)

    is_ctx = pl.program_id(2) < n_ctx_tiles

    @pl.when(is_ctx)
    def _():
        run(ctx_len)

    @pl.when(jnp.logical_not(is_ctx))
    def _():
        run(k_ref.shape[0])


def _mla_attn_call(qc, kc, vc, tq, n_ctx_tiles, ctx_len):
    b, s, _ = qc.shape
    qw = MLA_HEAD_GROUP * MLA_QK_PAD
    vw = MLA_HEAD_GROUP * MLA_V
    return pl.pallas_call(
        functools.partial(_mla_attn_kernel, n_ctx_tiles=n_ctx_tiles, ctx_len=ctx_len),
        grid=(b, MLA_HEADS // MLA_HEAD_GROUP, s // tq),
        in_specs=[pl.BlockSpec((None, tq, qw), lambda bi, g, i: (bi, i, g)),
                  pl.BlockSpec((None, s, qw), lambda bi, g, i: (bi, 0, g)),
                  pl.BlockSpec((None, s, vw), lambda bi, g, i: (bi, 0, g))],
        out_specs=pl.BlockSpec((None, tq, vw), lambda bi, g, i: (bi, i, g)),
        out_shape=jax.ShapeDtypeStruct((b, s, MLA_WIDTH), BF16),
        scratch_shapes=[pltpu.VMEM((s, 2 * vw), BF16)],
        compiler_params=_cparams(("parallel", "parallel", "arbitrary")),
        name="mla_attn",
    )(qc, kc, vc)


RET_HEAD_GROUP = 4
_TN = (((0,), (0,)), ((), ()))


def _ret_kernel(lg_ref, q_ref, k_ref, v_ref, g_ref, gn_ref, o_ref, intra_ref, rowdec_ref, cdec_ref, st_ref, acc_ref,
                *, chunk, n_ctx_chunks):
    c = chunk
    n_chunks = q_ref.shape[0] // c
    hp = pl.program_id(1)
    row = lax.broadcasted_iota(jnp.int32, (c, c), 0).astype(F32)
    col = lax.broadcasted_iota(jnp.int32, (c, c), 1).astype(F32)
    rel = row - col
    pos = lax.broadcasted_iota(jnp.int32, (c, RET_DK), 0).astype(F32)
    span = jnp.full((RET_DK, RET_DK), float(c), F32)

    for hh in range(RET_HEAD_GROUP):
        lgf = lg_ref[0, hp * RET_HEAD_GROUP + hh]
        lgb = lg_ref[1, hp * RET_HEAD_GROUP + hh]
        intra_ref[hh, 0] = jnp.where(rel >= 0, jnp.exp(jnp.maximum(rel, 0.0) * lgf), 0.0)
        intra_ref[hh, 1] = jnp.where(rel <= 0, jnp.exp(jnp.maximum(-rel, 0.0) * lgb), 0.0)
        rowdec_ref[hh, 0] = jnp.exp((pos + 1.0) * lgf)
        rowdec_ref[hh, 1] = jnp.exp((c - 1.0 - pos) * lgf)
        rowdec_ref[hh, 2] = jnp.exp((c - pos) * lgb)
        rowdec_ref[hh, 3] = jnp.exp(pos * lgb)
        cdec_ref[hh, 0] = jnp.exp(span * lgf)
        cdec_ref[hh, 1] = jnp.exp(span * lgb)

    st_ref[...] = jnp.zeros_like(st_ref)

    def chunk_update(hh, direction, idx):
        sl = slice(hh * RET_DK, (hh + 1) * RET_DK)
        rows = pl.ds(pl.multiple_of(idx * c, c), c)
        qc = q_ref[rows, sl]
        kc = k_ref[rows, sl]
        vc = v_ref[rows, sl]
        state = st_ref[hh, direction]
        att = lax.dot_general(qc, kc, _NT, preferred_element_type=F32) * intra_ref[hh, direction]
        o = (jnp.dot(att.astype(BF16), vc, preferred_element_type=F32)
             + jnp.dot(qc, state.astype(BF16), preferred_element_type=F32) * rowdec_ref[hh, 2 * direction])
        kd = (kc.astype(F32) * rowdec_ref[hh, 2 * direction + 1]).astype(BF16)
        st_ref[hh, direction] = (state * cdec_ref[hh, direction]
                                 + lax.dot_general(kd, vc, _TN, preferred_element_type=F32))
        acc_ref[direction, rows, sl] = o

    def step(t, carry):
        bwd = jnp.where(t < n_ctx_chunks, n_ctx_chunks - 1 - t, n_chunks - 1 - t + n_ctx_chunks)
        for hh in range(RET_HEAD_GROUP):
            chunk_update(hh, 0, t)
            chunk_update(hh, 1, bwd)
        return carry

    lax.fori_loop(0, n_chunks, step, 0)

    for hh in range(RET_HEAD_GROUP):
        sl = slice(hh * RET_DK, (hh + 1) * RET_DK)
        o = acc_ref[0, :, sl] + acc_ref[1, :, sl]
        mu = jnp.mean(o, axis=-1, keepdims=True)
        oc = o - mu
        y = oc * lax.rsqrt(jnp.mean(oc * oc, axis=-1, keepdims=True) + 1e-5) * gn_ref[:, sl]
        g = g_ref[:, sl].astype(F32)
        o_ref[:, sl] = (y * (g * jax.nn.sigmoid(g))).astype(BF16)


def _ret_call(p3d, log_gamma, gn_gain, chunk, n_ctx_chunks):
    b, s, _ = p3d.shape
    w = RET_HEAD_GROUP * RET_DK
    base = 3 * SEG // w

    def col(seg):
        return lambda bi, hp, lg: (bi, 0, base + seg * (SEG // w) + hp)

    grid_spec = pltpu.PrefetchScalarGridSpec(
        num_scalar_prefetch=1,
        grid=(b, RET_HEADS // RET_HEAD_GROUP),
        in_specs=[pl.BlockSpec((None, s, w), col(0)),
                  pl.BlockSpec((None, s, w), col(1)),
                  pl.BlockSpec((None, s, w), col(2)),
                  pl.BlockSpec((None, s, w), col(3)),
                  pl.BlockSpec((1, w), lambda bi, hp, lg: (0, hp))],
        out_specs=pl.BlockSpec((None, s, w), lambda bi, hp, lg: (bi, 0, hp)),
        scratch_shapes=[pltpu.VMEM((RET_HEAD_GROUP, 2, chunk, chunk), F32),
                        pltpu.VMEM((RET_HEAD_GROUP, 4, chunk, RET_DK), F32),
                        pltpu.VMEM((RET_HEAD_GROUP, 2, RET_DK, RET_DK), F32),
                        pltpu.VMEM((RET_HEAD_GROUP, 2, RET_DK, RET_DK), F32),
                        pltpu.VMEM((2, s, w), F32)],
    )
    return pl.pallas_call(
        functools.partial(_ret_kernel, chunk=chunk, n_ctx_chunks=n_ctx_chunks),
        grid_spec=grid_spec,
        out_shape=jax.ShapeDtypeStruct((b, s, RET_WIDTH), BF16),
        compiler_params=_cparams(("parallel", "arbitrary")),
        name="retention",
    )(log_gamma, p3d, p3d, p3d, p3d, gn_gain)


ROW_TILE = 128


def _tile_mod_row(tiles_per_batch, n_ctx_tiles):
    i = pl.program_id(0)
    return jnp.where(i % tiles_per_batch < n_ctx_tiles, CTX_MOD_ROW, i // tiles_per_batch)


def _resident(shape, index_map):
    return pl.BlockSpec(shape, index_map, pipeline_mode=pl.Buffered(1))


def _out_proj_kernel(a_ref, b_ref, c_ref, w_ref, x_ref, g1_ref, gain_ref, sh_ref, sc_ref, xo_ref, h_ref,
                     *, tiles_per_batch, n_ctx_tiles):
    row = pl.ds(_tile_mod_row(tiles_per_batch, n_ctx_tiles), 1)
    acc = jnp.dot(a_ref[...], w_ref[0:DIFF_WIDTH, :], preferred_element_type=F32)
    acc += jnp.dot(b_ref[...], w_ref[DIFF_WIDTH:DIFF_WIDTH + RET_WIDTH, :], preferred_element_type=F32)
    acc += jnp.dot(c_ref[...], w_ref[DIFF_WIDTH + RET_WIDTH:, :], preferred_element_type=F32)
    x_new = x_ref[...] + g1_ref[row, :] * acc
    xo_ref[...] = x_new
    h_ref[...] = _mod_norm(x_new, gain_ref[...], sh_ref[row, :], sc_ref[row, :]).astype(BF16)


def _out_proj_call(a2d, b2d, c2d, w_out, l, x2d, mods_l, gain, tiles_per_batch, n_ctx_tiles):
    r, d = x2d.shape
    tr = ROW_TILE
    rows = lambda w: pl.BlockSpec((tr, w), lambda i: (i, 0))
    mod = lambda blk: pl.BlockSpec((MOD_ROWS, d), lambda i: (0, blk))
    return pl.pallas_call(
        functools.partial(_out_proj_kernel, tiles_per_batch=tiles_per_batch, n_ctx_tiles=n_ctx_tiles),
        grid=(r // tr,),
        in_specs=[rows(DIFF_WIDTH), rows(RET_WIDTH), rows(MLA_WIDTH),
                  _resident((None, d, d), lambda i: (l, 0, 0)),
                  rows(d), mod(2),
                  pl.BlockSpec((1, d), lambda i: (0, 0)), mod(3), mod(4)],
        out_specs=[rows(d), rows(d)],
        out_shape=[jax.ShapeDtypeStruct((r, d), F32), jax.ShapeDtypeStruct((r, d), BF16)],
        compiler_params=_cparams(("arbitrary",)),
        name="out_proj",
    )(a2d, b2d, c2d, w_out, x2d, mods_l, gain, mods_l, mods_l)


def _moe_up_kernel(h_ref, w_ref, b_ref, rw_ref, rb_ref, o_ref, gate_out_ref, gate_ref):
    j = pl.program_id(1)
    n_e = w_ref.shape[0]

    @pl.when(j == 0)
    def _():
        gates = _router_gates(h_ref[...], rw_ref, rb_ref)
        gate_ref[...] = gates
        gate_out_ref[...] = gates

    e_row = lax.broadcasted_iota(jnp.int32, (LANES, n_e * EXPERT_FF), 0)
    e_col = lax.broadcasted_iota(jnp.int32, (LANES, n_e * EXPERT_FF), 1) // EXPERT_FF
    expand = (e_row == j * n_e + e_col).astype(BF16)
    ge = jnp.dot(gate_ref[...].astype(BF16), expand, preferred_element_type=F32)
    h = h_ref[...]
    for e in range(n_e):
        lo = e * 2 * EXPERT_FF
        gu = jnp.dot(h, w_ref[e], preferred_element_type=F32) + b_ref[:, lo:lo + 2 * EXPERT_FF]
        glu = jnp.minimum(gu[:, :EXPERT_FF], SWIGLU_LIMIT)
        lin = jnp.clip(gu[:, EXPERT_FF:], -SWIGLU_LIMIT, SWIGLU_LIMIT)
        act = glu * jax.nn.sigmoid(SWIGLU_ALPHA * glu) * (lin + 1.0)
        sl = slice(e * EXPERT_FF, (e + 1) * EXPERT_FF)
        o_ref[:, sl] = (act * ge[:, sl]).astype(BF16)


def _moe_up_call(h2d, w_gu, b_gu, rw, rb, l, tm):
    r, d = h2d.shape
    tn = SEG
    n_e = tn // (2 * EXPERT_FF)
    n = N_EXPERTS * 2 * EXPERT_FF
    return pl.pallas_call(
        _moe_up_kernel,
        grid=(r // tm, n // tn),
        in_specs=[pl.BlockSpec((tm, d), lambda i, j: (i, 0)),
                  pl.BlockSpec((None, n_e, d, 2 * EXPERT_FF), lambda i, j: (l, j, 0, 0)),
                  pl.BlockSpec((None, 1, tn), lambda i, j: (l, 0, j)),
                  pl.BlockSpec((None, d, LANES), lambda i, j: (l, 0, 0)),
                  pl.BlockSpec((None, 1, LANES), lambda i, j: (l, 0, 0))],
        out_specs=[pl.BlockSpec((tm, tn // 2), lambda i, j: (i, j)),
                   pl.BlockSpec((tm, LANES), lambda i, j: (i, 0))],
        out_shape=[jax.ShapeDtypeStruct((r, n // 2), BF16), jax.ShapeDtypeStruct((r, LANES), F32)],
        scratch_shapes=[pltpu.VMEM((tm, LANES), F32)],
        compiler_params=_cparams(("parallel", "arbitrary")),
        name="moe_up",
    )(h2d, w_gu, b_gu, rw, rb)


def _moe_down_rows(a_ref, w_ref, gates_ref, bd_ref, x_ref, g2_ref, row):
    acc = jnp.dot(a_ref[...], w_ref[...], preferred_element_type=F32)
    acc += jnp.dot(gates_ref[...].astype(BF16), bd_ref[...], preferred_element_type=F32)
    return x_ref[...] + g2_ref[row, :] * acc


def _moe_down_kernel(a_ref, w_ref, gates_ref, bd_ref, x_ref, g2_ref, gain_ref, sh_ref, sc_ref, xo_ref, h_ref,
                     *, tiles_per_batch, n_ctx_tiles):
    row = pl.ds(_tile_mod_row(tiles_per_batch, n_ctx_tiles), 1)
    x_new = _moe_down_rows(a_ref, w_ref, gates_ref, bd_ref, x_ref, g2_ref, row)
    xo_ref[...] = x_new
    h_ref[...] = _mod_norm(x_new, gain_ref[...], sh_ref[row, :], sc_ref[row, :]).astype(BF16)


def _moe_down_final_kernel(a_ref, w_ref, gates_ref, bd_ref, x_ref, g2_ref, gain_ref, o_ref,
                           *, tiles_per_batch, n_ctx_tiles):
    row = pl.ds(_tile_mod_row(tiles_per_batch, n_ctx_tiles), 1)
    x_new = _moe_down_rows(a_ref, w_ref, gates_ref, bd_ref, x_ref, g2_ref, row)
    ms = jnp.mean(x_new * x_new, axis=-1, keepdims=True)
    o_ref[...] = x_new * lax.rsqrt(ms + EPS) * gain_ref[...]


def _moe_down_call(a2d, w_down, gates2d, b_down, l, x2d, mods_l, gain, mods_next, tiles_per_batch, n_ctx_tiles,
                   out_batch=None):
    r, d = x2d.shape
    k = a2d.shape[1]
    tr = ROW_TILE
    rows = lambda w: pl.BlockSpec((tr, w), lambda i: (i, 0))
    mod = lambda blk: pl.BlockSpec((MOD_ROWS, d), lambda i: (0, blk))
    in_specs = [rows(k), _resident((None, k, d), lambda i: (l, 0, 0)), rows(LANES),
                _resident((None, LANES, d), lambda i: (l, 0, 0)), rows(d), mod(5),
                pl.BlockSpec((1, d), lambda i: (0, 0))]
    static = dict(tiles_per_batch=tiles_per_batch, n_ctx_tiles=n_ctx_tiles)
    if out_batch is None:
        return pl.pallas_call(
            functools.partial(_moe_down_kernel, **static),
            grid=(r // tr,),
            in_specs=in_specs + [mod(0), mod(1)],
            out_specs=[rows(d), rows(d)],
            out_shape=[jax.ShapeDtypeStruct((r, d), F32), jax.ShapeDtypeStruct((r, d), BF16)],
            compiler_params=_cparams(("arbitrary",)),
            name="moe_down",
        )(a2d, w_down, gates2d, b_down, x2d, mods_l, gain, mods_next, mods_next)
    latent_tiles = tiles_per_batch - n_ctx_tiles
    return pl.pallas_call(
        functools.partial(_moe_down_final_kernel, **static),
        grid=(r // tr,),
        in_specs=in_specs,
        out_specs=pl.BlockSpec((None, tr, d), lambda i: (i // tiles_per_batch,
                                                          jnp.maximum(i % tiles_per_batch - n_ctx_tiles, 0), 0)),
        out_shape=jax.ShapeDtypeStruct((out_batch, latent_tiles * tr, d), F32),
        compiler_params=_cparams(("arbitrary",)),
        name="moe_down_final",
    )(a2d, w_down, gates2d, b_down, x2d, mods_l, gain)


def _rope_tables(n, ctx_len):
    rows = n // GRID_W
    row = jnp.repeat(jnp.arange(rows, dtype=F32), GRID_W)
    col = jnp.tile(jnp.arange(GRID_W, dtype=F32), rows)

    def cos_sin(rot_dim):
        pairs = rot_dim // 4
        inv = ROPE_BASE ** (-jnp.arange(pairs, dtype=F32) / pairs)
        ang = jnp.concatenate([row[:, None] * inv, col[:, None] * inv], axis=-1)
        cos = jnp.concatenate([jnp.ones((ctx_len, rot_dim // 2), F32), jnp.cos(ang)], axis=0)
        sin = jnp.concatenate([jnp.zeros((ctx_len, rot_dim // 2), F32), jnp.sin(ang)], axis=0)
        return cos, sin

    cos32, sin32 = cos_sin(DIFF_HALF)
    cos64, sin64 = cos_sin(RET_DK)
    z32 = jnp.zeros_like(sin32)
    one64 = jnp.ones((cos32.shape[0], 64), F32)
    z64 = jnp.zeros_like(one64)
    return {
        "c64": jnp.concatenate([cos32] * 4, axis=-1),
        "sa64": jnp.concatenate([-sin32, z32, -sin32, z32], axis=-1),
        "sb64": jnp.concatenate([z32, sin32, z32, sin32], axis=-1),
        "c64r": jnp.concatenate([cos32, cos32, one64], axis=-1),
        "sa64r": jnp.concatenate([-sin32, z32, z64], axis=-1),
        "sb64r": jnp.concatenate([z32, sin32, z64], axis=-1),
        "c128": jnp.concatenate([cos64, cos64], axis=-1),
        "s128": jnp.concatenate([-sin64, sin64], axis=-1),
    }


def _prep_weights(w_in, mla_wq_b, mla_wkv_b, w_out, router_w, router_b,
                  exp_w_gu, exp_b_gu, exp_w_down, exp_b_down):
    depth, d, in_width = w_in.shape
    w_tail = jnp.pad(w_in[:, :, MAIN_WIDTH:], ((0, 0), (0, 0), (0, TAIL_WIDTH - (in_width - MAIN_WIDTH)))).astype(BF16)
    wq = mla_wq_b.reshape(depth, MLA_Q_RANK, MLA_HEADS, MLA_NOPE + MLA_ROPE)
    wq = jnp.pad(wq, ((0, 0), (0, 0), (0, 0), (0, MLA_QK_PAD - MLA_NOPE - MLA_ROPE)))
    wq = wq.reshape(depth, MLA_Q_RANK, MLA_HEADS * MLA_QK_PAD).astype(BF16)
    wkv = mla_wkv_b.reshape(depth, MLA_KV_RANK, MLA_HEADS, MLA_NOPE + MLA_V)
    wkv = jnp.concatenate([wkv[..., :MLA_NOPE].reshape(depth, MLA_KV_RANK, -1),
                           wkv[..., MLA_NOPE:].reshape(depth, MLA_KV_RANK, -1)], axis=-1).astype(BF16)
    rw = jnp.pad(router_w, ((0, 0), (0, 0), (0, LANES - N_EXPERTS))).astype(BF16)
    rb = jnp.concatenate([router_b, jnp.full((depth, LANES - N_EXPERTS), NEG_BIG, F32)], axis=-1)
    return dict(
        w_in=w_in.astype(BF16), w_tail=w_tail, wq=wq, wkv=wkv, w_out=w_out.astype(BF16),
        rw=rw, rb=rb.reshape(depth, 1, LANES),
        w_gu=exp_w_gu.astype(BF16),
        b_gu=exp_b_gu.reshape(depth, 1, N_EXPERTS * 2 * EXPERT_FF),
        w_down=exp_w_down.astype(BF16).reshape(depth, N_EXPERTS * EXPERT_FF, d),
        b_down=jnp.pad(exp_b_down, ((0, 0), (0, LANES - N_EXPERTS), (0, 0))).astype(BF16))


def kernel(x, c, ctx, c_ctx, ada_w, ada_b, norm1, norm2, w_in, diff_lq1, diff_lk1, diff_lq2, diff_lk2, diff_subln, ret_decay, ret_norm, mla_q_norm, mla_kv_norm, mla_wq_b, mla_wkv_b, w_out, router_w, router_b, exp_w_gu, exp_b_gu, exp_w_down, exp_b_down, final_norm):
    batch, n, d = x.shape
    ctx_len = ctx.shape[1]
    depth = ada_w.shape[0]
    s = ctx_len + n
    r = batch * s
    assert d == D_MODEL and batch <= CTX_MOD_ROW and ctx_len % RET_CHUNK == 0 and n % RET_CHUNK == 0
    assert depth >= 1 and RET_CHUNK % ROW_TILE == 0

    tq = _pick_tile(ctx_len, (256, 128))
    n_ctx_tiles = ctx_len // tq
    tm = _pick_tile(s, (768, 576, 384, 256, 128), at_least=ctx_len)
    tm_mla = _pick_tile(s, (384, 256, 128))
    tiles_per_batch = s // tm
    ret_chunk = _pick_tile(math.gcd(ctx_len, n), (2 * RET_CHUNK, RET_CHUNK))

    mods = _ada_tables(_stack_conditioning(c, c_ctx), ada_w, ada_b)
    tabs = _rope_tables(n, ctx_len)
    log_gamma = jnp.log1p(-jnp.exp2(-ret_decay.astype(F32)))

    wts = _prep_weights(w_in, mla_wq_b, mla_wkv_b, w_out, router_w, router_b,
                        exp_w_gu, exp_b_gu, exp_w_down, exp_b_down)
    q_norm = mla_q_norm.reshape(depth, 1, -1)
    kv_norm = mla_kv_norm.reshape(depth, 1, -1)

    n_row_tiles = s // ROW_TILE
    n_ctx_row_tiles = ctx_len // ROW_TILE
    xs, h = _norm_call(ctx, x, norm1[0].reshape(1, d), mods[0], 0, 1, n_ctx_tiles, tq)
    x2d = xs.reshape(r, d)
    h2d = h.reshape(r, d)

    for l in range(depth):
        lam_init = 0.8 - 0.6 * math.exp(-0.3 * l)
        mods_l = mods[l]
        p2d = _proj_call(h2d, wts["w_in"], l, tabs, tm, tiles_per_batch)
        p3d = p2d.reshape(batch, s, MAIN_WIDTH)
        qc = _mla_q_call(p2d, q_norm, wts["wq"], l, tabs, tm_mla, s // tm_mla)
        kc, vc = _mla_kv_call(h2d, wts["w_tail"], kv_norm, wts["wkv"], l, tabs, tm_mla, s // tm_mla)
        lamv = jnp.stack([diff_lq1[l], diff_lk1[l], diff_lq2[l], diff_lk2[l]]).astype(F32)
        a = _diff_attn_call(p3d, lamv, diff_subln[l].reshape(1, -1), lam_init, tq, n_ctx_tiles, ctx_len)
        bb = _ret_call(p3d, log_gamma[l], ret_norm[l].reshape(1, -1), ret_chunk, ctx_len // ret_chunk)
        cc = _mla_attn_call(qc.reshape(batch, s, -1), kc.reshape(batch, s, -1), vc.reshape(batch, s, -1),
                            tq, n_ctx_tiles, ctx_len)
        x2d, h2 = _out_proj_call(a.reshape(r, -1), bb.reshape(r, -1), cc.reshape(r, -1), wts["w_out"], l, x2d,
                                 mods_l, norm2[l].reshape(1, d), n_row_tiles, n_ctx_row_tiles)
        act, gates2d = _moe_up_call(h2, wts["w_gu"], wts["b_gu"], wts["rw"], wts["rb"], l, tm)
        if l + 1 < depth:
            x2d, h2d = _moe_down_call(act, wts["w_down"], gates2d, wts["b_down"], l, x2d, mods_l,
                                      norm1[l + 1].reshape(1, d), mods[l + 1], n_row_tiles, n_ctx_row_tiles)
        else:
            return _moe_down_call(act, wts["w_down"], gates2d, wts["b_down"], l, x2d, mods_l,
                                  final_norm.reshape(1, d), None, n_row_tiles, n_ctx_row_tiles, out_batch=batch)
```

```python
import functools
import math

import jax
import jax.numpy as jnp
from jax import lax
from jax.experimental import pallas as pl
from jax.experimental.pallas import tpu as pltpu

F32 = jnp.float32
BF16 = jnp.bfloat16

D_MODEL = 4096
GRID_W = 64
RET_CHUNK = 128
ROPE_BASE = 10000.0
EPS = 1e-6
HEAD_DIM = 128
DIFF_HEADS = 8
DIFF_HALF = 64
DIFF_WIDTH = 1024
RET_HEADS = 8
RET_DK = 128
RET_WIDTH = 1024
MLA_HEADS = 16
MLA_Q_RANK = 1024
MLA_KV_RANK = 512
MLA_NOPE = 128
MLA_ROPE = 64
MLA_V = 128
MLA_WIDTH = 2048
MLA_QK_PAD = 256
N_EXPERTS = 32
TOP_K = 4
EXPERT_FF = 128
SWIGLU_LIMIT = 7.0
SWIGLU_ALPHA = 1.702
MAIN_WIDTH = 8192
TAIL_WIDTH = 640
SEG = 1024

LANES = 128
VMEM_LIMIT_BYTES = 56 * 1024 * 1024
NEG_BIG = -1e30
CTX_MOD_ROW = 8
MOD_ROWS = 16


def _cparams(sem):
    return pltpu.CompilerParams(dimension_semantics=sem, vmem_limit_bytes=VMEM_LIMIT_BYTES)


def _pick_tile(total, candidates, at_least=1):
    for c in candidates:
        if total % c == 0 and c >= at_least:
            return c
    raise ValueError(f"no tile for {total}")


def _ada_kernel(c_ref, w_ref, b_ref, o_ref):
    c = c_ref[...]
    cs = (c * jax.nn.sigmoid(c)).astype(BF16)
    o_ref[...] = jnp.dot(cs, w_ref[...].astype(BF16), preferred_element_type=F32) + b_ref[...]


def _stack_conditioning(c, c_ctx):
    batch, d = c.shape
    return jnp.concatenate([c, jnp.zeros((CTX_MOD_ROW - batch, d), F32), c_ctx[None, :],
                            jnp.zeros((MOD_ROWS - CTX_MOD_ROW - 1, d), F32)], axis=0)


def _ada_tables(c_all, ada_w, ada_b):
    depth, d, n6 = ada_w.shape
    tn = 512
    return pl.pallas_call(
        _ada_kernel,
        grid=(depth, n6 // tn),
        in_specs=[
            pl.BlockSpec((MOD_ROWS, d), lambda l, j: (0, 0)),
            pl.BlockSpec((None, d, tn), lambda l, j: (l, 0, j)),
            pl.BlockSpec((None, 1, tn), lambda l, j: (l, 0, j)),
        ],
        out_specs=pl.BlockSpec((None, MOD_ROWS, tn), lambda l, j: (l, 0, j)),
        out_shape=jax.ShapeDtypeStruct((depth, MOD_ROWS, n6), F32),
        compiler_params=_cparams(("arbitrary", "arbitrary")),
        name="ada_tables",
    )(c_all, ada_w, ada_b.reshape(depth, 1, n6))


def _mod_norm(x, gain, shift, scale):
    ms = jnp.mean(x * x, axis=-1, keepdims=True)
    return x * lax.rsqrt(ms + EPS) * gain * (1.0 + scale) + shift


def _norm_kernel(ctx_ref, x_ref, g_ref, sh_ref, sc_ref, xs_ref, o_ref, *, n_ctx_tiles):
    def emit(src_ref, row):
        v = src_ref[...]
        xs_ref[...] = v
        o_ref[...] = _mod_norm(v, g_ref[...], sh_ref[pl.ds(row, 1), :], sc_ref[pl.ds(row, 1), :]).astype(BF16)

    is_ctx = pl.program_id(1) < n_ctx_tiles

    @pl.when(is_ctx)
    def _():
        emit(ctx_ref, CTX_MOD_ROW)

    @pl.when(jnp.logical_not(is_ctx))
    def _():
        emit(x_ref, pl.program_id(0))


def _router_gates(h, rw_ref, rb_ref):
    logits = jnp.dot(h, rw_ref[...], preferred_element_type=F32) + rb_ref[...]
    lane = lax.broadcasted_iota(jnp.int32, logits.shape, 1)
    work = logits
    num = jnp.zeros_like(logits)
    den = jnp.zeros((logits.shape[0], 1), F32)
    m0 = None
    for k in range(TOP_K):
        m = jnp.max(work, axis=-1, keepdims=True)
        idx = jnp.min(jnp.where(work == m, lane, LANES), axis=-1, keepdims=True)
        pick = lane == idx
        if k == 0:
            m0 = m
        e = jnp.exp(m - m0)
        num = num + jnp.where(pick, e, 0.0)
        den = den + e
        work = jnp.where(pick, -jnp.inf, work)
    return num / den


def _norm_call(ctx, x, gain, mods_l, shift_blk, scale_blk, n_ctx_tiles, tr):
    b, n, d = x.shape
    s = ctx.shape[1] + n
    ctx_spec = pl.BlockSpec((None, tr, d), lambda bi, i: (bi, jnp.minimum(i, n_ctx_tiles - 1), 0))
    x_spec = pl.BlockSpec((None, tr, d), lambda bi, i: (bi, jnp.maximum(i - n_ctx_tiles, 0), 0))
    out_spec = pl.BlockSpec((None, tr, d), lambda bi, i: (bi, i, 0))
    return pl.pallas_call(
        functools.partial(_norm_kernel, n_ctx_tiles=n_ctx_tiles),
        grid=(b, s // tr),
        in_specs=[ctx_spec, x_spec,
                  pl.BlockSpec((1, d), lambda bi, i: (0, 0)),
                  pl.BlockSpec((MOD_ROWS, d), lambda bi, i: (0, shift_blk)),
                  pl.BlockSpec((MOD_ROWS, d), lambda bi, i: (0, scale_blk))],
        out_specs=[out_spec, out_spec],
        out_shape=[jax.ShapeDtypeStruct((b, s, d), F32), jax.ShapeDtypeStruct((b, s, d), BF16)],
        compiler_params=_cparams(("parallel", "arbitrary")),
        name="mod_norm",
    )(ctx, x, gain, mods_l, mods_l)


def _rope64(blk, c, sa, sb):
    return blk * c + pltpu.roll(blk, 96, 1) * sa + pltpu.roll(blk, 32, 1) * sb


PROJ_SPLIT = 4


def _proj_kernel(h_ref, w_ref, c64_ref, sa64_ref, sb64_ref, c128_ref, s128_ref, o_ref):
    j = pl.program_id(1)
    h = h_ref[...]
    part = w_ref.shape[1] // PROJ_SPLIT
    n_blk = part // LANES

    def parts():
        for p in range(PROJ_SPLIT):
            acc = jnp.dot(h, w_ref[:, p * part:(p + 1) * part], preferred_element_type=F32)
            for t in range(n_blk):
                yield slice(p * part + t * LANES, p * part + (t + 1) * LANES), acc[:, t * LANES:(t + 1) * LANES]

    @pl.when((j == 0) | (j == 1))
    def _():
        scale = jnp.where(j == 0, DIFF_HALF ** -0.5, 1.0).astype(F32)
        c, sa, sb = c64_ref[...] * scale, sa64_ref[...] * scale, sb64_ref[...] * scale
        for sl, blk in parts():
            o_ref[:, sl] = _rope64(blk, c, sa, sb).astype(BF16)

    @pl.when((j == 3) | (j == 4))
    def _():
        scale = jnp.where(j == 4, RET_DK ** -0.5, 1.0).astype(F32)
        c, s = c128_ref[...] * scale, s128_ref[...] * scale
        for sl, blk in parts():
            o_ref[:, sl] = (blk * c + pltpu.roll(blk, 64, 1) * s).astype(BF16)

    @pl.when((j == 2) | (j >= 5))
    def _():
        for sl, blk in parts():
            o_ref[:, sl] = blk.astype(BF16)


def _proj_call(h2d, w_in_bf, l, tabs, tm, tiles_per_batch):
    r, d = h2d.shape
    tab_spec = pl.BlockSpec((tm, LANES), lambda i, j: (i % tiles_per_batch, 0))
    return pl.pallas_call(
        _proj_kernel,
        grid=(r // tm, MAIN_WIDTH // SEG),
        in_specs=[pl.BlockSpec((tm, d), lambda i, j: (i, 0)),
                  pl.BlockSpec((None, d, SEG), lambda i, j: (l, 0, j)),
                  tab_spec, tab_spec, tab_spec, tab_spec, tab_spec],
        out_specs=pl.BlockSpec((tm, SEG), lambda i, j: (i, j)),
        out_shape=jax.ShapeDtypeStruct((r, MAIN_WIDTH), BF16),
        compiler_params=_cparams(("parallel", "arbitrary")),
        name="in_proj",
    )(h2d, w_in_bf, tabs["c64"], tabs["sa64"], tabs["sb64"], tabs["c128"], tabs["s128"])


def _rms_gain(x, g_ref):
    ms = jnp.mean(x * x, axis=-1, keepdims=True)
    return x * lax.rsqrt(ms + EPS) * g_ref[...]


def _mla_q_kernel(qa_ref, g_ref, w_ref, c_ref, sa_ref, sb_ref, o_ref):
    qn = _rms_gain(qa_ref[...].astype(F32), g_ref).astype(BF16)
    q = jnp.dot(qn, w_ref[...], preferred_element_type=F32)
    scale = (MLA_NOPE + MLA_ROPE) ** -0.5
    c, sa, sb = c_ref[...] * scale, sa_ref[...] * scale, sb_ref[...] * scale
    for h in range(MLA_HEADS):
        lo = h * MLA_QK_PAD
        o_ref[:, lo:lo + LANES] = (q[:, lo:lo + LANES] * scale).astype(BF16)
        o_ref[:, lo + LANES:lo + 2 * LANES] = _rope64(q[:, lo + LANES:lo + 2 * LANES], c, sa, sb).astype(BF16)


def _mla_kv_kernel(h_ref, wt_ref, g_ref, w_ref, c_ref, sa_ref, sb_ref, k_ref, v_ref):
    t = jnp.dot(h_ref[...], wt_ref[...], preferred_element_type=F32)
    kvn = _rms_gain(t[:, :MLA_KV_RANK], g_ref).astype(BF16)
    kr = _rope64(t[:, MLA_KV_RANK:MLA_KV_RANK + LANES], c_ref[...], sa_ref[...], sb_ref[...]).astype(BF16)
    kv = jnp.dot(kvn, w_ref[...], preferred_element_type=F32)
    kn_width = MLA_HEADS * MLA_NOPE
    for h in range(MLA_HEADS):
        lo = h * MLA_QK_PAD
        k_ref[:, lo:lo + LANES] = kv[:, h * MLA_NOPE:(h + 1) * MLA_NOPE].astype(BF16)
        k_ref[:, lo + LANES:lo + 2 * LANES] = kr
    v_ref[...] = kv[:, kn_width:].astype(BF16)


def _mla_q_call(p2d, q_norm, wq, l, tabs, tm, tiles_per_batch):
    r = p2d.shape[0]
    tab_spec = pl.BlockSpec((tm, LANES), lambda i: (i % tiles_per_batch, 0))
    width = MLA_HEADS * MLA_QK_PAD
    return pl.pallas_call(
        _mla_q_kernel,
        grid=(r // tm,),
        in_specs=[pl.BlockSpec((tm, MLA_Q_RANK), lambda i: (i, (MAIN_WIDTH - MLA_Q_RANK) // MLA_Q_RANK)),
                  pl.BlockSpec((None, 1, MLA_Q_RANK), lambda i: (l, 0, 0)),
                  pl.BlockSpec((None, MLA_Q_RANK, width), lambda i: (l, 0, 0)),
                  tab_spec, tab_spec, tab_spec],
        out_specs=pl.BlockSpec((tm, width), lambda i: (i, 0)),
        out_shape=jax.ShapeDtypeStruct((r, width), BF16),
        compiler_params=_cparams(("parallel",)),
        name="mla_q_proj",
    )(p2d, q_norm, wq, tabs["c64r"], tabs["sa64r"], tabs["sb64r"])


def _mla_kv_call(h2d, w_tail, kv_norm, wkv, l, tabs, tm, tiles_per_batch):
    r, d = h2d.shape
    tab_spec = pl.BlockSpec((tm, LANES), lambda i: (i % tiles_per_batch, 0))
    kwidth = MLA_HEADS * MLA_QK_PAD
    return pl.pallas_call(
        _mla_kv_kernel,
        grid=(r // tm,),
        in_specs=[pl.BlockSpec((tm, d), lambda i: (i, 0)),
                  pl.BlockSpec((None, d, TAIL_WIDTH), lambda i: (l, 0, 0)),
                  pl.BlockSpec((None, 1, MLA_KV_RANK), lambda i: (l, 0, 0)),
                  pl.BlockSpec((None, MLA_KV_RANK, 2 * MLA_WIDTH), lambda i: (l, 0, 0)),
                  tab_spec, tab_spec, tab_spec],
        out_specs=[pl.BlockSpec((tm, kwidth), lambda i: (i, 0)),
                   pl.BlockSpec((tm, MLA_WIDTH), lambda i: (i, 0))],
        out_shape=[jax.ShapeDtypeStruct((r, kwidth), BF16),
                   jax.ShapeDtypeStruct((r, MLA_WIDTH), BF16)],
        compiler_params=_cparams(("parallel",)),
        name="mla_kv_proj",
    )(h2d, w_tail, kv_norm, wkv, tabs["c64r"], tabs["sa64r"], tabs["sb64r"])


KEY_CHUNK = 768


def _softmax_pv(q, k_ref, vo_ref, m_keys, k_cols, v_cols):
    chunk = _pick_tile(m_keys, (KEY_CHUNK, 512, 384, 256, 128))
    m_run = None
    acc = None
    for c0 in range(0, m_keys, chunk):
        s = lax.dot_general(q, k_ref[c0:c0 + chunk, k_cols], _NT, preferred_element_type=F32)
        m_c = jnp.max(s, axis=-1, keepdims=True)
        v_ones = vo_ref[c0:c0 + chunk, v_cols]
        if m_run is None:
            m_run = m_c
            acc = jnp.dot(jnp.exp((s - m_run).astype(BF16)), v_ones, preferred_element_type=F32)
        else:
            m_new = jnp.maximum(m_run, m_c)
            acc = (acc * jnp.exp(m_run - m_new)
                   + jnp.dot(jnp.exp((s - m_new).astype(BF16)), v_ones, preferred_element_type=F32))
            m_run = m_new
    half = acc.shape[1] // 2
    return acc[:, :half] / acc[:, half:]


def _fill_v_ones(v_ref, vo_ref, n_heads, width):
    ones = jnp.ones((v_ref.shape[0], width), BF16)
    for h in range(n_heads):
        vo_ref[:, 2 * width * h:2 * width * h + width] = v_ref[:, width * h:width * (h + 1)]
        vo_ref[:, 2 * width * h + width:2 * width * (h + 1)] = ones


_NT = (((1,), (1,)), ((), ()))


def _diff_attn_kernel(lam_ref, sub_ref, q_ref, k_ref, v_ref, o_ref, vo_ref, *, n_ctx_tiles, ctx_len, lam_init):
    @pl.when(pl.program_id(1) == 0)
    def _():
        _fill_v_ones(v_ref, vo_ref, DIFF_HEADS, HEAD_DIM)

    lv = lam_ref[...]
    lam = (jnp.exp(jnp.sum(lv[0:1] * lv[1:2], axis=-1, keepdims=True))
           - jnp.exp(jnp.sum(lv[2:3] * lv[3:4], axis=-1, keepdims=True)) + lam_init)
    sub = sub_ref[...] * (1.0 - lam_init)

    def run(m_keys):
        for h in range(DIFF_HEADS):
            sl = slice(h * HEAD_DIM, (h + 1) * HEAD_DIM)
            q = q_ref[:, sl]
            v_cols = slice(2 * h * HEAD_DIM, 2 * (h + 1) * HEAD_DIM)
            first = lax.broadcasted_iota(jnp.int32, q.shape, 1) < DIFF_HALF
            zero = jnp.zeros_like(q)
            o = (_softmax_pv(jnp.where(first, q, zero), k_ref, vo_ref, m_keys, sl, v_cols)
                 - lam * _softmax_pv(jnp.where(first, zero, q), k_ref, vo_ref, m_keys, sl, v_cols))
            ms = jnp.mean(o * o, axis=-1, keepdims=True)
            o_ref[:, sl] = (o * lax.rsqrt(ms + EPS) * sub).astype(BF16)

    is_ctx = pl.program_id(1) < n_ctx_tiles

    @pl.when(is_ctx)
    def _():
        run(ctx_len)

    @pl.when(jnp.logical_not(is_ctx))
    def _():
        run(k_ref.shape[0])


def _diff_attn_call(p3d, lamv, subln, lam_init, tq, n_ctx_tiles, ctx_len):
    b, s, _ = p3d.shape
    return pl.pallas_call(
        functools.partial(_diff_attn_kernel, n_ctx_tiles=n_ctx_tiles, ctx_len=ctx_len, lam_init=lam_init),
        grid=(b, s // tq),
        in_specs=[pl.BlockSpec((4, DIFF_HALF), lambda bi, i: (0, 0)),
                  pl.BlockSpec((1, HEAD_DIM), lambda bi, i: (0, 0)),
                  pl.BlockSpec((None, tq, DIFF_WIDTH), lambda bi, i: (bi, i, 0)),
                  pl.BlockSpec((None, s, DIFF_WIDTH), lambda bi, i: (bi, 0, 1)),
                  pl.BlockSpec((None, s, DIFF_WIDTH), lambda bi, i: (bi, 0, 2))],
        out_specs=pl.BlockSpec((None, tq, DIFF_WIDTH), lambda bi, i: (bi, i, 0)),
        out_shape=jax.ShapeDtypeStruct((b, s, DIFF_WIDTH), BF16),
        scratch_shapes=[pltpu.VMEM((s, 2 * DIFF_WIDTH), BF16)],
        compiler_params=_cparams(("parallel", "arbitrary")),
        name="diff_attn",
    )(lamv, subln, p3d, p3d, p3d)


MLA_HEAD_GROUP = 8


def _mla_attn_kernel(q_ref, k_ref, v_ref, o_ref, vo_ref, *, n_ctx_tiles, ctx_len):
    @pl.when(pl.program_id(2) == 0)
    def _():
        _fill_v_ones(v_ref, vo_ref, MLA_HEAD_GROUP, MLA_V)

    def run(m_keys):
        for h in range(MLA_HEAD_GROUP):
            qk = slice(h * MLA_QK_PAD, (h + 1) * MLA_QK_PAD)
            vv = slice(h * MLA_V, (h + 1) * MLA_V)
            v_cols = slice(2 * h * MLA_V, 2 * (h + 1) * MLA_V)
            o_ref[:, vv] = _softmax_pv(q_ref[:, qk], k_ref, vo_ref, m_keys, qk, v_cols).astype(BF16)

    is_ctx = pl.program_id(2) < n_ctx_tiles

    @pl.when(is_ctx)
    def _():
        run(ctx_len)

    @pl.when(jnp.logical_not(is_ctx))
    def _():
        run(k_ref.shape[0])


def _mla_attn_call(qc, kc, vc, tq, n_ctx_tiles, ctx_len):
    b, s, _ = qc.shape
    qw = MLA_HEAD_GROUP * MLA_QK_PAD
    vw = MLA_HEAD_GROUP * MLA_V
    return pl.pallas_call(
        functools.partial(_mla_attn_kernel, n_ctx_tiles=n_ctx_tiles, ctx_len=ctx_len),
        grid=(b, MLA_HEADS // MLA_HEAD_GROUP, s // tq),
        in_specs=[pl.BlockSpec((None, tq, qw), lambda bi, g, i: (bi, i, g)),
                  pl.BlockSpec((None, s, qw), lambda bi, g, i: (bi, 0, g)),
                  pl.BlockSpec((None, s, vw), lambda bi, g, i: (bi, 0, g))],
        out_specs=pl.BlockSpec((None, tq, vw), lambda bi, g, i: (bi, i, g)),
        out_shape=jax.ShapeDtypeStruct((b, s, MLA_WIDTH), BF16),
        scratch_shapes=[pltpu.VMEM((s, 2 * vw), BF16)],
        compiler_params=_cparams(("parallel", "parallel", "arbitrary")),
        name="mla_attn",
    )(qc, kc, vc)


RET_HEAD_GROUP = 4
_TN = (((0,), (0,)), ((), ()))


def _ret_kernel(lg_ref, q_ref, k_ref, v_ref, g_ref, gn_ref, o_ref, intra_ref, rowdec_ref, cdec_ref, st_ref, acc_ref,
                *, chunk, n_ctx_chunks):
    c = chunk
    n_chunks = q_ref.shape[0] // c
    hp = pl.program_id(1)
    row = lax.broadcasted_iota(jnp.int32, (c, c), 0).astype(F32)
    col = lax.broadcasted_iota(jnp.int32, (c, c), 1).astype(F32)
    rel = row - col
    pos = lax.broadcasted_iota(jnp.int32, (c, RET_DK), 0).astype(F32)
    span = jnp.full((RET_DK, RET_DK), float(c), F32)

    for hh in range(RET_HEAD_GROUP):
        lgf = lg_ref[0, hp * RET_HEAD_GROUP + hh]
        lgb = lg_ref[1, hp * RET_HEAD_GROUP + hh]
        intra_ref[hh, 0] = jnp.where(rel >= 0, jnp.exp(jnp.maximum(rel, 0.0) * lgf), 0.0)
        intra_ref[hh, 1] = jnp.where(rel <= 0, jnp.exp(jnp.maximum(-rel, 0.0) * lgb), 0.0)
        rowdec_ref[hh, 0] = jnp.exp((pos + 1.0) * lgf)
        rowdec_ref[hh, 1] = jnp.exp((c - 1.0 - pos) * lgf)
        rowdec_ref[hh, 2] = jnp.exp((c - pos) * lgb)
        rowdec_ref[hh, 3] = jnp.exp(pos * lgb)
        cdec_ref[hh, 0] = jnp.exp(span * lgf)
        cdec_ref[hh, 1] = jnp.exp(span * lgb)

    st_ref[...] = jnp.zeros_like(st_ref)

    def chunk_update(hh, direction, idx):
        sl = slice(hh * RET_DK, (hh + 1) * RET_DK)
        rows = pl.ds(pl.multiple_of(idx * c, c), c)
        qc = q_ref[rows, sl]
        kc = k_ref[rows, sl]
        vc = v_ref[rows, sl]
        state = st_ref[hh, direction]
        att = lax.dot_general(qc, kc, _NT, preferred_element_type=F32) * intra_ref[hh, direction]
        o = (jnp.dot(att.astype(BF16), vc, preferred_element_type=F32)
             + jnp.dot(qc, state.astype(BF16), preferred_element_type=F32) * rowdec_ref[hh, 2 * direction])
        kd = (kc.astype(F32) * rowdec_ref[hh, 2 * direction + 1]).astype(BF16)
        st_ref[hh, direction] = (state * cdec_ref[hh, direction]
                                 + lax.dot_general(kd, vc, _TN, preferred_element_type=F32))
        acc_ref[direction, rows, sl] = o

    def step(t, carry):
        bwd = jnp.where(t < n_ctx_chunks, n_ctx_chunks - 1 - t, n_chunks - 1 - t + n_ctx_chunks)
        for hh in range(RET_HEAD_GROUP):
            chunk_update(hh, 0, t)
            chunk_update(hh, 1, bwd)
        return carry

    lax.fori_loop(0, n_chunks, step, 0)

    for hh in range(RET_HEAD_GROUP):
        sl = slice(hh * RET_DK, (hh + 1) * RET_DK)
        o = acc_ref[0, :, sl] + acc_ref[1, :, sl]
        mu = jnp.mean(o, axis=-1, keepdims=True)
        oc = o - mu
        y = oc * lax.rsqrt(jnp.mean(oc * oc, axis=-1, keepdims=True) + 1e-5) * gn_ref[:, sl]
        g = g_ref[:, sl].astype(F32)
        o_ref[:, sl] = (y * (g * jax.nn.sigmoid(g))).astype(BF16)


def _ret_call(p3d, log_gamma, gn_gain, chunk, n_ctx_chunks):
    b, s, _ = p3d.shape
    w = RET_HEAD_GROUP * RET_DK
    base = 3 * SEG // w

    def col(seg):
        return lambda bi, hp, lg: (bi, 0, base + seg * (SEG // w) + hp)

    grid_spec = pltpu.PrefetchScalarGridSpec(
        num_scalar_prefetch=1,
        grid=(b, RET_HEADS // RET_HEAD_GROUP),
        in_specs=[pl.BlockSpec((None, s, w), col(0)),
                  pl.BlockSpec((None, s, w), col(1)),
                  pl.BlockSpec((None, s, w), col(2)),
                  pl.BlockSpec((None, s, w), col(3)),
                  pl.BlockSpec((1, w), lambda bi, hp, lg: (0, hp))],
        out_specs=pl.BlockSpec((None, s, w), lambda bi, hp, lg: (bi, 0, hp)),
        scratch_shapes=[pltpu.VMEM((RET_HEAD_GROUP, 2, chunk, chunk), F32),
                        pltpu.VMEM((RET_HEAD_GROUP, 4, chunk, RET_DK), F32),
                        pltpu.VMEM((RET_HEAD_GROUP, 2, RET_DK, RET_DK), F32),
                        pltpu.VMEM((RET_HEAD_GROUP, 2, RET_DK, RET_DK), F32),
                        pltpu.VMEM((2, s, w), F32)],
    )
    return pl.pallas_call(
        functools.partial(_ret_kernel, chunk=chunk, n_ctx_chunks=n_ctx_chunks),
        grid_spec=grid_spec,
        out_shape=jax.ShapeDtypeStruct((b, s, RET_WIDTH), BF16),
        compiler_params=_cparams(("parallel", "arbitrary")),
        name="retention",
    )(log_gamma, p3d, p3d, p3d, p3d, gn_gain)


ROW_TILE = 128


def _tile_mod_row(tiles_per_batch, n_ctx_tiles):
    i = pl.program_id(0)
    return jnp.where(i % tiles_per_batch < n_ctx_tiles, CTX_MOD_ROW, i // tiles_per_batch)


def _resident(shape, index_map):
    return pl.BlockSpec(shape, index_map, pipeline_mode=pl.Buffered(1))


def _out_proj_kernel(a_ref, b_ref, c_ref, w_ref, x_ref, g1_ref, gain_ref, sh_ref, sc_ref, xo_ref, h_ref,
                     *, tiles_per_batch, n_ctx_tiles):
    row = pl.ds(_tile_mod_row(tiles_per_batch, n_ctx_tiles), 1)
    acc = jnp.dot(a_ref[...], w_ref[0:DIFF_WIDTH, :], preferred_element_type=F32)
    acc += jnp.dot(b_ref[...], w_ref[DIFF_WIDTH:DIFF_WIDTH + RET_WIDTH, :], preferred_element_type=F32)
    acc += jnp.dot(c_ref[...], w_ref[DIFF_WIDTH + RET_WIDTH:, :], preferred_element_type=F32)
    x_new = x_ref[...] + g1_ref[row, :] * acc
    xo_ref[...] = x_new
    h_ref[...] = _mod_norm(x_new, gain_ref[...], sh_ref[row, :], sc_ref[row, :]).astype(BF16)


def _out_proj_call(a2d, b2d, c2d, w_out, l, x2d, mods_l, gain, tiles_per_batch, n_ctx_tiles):
    r, d = x2d.shape
    tr = ROW_TILE
    rows = lambda w: pl.BlockSpec((tr, w), lambda i: (i, 0))
    mod = lambda blk: pl.BlockSpec((MOD_ROWS, d), lambda i: (0, blk))
    return pl.pallas_call(
        functools.partial(_out_proj_kernel, tiles_per_batch=tiles_per_batch, n_ctx_tiles=n_ctx_tiles),
        grid=(r // tr,),
        in_specs=[rows(DIFF_WIDTH), rows(RET_WIDTH), rows(MLA_WIDTH),
                  _resident((None, d, d), lambda i: (l, 0, 0)),
                  rows(d), mod(2),
                  pl.BlockSpec((1, d), lambda i: (0, 0)), mod(3), mod(4)],
        out_specs=[rows(d), rows(d)],
        out_shape=[jax.ShapeDtypeStruct((r, d), F32), jax.ShapeDtypeStruct((r, d), BF16)],
        compiler_params=_cparams(("arbitrary",)),
        name="out_proj",
    )(a2d, b2d, c2d, w_out, x2d, mods_l, gain, mods_l, mods_l)


def _moe_up_kernel(h_ref, w_ref, b_ref, rw_ref, rb_ref, o_ref, gate_out_ref, gate_ref):
    j = pl.program_id(1)
    n_e = w_ref.shape[0]

    @pl.when(j == 0)
    def _():
        gates = _router_gates(h_ref[...], rw_ref, rb_ref)
        gate_ref[...] = gates
        gate_out_ref[...] = gates

    e_row = lax.broadcasted_iota(jnp.int32, (LANES, n_e * EXPERT_FF), 0)
    e_col = lax.broadcasted_iota(jnp.int32, (LANES, n_e * EXPERT_FF), 1) // EXPERT_FF
    expand = (e_row == j * n_e + e_col).astype(BF16)
    ge = jnp.dot(gate_ref[...].astype(BF16), expand, preferred_element_type=F32)
    h = h_ref[...]
    for e in range(n_e):
        lo = e * 2 * EXPERT_FF
        gu = jnp.dot(h, w_ref[e], preferred_element_type=F32) + b_ref[:, lo:lo + 2 * EXPERT_FF]
        glu = jnp.minimum(gu[:, :EXPERT_FF], SWIGLU_LIMIT)
        lin = jnp.clip(gu[:, EXPERT_FF:], -SWIGLU_LIMIT, SWIGLU_LIMIT)
        act = glu * jax.nn.sigmoid(SWIGLU_ALPHA * glu) * (lin + 1.0)
        sl = slice(e * EXPERT_FF, (e + 1) * EXPERT_FF)
        o_ref[:, sl] = (act * ge[:, sl]).astype(BF16)


def _moe_up_call(h2d, w_gu, b_gu, rw, rb, l, tm):
    r, d = h2d.shape
    tn = SEG
    n_e = tn // (2 * EXPERT_FF)
    n = N_EXPERTS * 2 * EXPERT_FF
    return pl.pallas_call(
        _moe_up_kernel,
        grid=(r // tm, n // tn),
        in_specs=[pl.BlockSpec((tm, d), lambda i, j: (i, 0)),
                  pl.BlockSpec((None, n_e, d, 2 * EXPERT_FF), lambda i, j: (l, j, 0, 0)),
                  pl.BlockSpec((None, 1, tn), lambda i, j: (l, 0, j)),
                  pl.BlockSpec((None, d, LANES), lambda i, j: (l, 0, 0)),
                  pl.BlockSpec((None, 1, LANES), lambda i, j: (l, 0, 0))],
        out_specs=[pl.BlockSpec((tm, tn // 2), lambda i, j: (i, j)),
                   pl.BlockSpec((tm, LANES), lambda i, j: (i, 0))],
        out_shape=[jax.ShapeDtypeStruct((r, n // 2), BF16), jax.ShapeDtypeStruct((r, LANES), F32)],
        scratch_shapes=[pltpu.VMEM((tm, LANES), F32)],
        compiler_params=_cparams(("parallel", "arbitrary")),
        name="moe_up",
    )(h2d, w_gu, b_gu, rw, rb)


def _moe_down_rows(a_ref, w_ref, gates_ref, bd_ref, x_ref, g2_ref, row):
    acc = jnp.dot(a_ref[...], w_ref[...], preferred_element_type=F32)
    acc += jnp.dot(gates_ref[...].astype(BF16), bd_ref[...], preferred_element_type=F32)
    return x_ref[...] + g2_ref[row, :] * acc


def _moe_down_kernel(a_ref, w_ref, gates_ref, bd_ref, x_ref, g2_ref, gain_ref, sh_ref, sc_ref, xo_ref, h_ref,
                     *, tiles_per_batch, n_ctx_tiles):
    row = pl.ds(_tile_mod_row(tiles_per_batch, n_ctx_tiles), 1)
    x_new = _moe_down_rows(a_ref, w_ref, gates_ref, bd_ref, x_ref, g2_ref, row)
    xo_ref[...] = x_new
    h_ref[...] = _mod_norm(x_new, gain_ref[...], sh_ref[row, :], sc_ref[row, :]).astype(BF16)


def _moe_down_final_kernel(a_ref, w_ref, gates_ref, bd_ref, x_ref, g2_ref, gain_ref, o_ref,
                           *, tiles_per_batch, n_ctx_tiles):
    row = pl.ds(_tile_mod_row(tiles_per_batch, n_ctx_tiles), 1)
    x_new = _moe_down_rows(a_ref, w_ref, gates_ref, bd_ref, x_ref, g2_ref, row)
    ms = jnp.mean(x_new * x_new, axis=-1, keepdims=True)
    o_ref[...] = x_new * lax.rsqrt(ms + EPS) * gain_ref[...]


def _moe_down_call(a2d, w_down, gates2d, b_down, l, x2d, mods_l, gain, mods_next, tiles_per_batch, n_ctx_tiles,
                   out_batch=None):
    r, d = x2d.shape
    k = a2d.shape[1]
    tr = ROW_TILE
    rows = lambda w: pl.BlockSpec((tr, w), lambda i: (i, 0))
    mod = lambda blk: pl.BlockSpec((MOD_ROWS, d), lambda i: (0, blk))
    in_specs = [rows(k), _resident((None, k, d), lambda i: (l, 0, 0)), rows(LANES),
                _resident((None, LANES, d), lambda i: (l, 0, 0)), rows(d), mod(5),
                pl.BlockSpec((1, d), lambda i: (0, 0))]
    static = dict(tiles_per_batch=tiles_per_batch, n_ctx_tiles=n_ctx_tiles)
    if out_batch is None:
        return pl.pallas_call(
            functools.partial(_moe_down_kernel, **static),
            grid=(r // tr,),
            in_specs=in_specs + [mod(0), mod(1)],
            out_specs=[rows(d), rows(d)],
            out_shape=[jax.ShapeDtypeStruct((r, d), F32), jax.ShapeDtypeStruct((r, d), BF16)],
            compiler_params=_cparams(("arbitrary",)),
            name="moe_down",
        )(a2d, w_down, gates2d, b_down, x2d, mods_l, gain, mods_next, mods_next)
    latent_tiles = tiles_per_batch - n_ctx_tiles
    return pl.pallas_call(
        functools.partial(_moe_down_final_kernel, **static),
        grid=(r // tr,),
        in_specs=in_specs,
        out_specs=pl.BlockSpec((None, tr, d), lambda i: (i // tiles_per_batch,
                                                          jnp.maximum(i % tiles_per_batch - n_ctx_tiles, 0), 0)),
        out_shape=jax.ShapeDtypeStruct((out_batch, latent_tiles * tr, d), F32),
        compiler_params=_cparams(("arbitrary",)),
        name="moe_down_final",
    )(a2d, w_down, gates2d, b_down, x2d, mods_l, gain)


def _rope_tables(n, ctx_len):
    rows = n // GRID_W
    row = jnp.repeat(jnp.arange(rows, dtype=F32), GRID_W)
    col = jnp.tile(jnp.arange(GRID_W, dtype=F32), rows)

    def cos_sin(rot_dim):
        pairs = rot_dim // 4
        inv = ROPE_BASE ** (-jnp.arange(pairs, dtype=F32) / pairs)
        ang = jnp.concatenate([row[:, None] * inv, col[:, None] * inv], axis=-1)
        cos = jnp.concatenate([jnp.ones((ctx_len, rot_dim // 2), F32), jnp.cos(ang)], axis=0)
        sin = jnp.concatenate([jnp.zeros((ctx_len, rot_dim // 2), F32), jnp.sin(ang)], axis=0)
        return cos, sin

    cos32, sin32 = cos_sin(DIFF_HALF)
    cos64, sin64 = cos_sin(RET_DK)
    z32 = jnp.zeros_like(sin32)
    one64 = jnp.ones((cos32.shape[0], 64), F32)
    z64 = jnp.zeros_like(one64)
    return {
        "c64": jnp.concatenate([cos32] * 4, axis=-1),
        "sa64": jnp.concatenate([-sin32, z32, -sin32, z32], axis=-1),
        "sb64": jnp.concatenate([z32, sin32, z32, sin32], axis=-1),
        "c64r": jnp.concatenate([cos32, cos32, one64], axis=-1),
        "sa64r": jnp.concatenate([-sin32, z32, z64], axis=-1),
        "sb64r": jnp.concatenate([z32, sin32, z64], axis=-1),
        "c128": jnp.concatenate([cos64, cos64], axis=-1),
        "s128": jnp.concatenate([-sin64, sin64], axis=-1),
    }


def _prep_weights(w_in, mla_wq_b, mla_wkv_b, w_out, router_w, router_b,
                  exp_w_gu, exp_b_gu, exp_w_down, exp_b_down):
    depth, d, in_width = w_in.shape
    w_tail = jnp.pad(w_in[:, :, MAIN_WIDTH:], ((0, 0), (0, 0), (0, TAIL_WIDTH - (in_width - MAIN_WIDTH)))).astype(BF16)
    wq = mla_wq_b.reshape(depth, MLA_Q_RANK, MLA_HEADS, MLA_NOPE + MLA_ROPE)
    wq = jnp.pad(wq, ((0, 0), (0, 0), (0, 0), (0, MLA_QK_PAD - MLA_NOPE - MLA_ROPE)))
    wq = wq.reshape(depth, MLA_Q_RANK, MLA_HEADS * MLA_QK_PAD).astype(BF16)
    wkv = mla_wkv_b.reshape(depth, MLA_KV_RANK, MLA_HEADS, MLA_NOPE + MLA_V)
    wkv = jnp.concatenate([wkv[..., :MLA_NOPE].reshape(depth, MLA_KV_RANK, -1),
                           wkv[..., MLA_NOPE:].reshape(depth, MLA_KV_RANK, -1)], axis=-1).astype(BF16)
    rw = jnp.pad(router_w, ((0, 0), (0, 0), (0, LANES - N_EXPERTS))).astype(BF16)
    rb = jnp.concatenate([router_b, jnp.full((depth, LANES - N_EXPERTS), NEG_BIG, F32)], axis=-1)
    return dict(
        w_in=w_in.astype(BF16), w_tail=w_tail, wq=wq, wkv=wkv, w_out=w_out.astype(BF16),
        rw=rw, rb=rb.reshape(depth, 1, LANES),
        w_gu=exp_w_gu.astype(BF16),
        b_gu=exp_b_gu.reshape(depth, 1, N_EXPERTS * 2 * EXPERT_FF),
        w_down=exp_w_down.astype(BF16).reshape(depth, N_EXPERTS * EXPERT_FF, d),
        b_down=jnp.pad(exp_b_down, ((0, 0), (0, LANES - N_EXPERTS), (0, 0))).astype(BF16))


def kernel(x, c, ctx, c_ctx, ada_w, ada_b, norm1, norm2, w_in, diff_lq1, diff_lk1, diff_lq2, diff_lk2, diff_subln, ret_decay, ret_norm, mla_q_norm, mla_kv_norm, mla_wq_b, mla_wkv_b, w_out, router_w, router_b, exp_w_gu, exp_b_gu, exp_w_down, exp_b_down, final_norm):
    batch, n, d = x.shape
    ctx_len = ctx.shape[1]
    depth = ada_w.shape[0]
    s = ctx_len + n
    r = batch * s
    assert d == D_MODEL and batch <= CTX_MOD_ROW and ctx_len % RET_CHUNK == 0 and n % RET_CHUNK == 0
    assert depth >= 1 and RET_CHUNK % ROW_TILE == 0

    tq = _pick_tile(ctx_len, (256, 128))
    n_ctx_tiles = ctx_len // tq
    tm = _pick_tile(s, (768, 576, 384, 256, 128), at_least=ctx_len)
    tm_mla = _pick_tile(s, (768, 384, 256, 128))
    tiles_per_batch = s // tm
    ret_chunk = _pick_tile(math.gcd(ctx_len, n), (2 * RET_CHUNK, RET_CHUNK))

    mods = _ada_tables(_stack_conditioning(c, c_ctx), ada_w, ada_b)
    tabs = _rope_tables(n, ctx_len)
    log_gamma = jnp.log1p(-jnp.exp2(-ret_decay.astype(F32)))

    wts = _prep_weights(w_in, mla_wq_b, mla_wkv_b, w_out, router_w, router_b,
                        exp_w_gu, exp_b_gu, exp_w_down, exp_b_down)
    q_norm = mla_q_norm.reshape(depth, 1, -1)
    kv_norm = mla_kv_norm.reshape(depth, 1, -1)

    n_row_tiles = s // ROW_TILE
    n_ctx_row_tiles = ctx_len // ROW_TILE
    xs, h = _norm_call(ctx, x, norm1[0].reshape(1, d), mods[0], 0, 1, n_ctx_tiles, tq)
    x2d = xs.reshape(r, d)
    h2d = h.reshape(r, d)

    for l in range(depth):
        lam_init = 0.8 - 0.6 * math.exp(-0.3 * l)
        mods_l = mods[l]
        p2d = _proj_call(h2d, wts["w_in"], l, tabs, tm, tiles_per_batch)
        p3d = p2d.reshape(batch, s, MAIN_WIDTH)
        qc = _mla_q_call(p2d, q_norm, wts["wq"], l, tabs, tm_mla, s // tm_mla)
        kc, vc = _mla_kv_call(h2d, wts["w_tail"], kv_norm, wts["wkv"], l, tabs, tm_mla, s // tm_mla)
        lamv = jnp.stack([diff_lq1[l], diff_lk1[l], diff_lq2[l], diff_lk2[l]]).astype(F32)
        a = _diff_attn_call(p3d, lamv, diff_subln[l].reshape(1, -1), lam_init, tq, n_ctx_tiles, ctx_len)
        bb = _ret_call(p3d, log_gamma[l], ret_norm[l].reshape(1, -1), ret_chunk, ctx_len // ret_chunk)
        cc = _mla_attn_call(qc.reshape(batch, s, -1), kc.reshape(batch, s, -1), vc.reshape(batch, s, -1),
                            tq, n_ctx_tiles, ctx_len)
        x2d, h2 = _out_proj_call(a.reshape(r, -1), bb.reshape(r, -1), cc.reshape(r, -1), wts["w_out"], l, x2d,
                                 mods_l, norm2[l].reshape(1, d), n_row_tiles, n_ctx_row_tiles)
        act, gates2d = _moe_up_call(h2, wts["w_gu"], wts["b_gu"], wts["rw"], wts["rb"], l, tm)
        if l + 1 < depth:
            x2d, h2d = _moe_down_call(act, wts["w_down"], gates2d, wts["b_down"], l, x2d, mods_l,
                                      norm1[l + 1].reshape(1, d), mods[l + 1], n_row_tiles, n_ctx_row_tiles)
        else:
            return _moe_down_call(act, wts["w_down"], gates2d, wts["b_down"], l, x2d, mods_l,
                                  final_norm.reshape(1, d), None, n_row_tiles, n_ctx_row_tiles, out_batch=batch)
```
